```python
import math
import jax, jax.numpy as jnp
from jax import lax
import numpy as np

D_MODEL = 4096
BATCH = 2
SEQ = 4096
DEPTH = 2

HEAD_DIM = 128
W_A = 3 * D_MODEL // 8
W_B = D_MODEL // 4
W_C = 3 * D_MODEL // 8
H_A = W_A // HEAD_DIM
G_B = W_B // HEAD_DIM
H_C = W_C // HEAD_DIM
DK_C = HEAD_DIM // 2
CHUNK = 128
Q_BLOCK = 128
CONV_W = 4
N_BUCKETS = 32
MAX_DIST = 128
N_BRANCH = 3
D_FF = ((8 * D_MODEL // 3 + 255) // 256) * 256
N_EXPERTS = 8
TOP_K = 2
D_FF_EXPERT = 7 * D_MODEL // 8
N_DENSE = (DEPTH + 1) // 2
N_MOE = DEPTH // 2
EPS = 1e-6
IN_SIZES = (W_A, W_A, W_A, W_A, H_A, H_A, 2 * W_B, W_C, W_C, W_C, N_BRANCH * D_MODEL)
N_IN = sum(IN_SIZES)

kernel_name = 'hybrid_mlstm_gmlp_diffattn_moe'


def rms_norm(x, g):
    xf = x.astype(jnp.float32)
    xf = xf * lax.rsqrt(jnp.mean(xf * xf, axis=-1, keepdims=True) + EPS)
    return (xf * g.astype(jnp.float32)).astype(x.dtype)


def layer_norm(x, g, b):
    xf = x.astype(jnp.float32)
    mu = jnp.mean(xf, axis=-1, keepdims=True)
    xc = xf - mu
    xf = xc * lax.rsqrt(jnp.mean(xc * xc, axis=-1, keepdims=True) + EPS)
    return (xf * g.astype(jnp.float32) + b.astype(jnp.float32)).astype(x.dtype)


def split_columns(z):
    out, start = [], 0
    for s in IN_SIZES:
        out.append(z[..., start:start + s])
        start += s
    return out


def to_heads(t, n_heads):
    B, S, _ = t.shape
    return jnp.transpose(t.reshape(B, S, n_heads, -1), (0, 2, 1, 3))


def causal_dwconv(x, w):
    S = x.shape[1]
    xp = jnp.pad(x, ((0, 0), (CONV_W - 1, 0), (0, 0)))
    y = xp[:, 0:S] * w[0]
    for j in range(1, CONV_W):
        y = y + xp[:, j:j + S] * w[j]
    return y


def mlstm_chunkwise(q, k, v, i_pre, f_pre):
    B, H, S, DH = q.shape
    nc = S // CHUNK

    def chunks(t):
        return t.reshape(B, H, nc, CHUNK, *t.shape[3:])

    q, k, v = chunks(q), chunks(k), chunks(v)
    i_c = chunks(i_pre)
    b = jnp.cumsum(jax.nn.log_sigmoid(chunks(f_pre)), axis=-1)
    g = b[..., -1]
    a = g[..., None] - b + i_c

    def step(carry, inp):
        c, n, m = carry
        g_c, a_c, k_c, v_c = inp
        m_new = jnp.maximum(g_c + m, jnp.max(a_c, axis=-1))
        w = jnp.exp(a_c - m_new[..., None])
        decay = jnp.exp(g_c + m - m_new)
        c_new = decay[..., None, None] * c + jnp.einsum('bhl,bhlk,bhlv->bhkv', w, k_c, v_c)
        n_new = decay[..., None] * n + jnp.einsum('bhl,bhlk->bhk', w, k_c)
        return (c_new, n_new, m_new), (c, n, m)

    xs = tuple(jnp.moveaxis(t, 2, 0) for t in (g, a, k, v))
    init = (jnp.zeros((B, H, DH, DH), jnp.float32),
            jnp.zeros((B, H, DH), jnp.float32),
            jnp.zeros((B, H), jnp.float32))
    _, (c_prev, n_prev, m_prev) = lax.scan(step, init, xs)
    c_prev = jnp.moveaxis(c_prev, 0, 2)
    n_prev = jnp.moveaxis(n_prev, 0, 2)
    m_prev = jnp.moveaxis(m_prev, 0, 2)

    causal = jnp.tril(jnp.ones((CHUNK, CHUNK), dtype=bool))
    log_d = jnp.where(causal, b[..., :, None] - b[..., None, :] + i_c[..., None, :], -jnp.inf)
    log_inter = b + m_prev[..., None]
    m_t = jnp.maximum(log_inter, jnp.max(log_d, axis=-1))
    w = jnp.exp(log_d - m_t[..., None]) * jnp.einsum('bhctd,bhcsd->bhcts', q, k)
    e_inter = jnp.exp(log_inter - m_t)
    num = (e_inter[..., None] * jnp.einsum('bhctk,bhckv->bhctv', q, c_prev)
           + jnp.einsum('bhcts,bhcsv->bhctv', w, v))
    den = e_inter * jnp.einsum('bhctk,bhck->bhct', q, n_prev) + jnp.sum(w, axis=-1)
    h = num / jnp.maximum(jnp.abs(den), jnp.exp(-m_t))[..., None]
    return h.reshape(B, H, S, DH)


def spatial_gating(bz, ln_g, ln_b, w_s, b_s):
    B, S, _ = bz.shape
    z = jax.nn.gelu(bz)
    u, v = z[..., :W_B], z[..., W_B:]
    v = layer_norm(v, ln_g, ln_b)
    nc = S // CHUNK
    v = v.reshape(B, nc, CHUNK, G_B, HEAD_DIM)
    causal = jnp.tril(jnp.ones((CHUNK, CHUNK), dtype=bool))
    w = jnp.where(causal, w_s, jnp.zeros_like(w_s))
    s = jnp.einsum('gts,bcsgd->bctgd', w, v) + jnp.transpose(b_s)[:, :, None]
    return u * s.reshape(B, S, W_B)


def t5_bucket(dist):
    n = jnp.maximum(dist, 0)
    max_exact = N_BUCKETS // 2
    nf = jnp.maximum(n, 1).astype(jnp.float32)
    large = max_exact + (jnp.log(nf / max_exact) / math.log(MAX_DIST / max_exact)
                         * (N_BUCKETS - max_exact)).astype(jnp.int32)
    large = jnp.minimum(large, N_BUCKETS - 1)
    return jnp.where(n < max_exact, n, large)


def diff_attention(q1, q2, k1, k2, v, lam, rel_bias):
    B, H, S, Dk = q1.shape
    nb = S // Q_BLOCK
    scale = Dk ** -0.5
    vf = v.astype(jnp.float32)
    kpos = jnp.arange(S, dtype=jnp.int32)

    def to_blocks(t):
        return jnp.moveaxis(t.reshape(B, H, nb, Q_BLOCK, t.shape[-1]), 2, 0)

    def one_block(args):
        q1b, q2b, start = args
        qpos = start + jnp.arange(Q_BLOCK, dtype=jnp.int32)
        dist = qpos[:, None] - kpos[None, :]
        bias = jnp.transpose(rel_bias[t5_bucket(dist)], (2, 0, 1)).astype(jnp.float32)
        causal = dist >= 0

        def probs(qb, kk):
            s = jnp.einsum('bhqd,bhkd->bhqk', qb, kk).astype(jnp.float32) * scale + bias
            return jax.nn.softmax(jnp.where(causal, s, -jnp.inf), axis=-1)

        a = probs(q1b, k1) - lam * probs(q2b, k2)
        return jnp.einsum('bhqk,bhkd->bhqd', a, vf)

    starts = jnp.arange(nb, dtype=jnp.int32) * Q_BLOCK
    out = lax.map(one_block, (to_blocks(q1), to_blocks(q2), starts))
    return jnp.moveaxis(out, 0, 2).reshape(B, H, S, -1)


def hybrid_mixer(h, layer_idx, w_in, a_gate_bias, a_conv, a_norm, b_ln_g, b_ln_b, b_ws, b_bs,
                 c_lambda, c_norm, rel_bias, w_br_a, w_br_b, w_br_c, w_out):
    B, S, _ = h.shape
    f32 = jnp.float32
    z = h @ w_in
    aq, ak, av, ao, ai, af, bz, cq, ck, cv, gpre = split_columns(z)

    qk = jax.nn.silu(causal_dwconv(jnp.concatenate([aq, ak], axis=-1), a_conv))
    q = to_heads(qk[..., :W_A], H_A).astype(f32)
    k = to_heads(qk[..., W_A:], H_A).astype(f32) * (HEAD_DIM ** -0.5)
    v = to_heads(av, H_A).astype(f32)
    gb = a_gate_bias.astype(f32)
    i_pre = jnp.transpose(ai.astype(f32) + gb[:H_A], (0, 2, 1))
    f_pre = jnp.transpose(af.astype(f32) + gb[H_A:], (0, 2, 1))
    h_cell = jnp.transpose(mlstm_chunkwise(q, k, v, i_pre, f_pre), (0, 2, 1, 3))
    h_cell = jax.nn.sigmoid(ao.astype(f32)).reshape(B, S, H_A, HEAD_DIM) * h_cell
    y_a = rms_norm(h_cell, a_norm.reshape(H_A, HEAD_DIM)).reshape(B, S, W_A).astype(h.dtype)

    y_b = spatial_gating(bz, b_ln_g, b_ln_b, b_ws, b_bs)

    qc, kc, vc = to_heads(cq, H_C), to_heads(ck, H_C), to_heads(cv, H_C)
    lam_init = 0.8 - 0.6 * math.exp(-0.3 * layer_idx)
    lf = c_lambda.astype(f32)
    lam = jnp.exp(jnp.sum(lf[0] * lf[1])) - jnp.exp(jnp.sum(lf[2] * lf[3])) + lam_init
    o = diff_attention(qc[..., :DK_C], qc[..., DK_C:], kc[..., :DK_C], kc[..., DK_C:], vc, lam, rel_bias)
    o = rms_norm(o, c_norm) * (1.0 - lam_init)
    y_c = jnp.transpose(o, (0, 2, 1, 3)).reshape(B, S, W_C).astype(h.dtype)

    gates = jax.nn.sigmoid(gpre).reshape(B, S, N_BRANCH, D_MODEL)
    merged = (gates[:, :, 0] * (y_a @ w_br_a)
              + gates[:, :, 1] * (y_b @ w_br_b)
              + gates[:, :, 2] * (y_c @ w_br_c))
    return merged @ w_out


def swiglu(t, wg, wu, wd):
    return (jax.nn.silu(t @ wg) * (t @ wu)) @ wd


def moe_swiglu(h, router, wg, wu, wd):
    B, S, D = h.shape
    t = h.reshape(-1, D)
    logits = (t @ router).astype(jnp.float32)
    top_vals, top_idx = lax.top_k(logits, TOP_K)
    top_w = jax.nn.softmax(top_vals, axis=-1)
    combine = jnp.sum(jax.nn.one_hot(top_idx, N_EXPERTS, dtype=jnp.float32) * top_w[..., None], axis=1)
    out = jnp.zeros(t.shape, jnp.float32)
    for e in range(N_EXPERTS):
        out = out + combine[:, e:e + 1] * swiglu(t, wg[e], wu[e], wd[e]).astype(jnp.float32)
    return out.astype(h.dtype).reshape(B, S, D)


def setup_inputs(seed: int = 0) -> dict:
    key = jax.random.key(seed)
    ks = jax.random.split(key, 32)
    f32 = jnp.float32

    def nrm(k, shape, scale):
        return jax.random.normal(k, shape, f32) * scale

    def gain(k, shape):
        return 1.0 + nrm(k, shape, 0.02)

    i_bias = nrm(ks[3], (DEPTH, H_A), 0.1)
    f_bias = jnp.linspace(3.0, 6.0, H_A, dtype=f32)[None, :] + nrm(ks[4], (DEPTH, H_A), 0.02)
    return {
        'x': nrm(ks[0], (BATCH, SEQ, D_MODEL), 1.0),
        'norm_mix': gain(ks[1], (DEPTH, D_MODEL)),
        'w_in': nrm(ks[2], (DEPTH, D_MODEL, N_IN), D_MODEL ** -0.5),
        'a_gate_bias': jnp.concatenate([i_bias, f_bias], axis=-1),
        'a_conv': nrm(ks[5], (DEPTH, CONV_W, 2 * W_A), CONV_W ** -0.5),
        'a_norm': gain(ks[6], (DEPTH, W_A)),
        'b_ln_g': gain(ks[7], (DEPTH, W_B)),
        'b_ln_b': nrm(ks[8], (DEPTH, W_B), 0.02),
        'b_ws': nrm(ks[9], (DEPTH, G_B, CHUNK, CHUNK), CHUNK ** -0.5),
        'b_bs': gain(ks[10], (DEPTH, G_B, CHUNK)),
        'c_lambda': nrm(ks[11], (DEPTH, 4, DK_C), 0.1),
        'c_norm': gain(ks[12], (DEPTH, HEAD_DIM)),
        'rel_bias': nrm(ks[13], (N_BUCKETS, H_C), 0.5),
        'w_br_a': nrm(ks[14], (DEPTH, W_A, D_MODEL), W_A ** -0.5),
        'w_br_b': nrm(ks[15], (DEPTH, W_B, D_MODEL), W_B ** -0.5),
        'w_br_c': nrm(ks[16], (DEPTH, W_C, D_MODEL), W_C ** -0.5),
        'w_out': nrm(ks[17], (DEPTH, D_MODEL, D_MODEL), D_MODEL ** -0.5),
        'norm_ffn': gain(ks[18], (DEPTH, D_MODEL)),
        'ffn_wg': nrm(ks[19], (N_DENSE, D_MODEL, D_FF), D_MODEL ** -0.5),
        'ffn_wu': nrm(ks[20], (N_DENSE, D_MODEL, D_FF), D_MODEL ** -0.5),
        'ffn_wd': nrm(ks[21], (N_DENSE, D_FF, D_MODEL), D_FF ** -0.5),
        'router': nrm(ks[22], (N_MOE, D_MODEL, N_EXPERTS), D_MODEL ** -0.5),
        'moe_wg': nrm(ks[23], (N_MOE, N_EXPERTS, D_MODEL, D_FF_EXPERT), D_MODEL ** -0.5),
        'moe_wu': nrm(ks[24], (N_MOE, N_EXPERTS, D_MODEL, D_FF_EXPERT), D_MODEL ** -0.5),
        'moe_wd': nrm(ks[25], (N_MOE, N_EXPERTS, D_FF_EXPERT, D_MODEL), D_FF_EXPERT ** -0.5),
        'final_norm': gain(ks[26], (D_MODEL,)),
    }


def reference(x, norm_mix, w_in, a_gate_bias, a_conv, a_norm, b_ln_g, b_ln_b, b_ws, b_bs,
              c_lambda, c_norm, rel_bias, w_br_a, w_br_b, w_br_c, w_out, norm_ffn,
              ffn_wg, ffn_wu, ffn_wd, router, moe_wg, moe_wu, moe_wd, final_norm):
    h = x
    for li in range(DEPTH):
        h = h + hybrid_mixer(rms_norm(h, norm_mix[li]), li, w_in[li], a_gate_bias[li], a_conv[li],
                             a_norm[li], b_ln_g[li], b_ln_b[li], b_ws[li], b_bs[li], c_lambda[li],
                             c_norm[li], rel_bias, w_br_a[li], w_br_b[li], w_br_c[li], w_out[li])
        hn = rms_norm(h, norm_ffn[li])
        if li % 2 == 0:
            j = li // 2
            h = h + swiglu(hn, ffn_wg[j], ffn_wu[j], ffn_wd[j])
        else:
            j = li // 2
            h = h + moe_swiglu(hn, router[j], moe_wg[j], moe_wu[j], moe_wd[j])
    return rms_norm(h, final_norm)
```

```python
import functools
import math

import numpy as np
import jax
import jax.numpy as jnp
from jax import lax
from jax.experimental import pallas as pl
from jax.experimental.pallas import tpu as pltpu

F32 = jnp.float32
BF16 = jnp.bfloat16

EPS = 1e-6
HEAD_DIM = 128
CHUNK = 128
CONV_W = 4
N_BUCKETS = 32
MAX_DIST = 128
TOP_K = 2
LANES = 128
VMEM_BYTES_V7X = 64 * 1024 * 1024
VMEM_SLACK = 6 * 1024 * 1024


def _divisor(n, candidates):
    for c in candidates:
        if n % c == 0:
            return c
    raise ValueError(f"no tile size in {candidates} divides {n}")


def _params(semantics, block_bytes, scratch_bytes=0):
    need = 2 * block_bytes + scratch_bytes + VMEM_SLACK
    limit = int(min(max(need, 16 * 1024 * 1024), VMEM_BYTES_V7X - 4 * 1024 * 1024))
    return pltpu.CompilerParams(dimension_semantics=semantics, vmem_limit_bytes=limit)


def _nbytes(shape, dtype):
    return int(np.prod(shape)) * jnp.dtype(dtype).itemsize


def _sigmoid(x):
    return 1.0 / (1.0 + jnp.exp(-x))


def _split_bf16(x):
    hi = x.astype(BF16)
    lo = (x - hi.astype(F32)).astype(BF16)
    return hi, lo


def _norm_kernel(*refs, has_proj, out_dtype):
    if has_proj:
        x_ref, g_ref, whi_ref, wlo_ref, y_ref, p_ref = refs
    else:
        x_ref, g_ref, y_ref = refs
    x = x_ref[...]
    y = x * lax.rsqrt(jnp.mean(x * x, axis=-1, keepdims=True) + EPS) * g_ref[...]
    y_ref[...] = y.astype(out_dtype)
    if has_proj:
        y_hi, y_lo = _split_bf16(y)
        w_hi = whi_ref[...]
        p = jnp.dot(y_hi, w_hi, preferred_element_type=F32)
        p += jnp.dot(y_hi, wlo_ref[...], preferred_element_type=F32)
        p += jnp.dot(y_lo, w_hi, preferred_element_type=F32)
        p_ref[...] = p


def rmsnorm(x, g, proj=None, out_dtype=BF16):
    T, D = x.shape
    tm = _divisor(T, (256, 128, 8))
    g2 = g.reshape(1, D).astype(F32)
    in_specs = [pl.BlockSpec((tm, D), lambda i: (i, 0)), pl.BlockSpec((1, D), lambda i: (0, 0))]
    out_shape = [jax.ShapeDtypeStruct((T, D), out_dtype)]
    out_specs = [pl.BlockSpec((tm, D), lambda i: (i, 0))]
    args = [x, g2]
    blk = _nbytes((tm, D), F32) + _nbytes((tm, D), out_dtype)
    if proj is not None:
        n = proj.shape[1]
        assert n <= LANES
        w = jnp.pad(proj.astype(F32), ((0, 0), (0, LANES - n)))
        w_hi, w_lo = _split_bf16(w)
        in_specs += [pl.BlockSpec((D, LANES), lambda i: (0, 0))] * 2
        out_shape.append(jax.ShapeDtypeStruct((T, LANES), F32))
        out_specs.append(pl.BlockSpec((tm, LANES), lambda i: (i, 0)))
        args += [w_hi, w_lo]
        blk += 2 * _nbytes((D, LANES), BF16)
    outs = pl.pallas_call(
        functools.partial(_norm_kernel, has_proj=proj is not None, out_dtype=out_dtype),
        grid=(T // tm,),
        in_specs=in_specs,
        out_specs=out_specs,
        out_shape=out_shape,
        compiler_params=_params(("parallel",), blk, 4 * _nbytes((tm, D), F32)),
        name="rmsnorm",
    )(*args)
    return (outs[0], outs[1]) if proj is not None else outs[0]


def _matmul_kernel(a_ref, w_ref, o_ref):
    o_ref[...] = jnp.dot(a_ref[...], w_ref[...], preferred_element_type=F32).astype(o_ref.dtype)


def matmul(a, w, out_dtype=BF16):
    T, K = a.shape
    N = w.shape[1]
    tm = _divisor(T, (1024, 512, 256, 128))
    tn = _divisor(N, (512, 256, 128))
    blk = _nbytes((tm, K), BF16) + _nbytes((K, tn), BF16) + _nbytes((tm, tn), out_dtype)
    return pl.pallas_call(
        _matmul_kernel,
        grid=(T // tm, N // tn),
        in_specs=[pl.BlockSpec((tm, K), lambda i, j: (i, 0)), pl.BlockSpec((K, tn), lambda i, j: (0, j))],
        out_specs=pl.BlockSpec((tm, tn), lambda i, j: (i, j)),
        out_shape=jax.ShapeDtypeStruct((T, N), out_dtype),
        compiler_params=_params(("parallel", "parallel"), blk, _nbytes((tm, tn), F32)),
        name="matmul",
    )(a, w)


def _matmul_res_kernel(a_ref, w_ref, r_ref, o_ref):
    k = pl.program_id(2)
    d = jnp.dot(a_ref[...], w_ref[...], preferred_element_type=F32)

    @pl.when(k == 0)
    def _():
        o_ref[...] = r_ref[...] + d

    @pl.when(k > 0)
    def _():
        o_ref[...] += d


def matmul_residual(a, w, res):
    T, K = a.shape
    N = w.shape[1]
    tm = _divisor(T, (512, 256, 128))
    tn = _divisor(N, (512, 256, 128))
    nk = next(n for n in range(1, K // LANES + 1) if K % (n * LANES) == 0 and K // n <= 6144)
    tk = K // nk
    blk = _nbytes((tm, tk), BF16) + _nbytes((tk, tn), BF16) + 2 * _nbytes((tm, tn), F32)
    return pl.pallas_call(
        _matmul_res_kernel,
        grid=(T // tm, N // tn, nk),
        in_specs=[pl.BlockSpec((tm, tk), lambda i, j, k: (i, k)),
                  pl.BlockSpec((tk, tn), lambda i, j, k: (k, j)),
                  pl.BlockSpec((tm, tn), lambda i, j, k: (i, j))],
        out_specs=pl.BlockSpec((tm, tn), lambda i, j, k: (i, j)),
        out_shape=jax.ShapeDtypeStruct((T, N), F32),
        compiler_params=_params(("parallel", "parallel", "arbitrary"), blk, _nbytes((tm, tn), F32)),
        name="matmul_residual",
    )(a, w, res)


def _swiglu_up_kernel(x_ref, wg_ref, wu_ref, o_ref):
    x = x_ref[...]
    g = jnp.dot(x, wg_ref[...], preferred_element_type=F32)
    u = jnp.dot(x, wu_ref[...], preferred_element_type=F32)
    o_ref[...] = (g * _sigmoid(g) * u).astype(o_ref.dtype)


def swiglu_up(x, wg, wu):
    T, K = x.shape
    F = wg.shape[1]
    tm = _divisor(T, (1024, 512, 256, 128))
    tn = _divisor(F, (512, 256, 128))
    blk = _nbytes((tm, K), BF16) + 2 * _nbytes((K, tn), BF16) + _nbytes((tm, tn), BF16)
    return pl.pallas_call(
        _swiglu_up_kernel,
        grid=(T // tm, F // tn),
        in_specs=[pl.BlockSpec((tm, K), lambda i, j: (i, 0)),
                  pl.BlockSpec((K, tn), lambda i, j: (0, j)),
                  pl.BlockSpec((K, tn), lambda i, j: (0, j))],
        out_specs=pl.BlockSpec((tm, tn), lambda i, j: (i, j)),
        out_shape=jax.ShapeDtypeStruct((T, F), BF16),
        compiler_params=_params(("parallel", "parallel"), blk, 3 * _nbytes((tm, tn), F32)),
        name="swiglu_up",
    )(x, wg, wu)


def _dot_nt(a, b):
    return lax.dot_general(a, b, (((1,), (1,)), ((), ())), preferred_element_type=F32)


def _dot_tn(a, b):
    return lax.dot_general(a, b, (((0,), (0,)), ((), ())), preferred_element_type=F32)


def _mlstm_kernel(gb_ref, q_ref, k_ref, v_ref, og_ref, ig_ref, fg_ref, cwq_ref, cwk_ref, an_ref, y_ref,
                  c_scr, n_scr, m_scr, pq_scr, pk_scr, *, n_heads, n_chunks):
    h = pl.program_id(1)
    L = CHUNK
    c_scr[...] = jnp.zeros_like(c_scr)
    n_scr[...] = jnp.zeros_like(n_scr)
    m_scr[...] = jnp.zeros_like(m_scr)
    pq_scr[...] = jnp.zeros_like(pq_scr)
    pk_scr[...] = jnp.zeros_like(pk_scr)
    gb_i = gb_ref[h]
    gb_f = gb_ref[n_heads + h]
    cwq = cwq_ref[...]
    cwk = cwk_ref[...]
    a_norm = an_ref[...]
    ri = lax.broadcasted_iota(jnp.int32, (L, L), 0)
    ci = lax.broadcasted_iota(jnp.int32, (L, L), 1)
    lower = ci <= ri
    strict_lower01 = jnp.where(ri > ci, 1.0, 0.0).astype(BF16)

    def conv_silu(x, prev, w):
        y = x * w[CONV_W - 1:CONV_W, :]
        for s in range(1, CONV_W):
            shifted = jnp.where(ri < s, pltpu.roll(prev, s, 0), pltpu.roll(x, s, 0))
            y = y + shifted * w[CONV_W - 1 - s:CONV_W - s, :]
        return y * _sigmoid(y)

    def chunk(c, carry):
        rows = pl.ds(pl.multiple_of(c * L, L), L)
        xq = q_ref[rows, :].astype(F32)
        xk = k_ref[rows, :].astype(F32)
        q = conv_silu(xq, pq_scr[...], cwq)
        k = conv_silu(xk, pk_scr[...], cwk) * (HEAD_DIM ** -0.5)
        pq_scr[...] = xq
        pk_scr[...] = xk
        v = v_ref[rows, :]
        q_bf = q.astype(BF16)
        k_bf = k.astype(BF16)

        i_row = ig_ref[0, 0, pl.ds(c, 1), :] + gb_i
        f_row = fg_ref[0, 0, pl.ds(c, 1), :] + gb_f
        lf = jnp.minimum(f_row, 0.0) - jnp.log(1.0 + jnp.exp(-jnp.abs(f_row)))
        lf_low = jnp.where(lower, lf, 0.0)
        b_col = jnp.sum(lf_low, axis=1, keepdims=True)
        hi, rest = _split_bf16_f32(lf_low)
        mid, lo = _split_bf16_f32(rest)
        dmat = (jnp.dot(hi, strict_lower01, preferred_element_type=F32)
                + jnp.dot(mid, strict_lower01, preferred_element_type=F32)
                + jnp.dot(lo.astype(BF16), strict_lower01, preferred_element_type=F32))
        g_tot = b_col[L - 1:L, :]
        a_row = dmat[L - 1:L, :] + i_row
        a_col = jnp.sum(jnp.where(ci > ri, lf, 0.0) + jnp.where(ci == ri, i_row, 0.0),
                        axis=1, keepdims=True)

        m_prev = m_scr[...]
        c_prev = c_scr[...]
        n_prev = n_scr[...]

        log_d = jnp.where(lower, dmat + i_row, -jnp.inf)
        log_inter = b_col + m_prev
        m_t = jnp.maximum(log_inter, jnp.max(log_d, axis=1, keepdims=True))
        w = jnp.exp(log_d - m_t) * _dot_nt(q_bf, k_bf)
        e_inter = jnp.exp(log_inter - m_t)
        num = (e_inter * jnp.dot(q_bf, c_prev.astype(BF16), preferred_element_type=F32)
               + jnp.dot(w.astype(BF16), v, preferred_element_type=F32))
        den = e_inter * jnp.sum(q * n_prev, axis=1, keepdims=True) + jnp.sum(w, axis=1, keepdims=True)
        hcell = num / jnp.maximum(jnp.abs(den), jnp.exp(-m_t))

        gated = _sigmoid(og_ref[rows, :].astype(F32)) * hcell
        y = gated * lax.rsqrt(jnp.mean(gated * gated, axis=-1, keepdims=True) + EPS) * a_norm
        y_ref[rows, :] = y.astype(y_ref.dtype)

        m_new = jnp.maximum(g_tot + m_prev, jnp.max(a_row, axis=1, keepdims=True))
        w_col = jnp.exp(a_col - m_new)
        decay = jnp.exp(g_tot + m_prev - m_new)
        c_scr[...] = decay * c_prev + _dot_tn(k_bf, (v.astype(F32) * w_col).astype(BF16))
        n_scr[...] = decay * n_prev + jnp.sum(k * w_col, axis=0, keepdims=True)
        m_scr[...] = m_new
        return carry

    lax.fori_loop(0, n_chunks, chunk, 0)


def _split_bf16_f32(x):
    hi = x.astype(BF16)
    return hi, x - hi.astype(F32)


def mlstm_branch(z, zg, a_gate_bias, a_conv, a_norm, batch, seq, n_heads, cols):
    T = batch * seq
    nc = seq // CHUNK
    W = n_heads * HEAD_DIM

    def gate_rows(g):
        return jnp.transpose(g.reshape(batch, seq, n_heads), (0, 2, 1)).reshape(batch, n_heads, nc, CHUNK)

    ig = gate_rows(zg[:, :n_heads])
    fg = gate_rows(zg[:, n_heads:2 * n_heads])

    def zspec(off):
        return pl.BlockSpec((seq, HEAD_DIM), lambda b, h, gb: (b, off + h))

    gspec = pl.BlockSpec((1, 1, nc, CHUNK), lambda b, h, gb: (b, h, 0, 0))
    blk = 5 * _nbytes((seq, HEAD_DIM), BF16) + 2 * _nbytes((nc, CHUNK), F32)
    grid_spec = pltpu.PrefetchScalarGridSpec(
        num_scalar_prefetch=1,
        grid=(batch, n_heads),
        in_specs=[zspec(cols["aq"]), zspec(cols["ak"]), zspec(cols["av"]), zspec(cols["ao"]), gspec, gspec,
                  pl.BlockSpec((CONV_W, HEAD_DIM), lambda b, h, gb: (0, h)),
                  pl.BlockSpec((CONV_W, HEAD_DIM), lambda b, h, gb: (0, n_heads + h)),
                  pl.BlockSpec((1, HEAD_DIM), lambda b, h, gb: (0, h))],
        out_specs=pl.BlockSpec((seq, HEAD_DIM), lambda b, h, gb: (b, h)),
        scratch_shapes=[pltpu.VMEM((HEAD_DIM, HEAD_DIM), F32), pltpu.VMEM((1, HEAD_DIM), F32),
                        pltpu.VMEM((1, 1), F32), pltpu.VMEM((CHUNK, HEAD_DIM), F32),
                        pltpu.VMEM((CHUNK, HEAD_DIM), F32)],
    )
    return pl.pallas_call(
        functools.partial(_mlstm_kernel, n_heads=n_heads, n_chunks=nc),
        grid_spec=grid_spec,
        out_shape=jax.ShapeDtypeStruct((T, W), BF16),
        compiler_params=_params(("parallel", "parallel"), blk, 64 * _nbytes((CHUNK, HEAD_DIM), F32)),
        name="mlstm",
    )(a_gate_bias.astype(F32), z, z, z, z, ig, fg, a_conv.astype(F32), a_conv.astype(F32),
      a_norm.reshape(1, W).astype(F32))


def _sgu_kernel(u_ref, v_ref, lng_ref, lnb_ref, ws_ref, bst_ref, y_ref, *, n_groups, chunks_per_block):
    L = CHUNK
    c0 = math.sqrt(2.0 / math.pi)

    def gelu(x):
        return x * (0.5 * (1.0 + jnp.tanh(c0 * (x + 0.044715 * (x * x * x)))))

    v = gelu(v_ref[...].astype(F32))
    mu = jnp.mean(v, axis=-1, keepdims=True)
    vc = v - mu
    vn = vc * lax.rsqrt(jnp.mean(vc * vc, axis=-1, keepdims=True) + EPS) * lng_ref[...] + lnb_ref[...]
    vn = vn.astype(BF16)
    ri = lax.broadcasted_iota(jnp.int32, (L, L), 0)
    ci = lax.broadcasted_iota(jnp.int32, (L, L), 1)
    bst = bst_ref[...]
    for g in range(n_groups):
        cs = slice(g * HEAD_DIM, (g + 1) * HEAD_DIM)
        w = jnp.where(ci <= ri, ws_ref[g], 0.0).astype(BF16)
        bias = bst[:, g:g + 1]
        for c in range(chunks_per_block):
            rs = slice(c * L, (c + 1) * L)
            s = jnp.dot(w, vn[rs, cs], preferred_element_type=F32) + bias
            y_ref[rs, cs] = (gelu(u_ref[rs, cs].astype(F32)) * s).astype(y_ref.dtype)


def sgu_branch(z, b_ln_g, b_ln_b, b_ws, b_bs, n_groups, cols):
    T = z.shape[0]
    W = n_groups * HEAD_DIM
    off = cols["bz"] * HEAD_DIM
    assert off % W == 0
    ublk = off // W
    cpb = _divisor(T // CHUNK, (4, 2, 1))
    R = cpb * CHUNK
    blk = 3 * _nbytes((R, W), BF16) + _nbytes((n_groups, CHUNK, CHUNK), F32)
    return pl.pallas_call(
        functools.partial(_sgu_kernel, n_groups=n_groups, chunks_per_block=cpb),
        grid=(T // R,),
        in_specs=[pl.BlockSpec((R, W), lambda i: (i, ublk)),
                  pl.BlockSpec((R, W), lambda i: (i, ublk + 1)),
                  pl.BlockSpec((1, W), lambda i: (0, 0)),
                  pl.BlockSpec((1, W), lambda i: (0, 0)),
                  pl.BlockSpec((n_groups, CHUNK, CHUNK), lambda i: (0, 0, 0)),
                  pl.BlockSpec((CHUNK, n_groups), lambda i: (0, 0))],
        out_specs=pl.BlockSpec((R, W), lambda i: (i, 0)),
        out_shape=jax.ShapeDtypeStruct((T, W), BF16),
        compiler_params=_params(("parallel",), blk, 6 * _nbytes((R, W), F32)),
        name="spatial_gating",
    )(z, z, b_ln_g.reshape(1, W).astype(F32), b_ln_b.reshape(1, W).astype(F32), b_ws.astype(F32),
      jnp.transpose(b_bs).astype(F32))


def _t5_bucket_table(n):
    d = np.arange(n, dtype=np.int64)
    max_exact = N_BUCKETS // 2
    nf = np.maximum(d, 1).astype(np.float32)
    scaled = (np.log(nf / np.float32(max_exact)) / np.float32(math.log(MAX_DIST / max_exact))
              * np.float32(N_BUCKETS - max_exact))
    large = np.minimum(max_exact + scaled.astype(np.int32), N_BUCKETS - 1)
    return np.where(d < max_exact, d, large).astype(np.int32)


def _bias_kernel(rb_ref, bkt_ref, o_ref, *, n_heads):
    h = pl.program_id(0)
    bkt = bkt_ref[...]
    far = rb_ref[(N_BUCKETS - 1) * n_heads + h]
    out = jnp.full(bkt.shape, -jnp.inf, F32)
    for b in range(N_BUCKETS):
        out = jnp.where(bkt == b, rb_ref[b * n_heads + h] - far, out)
    o_ref[0] = out


def attn_bias_tiles(rel_bias, tq):
    n_heads = rel_bias.shape[1]
    assert tq >= MAX_DIST
    r = np.arange(tq)[:, None]
    c = np.arange(2 * tq)[None, :]
    dist = r - c + tq
    table = _t5_bucket_table(2 * tq)
    bkt = np.where(dist >= 0, table[np.maximum(dist, 0)], -1).astype(np.int32)
    grid_spec = pltpu.PrefetchScalarGridSpec(
        num_scalar_prefetch=1,
        grid=(n_heads,),
        in_specs=[pl.BlockSpec((tq, 2 * tq), lambda h, rb: (0, 0))],
        out_specs=pl.BlockSpec((1, tq, 2 * tq), lambda h, rb: (h, 0, 0)),
    )
    return pl.pallas_call(
        functools.partial(_bias_kernel, n_heads=n_heads),
        grid_spec=grid_spec,
        out_shape=jax.ShapeDtypeStruct((n_heads, tq, 2 * tq), F32),
        compiler_params=_params(("arbitrary",), 2 * _nbytes((tq, 2 * tq), F32)),
        name="attn_bias_tiles",
    )(rel_bias.astype(F32).reshape(-1), jnp.asarray(bkt))


def _attn_kernel(q_ref, k_ref, v_ref, nb_ref, lam_ref, cn_ref, y_ref,
                 m1_scr, l1_scr, a1_scr, m2_scr, l2_scr, a2_scr, *, tq, lam_init):
    qi = pl.program_id(2)
    dk = HEAD_DIM // 2
    scale = dk ** -0.5
    q = (q_ref[...].astype(F32) * scale).astype(BF16)
    lane = lax.broadcasted_iota(jnp.int32, q.shape, 1)
    zero = jnp.zeros_like(q)
    q1 = jnp.where(lane < dk, q, zero)
    q2 = jnp.where(lane >= dk, q, zero)
    for m_scr, l_scr, a_scr in ((m1_scr, l1_scr, a1_scr), (m2_scr, l2_scr, a2_scr)):
        m_scr[...] = jnp.full(m_scr.shape, -jnp.inf, F32)
        l_scr[...] = jnp.zeros_like(l_scr)
        a_scr[...] = jnp.zeros_like(a_scr)

    def update(s, vb, m_scr, l_scr, a_scr):
        m_prev = m_scr[...]
        m_new = jnp.maximum(m_prev, jnp.max(s, axis=-1, keepdims=True))
        alpha = jnp.exp(m_prev - m_new)
        p = jnp.exp(s - m_new)
        l_scr[...] = alpha * l_scr[...] + jnp.sum(p, axis=-1, keepdims=True)
        a_scr[...] = alpha * a_scr[...] + jnp.dot(p.astype(BF16), vb, preferred_element_type=F32)
        m_scr[...] = m_new

    def block(kstart, bias):
        rows = pl.ds(pl.multiple_of(kstart, tq), tq)
        kb = k_ref[rows, :]
        vb = v_ref[rows, :]
        s1 = _dot_nt(q1, kb)
        s2 = _dot_nt(q2, kb)
        if bias is not None:
            s1 = s1 + bias
            s2 = s2 + bias
        update(s1, vb, m1_scr, l1_scr, a1_scr)
        update(s2, vb, m2_scr, l2_scr, a2_scr)

    def far_block(j, carry):
        block(j * tq, None)
        return carry

    lax.fori_loop(0, qi - 1, far_block, 0)

    @pl.when(qi > 0)
    def _():
        block((qi - 1) * tq, nb_ref[0, :, :tq])

    block(qi * tq, nb_ref[0, :, tq:])

    lf = lam_ref[...]
    lam = (jnp.exp(jnp.sum(lf[0:1] * lf[1:2], axis=-1, keepdims=True))
           - jnp.exp(jnp.sum(lf[2:3] * lf[3:4], axis=-1, keepdims=True)) + lam_init)
    o = a1_scr[...] / l1_scr[...] - lam * (a2_scr[...] / l2_scr[...])
    o = o * lax.rsqrt(jnp.mean(o * o, axis=-1, keepdims=True) + EPS) * cn_ref[...] * (1.0 - lam_init)
    y_ref[...] = o.astype(y_ref.dtype)


def attn_branch(z, near_bias, c_lambda, c_norm, batch, seq, n_heads, cols, lam_init, tq):
    T = batch * seq
    nq = seq // tq
    W = n_heads * HEAD_DIM
    blk = (2 * _nbytes((tq, HEAD_DIM), BF16) + 2 * _nbytes((seq, HEAD_DIM), BF16)
           + _nbytes((tq, 2 * tq), F32))
    return pl.pallas_call(
        functools.partial(_attn_kernel, tq=tq, lam_init=lam_init),
        grid=(batch, n_heads, nq),
        in_specs=[pl.BlockSpec((tq, HEAD_DIM), lambda b, h, i: (b * nq + i, cols["cq"] + h)),
                  pl.BlockSpec((seq, HEAD_DIM), lambda b, h, i: (b, cols["ck"] + h)),
                  pl.BlockSpec((seq, HEAD_DIM), lambda b, h, i: (b, cols["cv"] + h)),
                  pl.BlockSpec((1, tq, 2 * tq), lambda b, h, i: (h, 0, 0)),
                  pl.BlockSpec(c_lambda.shape, lambda b, h, i: (0, 0)),
                  pl.BlockSpec((1, HEAD_DIM), lambda b, h, i: (0, 0))],
        out_specs=pl.BlockSpec((tq, HEAD_DIM), lambda b, h, i: (b * nq + i, h)),
        out_shape=jax.ShapeDtypeStruct((T, W), BF16),
        scratch_shapes=[pltpu.VMEM((tq, 1), F32), pltpu.VMEM((tq, 1), F32), pltpu.VMEM((tq, HEAD_DIM), F32),
                        pltpu.VMEM((tq, 1), F32), pltpu.VMEM((tq, 1), F32), pltpu.VMEM((tq, HEAD_DIM), F32)],
        compiler_params=_params(("parallel", "parallel", "arbitrary"), blk, 16 * _nbytes((tq, tq), F32)),
        name="diff_attention",
    )(z, z, z, near_bias, c_lambda.astype(F32), c_norm.reshape(1, HEAD_DIM).astype(F32))


def _merge_kernel(ya_ref, yb_ref, yc_ref, wa_ref, wb_ref, wc_ref, g0_ref, g1_ref, g2_ref, o_ref):
    da = jnp.dot(ya_ref[...], wa_ref[...], preferred_element_type=F32)
    db = jnp.dot(yb_ref[...], wb_ref[...], preferred_element_type=F32)
    dc = jnp.dot(yc_ref[...], wc_ref[...], preferred_element_type=F32)
    merged = (_sigmoid(g0_ref[...].astype(F32)) * da + _sigmoid(g1_ref[...].astype(F32)) * db
              + _sigmoid(g2_ref[...].astype(F32)) * dc)
    o_ref[...] = merged.astype(o_ref.dtype)


def gated_merge(y_a, y_b, y_c, w_a, w_b, w_c, z, d_model, cols):
    T = z.shape[0]
    goff = cols["gate"] * HEAD_DIM
    tn = _divisor(math.gcd(goff, d_model), (512, 256, 128))
    tm = _divisor(T, (1024, 512, 256, 128))
    ka, kb, kc = y_a.shape[1], y_b.shape[1], y_c.shape[1]

    def gspec(j):
        base = (goff + j * d_model) // tn
        return pl.BlockSpec((tm, tn), lambda i, n: (i, base + n))

    blk = (_nbytes((tm, ka + kb + kc), BF16) + _nbytes((ka + kb + kc, tn), BF16) + 4 * _nbytes((tm, tn), BF16))
    return pl.pallas_call(
        _merge_kernel,
        grid=(T // tm, d_model // tn),
        in_specs=[pl.BlockSpec((tm, ka), lambda i, n: (i, 0)),
                  pl.BlockSpec((tm, kb), lambda i, n: (i, 0)),
                  pl.BlockSpec((tm, kc), lambda i, n: (i, 0)),
                  pl.BlockSpec((ka, tn), lambda i, n: (0, n)),
                  pl.BlockSpec((kb, tn), lambda i, n: (0, n)),
                  pl.BlockSpec((kc, tn), lambda i, n: (0, n)),
                  gspec(0), gspec(1), gspec(2)],
        out_specs=pl.BlockSpec((tm, tn), lambda i, n: (i, n)),
        out_shape=jax.ShapeDtypeStruct((T, d_model), BF16),
        compiler_params=_params(("parallel", "parallel"), blk, 6 * _nbytes((tm, tn), F32)),
        name="gated_merge",
    )(y_a, y_b, y_c, w_a, w_b, w_c, z, z, z)


def _route_kernel(l_ref, idx_ref, w_ref, *, n_experts):
    logits = l_ref[...]
    lane = lax.broadcasted_iota(jnp.int32, logits.shape, 1)
    logits = jnp.where(lane < n_experts, logits, -jnp.inf)
    m1 = jnp.max(logits, axis=-1, keepdims=True)
    i1 = jnp.min(jnp.where(logits == m1, lane, LANES), axis=-1, keepdims=True)
    rest = jnp.where(lane == i1, -jnp.inf, logits)
    m2 = jnp.max(rest, axis=-1, keepdims=True)
    i2 = jnp.min(jnp.where(rest == m2, lane, LANES), axis=-1, keepdims=True)
    e = jnp.exp(m2 - m1)
    w1 = 1.0 / (1.0 + e)
    w2 = e / (1.0 + e)
    idx_ref[...] = jnp.where(lane == 0, i1, jnp.where(lane == 1, i2, 0))
    w_ref[...] = jnp.where(lane == 0, w1, jnp.where(lane == 1, w2, 0.0))


def route_top2(logits, n_experts):
    T = logits.shape[0]
    tm = _divisor(T, (1024, 512, 256, 128, 8))
    spec = pl.BlockSpec((tm, LANES), lambda i: (i, 0))
    idx, w = pl.pallas_call(
        functools.partial(_route_kernel, n_experts=n_experts),
        grid=(T // tm,),
        in_specs=[spec],
        out_specs=[spec, spec],
        out_shape=[jax.ShapeDtypeStruct((T, LANES), jnp.int32), jax.ShapeDtypeStruct((T, LANES), F32)],
        compiler_params=_params(("parallel",), 3 * _nbytes((tm, LANES), F32)),
        name="route_top2",
    )(logits)
    return idx[:, :TOP_K], w


def _moe_up_kernel(te_ref, tv_ref, src_ref, x_hbm, wg_ref, wu_ref, o_ref, xrow_scr, xbf_scr, sem, *, tm):
    i = pl.program_id(0)
    j = pl.program_id(1)
    valid = tv_ref[i] > 0

    def row_copy(r, tok):
        return pltpu.make_async_copy(x_hbm.at[pl.ds(tok, 1), :], xrow_scr.at[pl.ds(r, 1), :], sem)

    @pl.when(jnp.logical_and(valid, j == 0))
    def _():
        def start(r, carry):
            row_copy(r, src_ref[i * tm + r]).start()
            return carry

        def wait(r, carry):
            row_copy(r, 0).wait()
            return carry

        lax.fori_loop(0, tm, start, 0)
        lax.fori_loop(0, tm, wait, 0)
        xbf_scr[...] = xrow_scr[...].astype(BF16)

    @pl.when(valid)
    def _():
        x = xbf_scr[...]
        g = jnp.dot(x, wg_ref[0], preferred_element_type=F32)
        u = jnp.dot(x, wu_ref[0], preferred_element_type=F32)
        o_ref[...] = (g * _sigmoid(g) * u).astype(o_ref.dtype)

    @pl.when(jnp.logical_not(valid))
    def _():
        o_ref[...] = jnp.zeros_like(o_ref)


def _moe_down_kernel(te_ref, tv_ref, a_ref, wd_ref, o_ref):
    i = pl.program_id(0)

    @pl.when(tv_ref[i] > 0)
    def _():
        o_ref[...] = jnp.dot(a_ref[...], wd_ref[0], preferred_element_type=F32)

    @pl.when(tv_ref[i] == 0)
    def _():
        o_ref[...] = jnp.zeros_like(o_ref)


def _moe_combine_kernel(dest_ref, y_hbm, h_ref, w_ref, o_ref, buf0, buf1, sem, *, tm):
    i = pl.program_id(0)

    def row_copy(r, k, row):
        buf = buf0 if k == 0 else buf1
        return pltpu.make_async_copy(y_hbm.at[pl.ds(row, 1), :], buf.at[pl.ds(r, 1), :], sem.at[k])

    def start(r, carry):
        a = TOP_K * (i * tm + r)
        row_copy(r, 0, dest_ref[a]).start()
        row_copy(r, 1, dest_ref[a + 1]).start()
        return carry

    def wait(r, carry):
        row_copy(r, 0, 0).wait()
        row_copy(r, 1, 0).wait()
        return carry

    lax.fori_loop(0, tm, start, 0)
    lax.fori_loop(0, tm, wait, 0)
    w = w_ref[...]
    o_ref[...] = h_ref[...] + (w[:, 0:1] * buf0[...] + w[:, 1:2] * buf1[...])


def moe_ffn(h, hn, logits, wg, wu, wd):
    T, D = hn.shape
    E, _, F = wg.shape
    idx, wts = route_top2(logits, E)
    tm = _divisor(T, (512, 256, 128))
    A = T * TOP_K
    n_tiles = A // tm + E

    e_flat = idx.reshape(-1)
    onehot = (e_flat[:, None] == jnp.arange(E, dtype=jnp.int32)[None, :]).astype(jnp.int32)
    csum = jnp.cumsum(onehot, axis=0)
    pos = jnp.take_along_axis(csum, e_flat[:, None], axis=1)[:, 0] - 1
    counts = csum[-1]
    padded = ((counts + tm - 1) // tm) * tm
    ends = jnp.cumsum(padded)
    starts = ends - padded
    dest = (starts[e_flat] + pos).astype(jnp.int32)
    tile_start = jnp.arange(n_tiles, dtype=jnp.int32) * tm
    tile_expert = jnp.minimum(jnp.sum((tile_start[:, None] >= ends[None, :]).astype(jnp.int32), axis=1), E - 1)
    tile_valid = (tile_start < ends[-1]).astype(jnp.int32)
    src = jnp.zeros((n_tiles * tm,), jnp.int32).at[dest].set(jnp.arange(A, dtype=jnp.int32) // TOP_K)

    tf = _divisor(F, (512, 256, 128))
    blk_up = 2 * _nbytes((D, tf), BF16) + _nbytes((tm, tf), BF16)
    act = pl.pallas_call(
        functools.partial(_moe_up_kernel, tm=tm),
        grid_spec=pltpu.PrefetchScalarGridSpec(
            num_scalar_prefetch=3,
            grid=(n_tiles, F // tf),
            in_specs=[pl.BlockSpec(memory_space=pl.ANY),
                      pl.BlockSpec((1, D, tf), lambda i, j, te, tv, sr: (te[i], 0, j)),
                      pl.BlockSpec((1, D, tf), lambda i, j, te, tv, sr: (te[i], 0, j))],
            out_specs=pl.BlockSpec((tm, tf), lambda i, j, te, tv, sr: (i, j)),
            scratch_shapes=[pltpu.VMEM((tm, D), F32), pltpu.VMEM((tm, D), BF16), pltpu.SemaphoreType.DMA(())],
        ),
        out_shape=jax.ShapeDtypeStruct((n_tiles * tm, F), BF16),
        compiler_params=_params(("arbitrary", "arbitrary"), blk_up,
                                _nbytes((tm, D), F32) + _nbytes((tm, D), BF16) + 3 * _nbytes((tm, tf), F32)),
        name="moe_up",
    )(tile_expert, tile_valid, src, hn, wg, wu)

    tn = _divisor(D, (512, 256, 128))
    blk_dn = _nbytes((tm, F), BF16) + _nbytes((F, tn), BF16) + _nbytes((tm, tn), F32)
    y_sorted = pl.pallas_call(
        _moe_down_kernel,
        grid_spec=pltpu.PrefetchScalarGridSpec(
            num_scalar_prefetch=2,
            grid=(n_tiles, D // tn),
            in_specs=[pl.BlockSpec((tm, F), lambda i, j, te, tv: (i, 0)),
                      pl.BlockSpec((1, F, tn), lambda i, j, te, tv: (te[i], 0, j))],
            out_specs=pl.BlockSpec((tm, tn), lambda i, j, te, tv: (i, j)),
        ),
        out_shape=jax.ShapeDtypeStruct((n_tiles * tm, D), F32),
        compiler_params=_params(("parallel", "parallel"), blk_dn, _nbytes((tm, tn), F32)),
        name="moe_down",
    )(tile_expert, tile_valid, act, wd)

    tc = _divisor(T, (256, 128))
    blk_c = 2 * _nbytes((tc, D), F32) + _nbytes((tc, LANES), F32)
    return pl.pallas_call(
        functools.partial(_moe_combine_kernel, tm=tc),
        grid_spec=pltpu.PrefetchScalarGridSpec(
            num_scalar_prefetch=1,
            grid=(T // tc,),
            in_specs=[pl.BlockSpec(memory_space=pl.ANY),
                      pl.BlockSpec((tc, D), lambda i, de: (i, 0)),
                      pl.BlockSpec((tc, LANES), lambda i, de: (i, 0))],
            out_specs=pl.BlockSpec((tc, D), lambda i, de: (i, 0)),
            scratch_shapes=[pltpu.VMEM((tc, D), F32), pltpu.VMEM((tc, D), F32), pltpu.SemaphoreType.DMA((2,))],
        ),
        out_shape=jax.ShapeDtypeStruct((T, D), F32),
        compiler_params=_params(("arbitrary",), blk_c, 3 * _nbytes((tc, D), F32)),
        name="moe_combine",
    )(dest, y_sorted, h, wts)


def _mixer(h, li, batch, seq, near_bias, tq, norm_g, w_in, a_gate_bias, a_conv, a_norm, b_ln_g, b_ln_b, b_ws,
           b_bs, c_lambda, c_norm, w_br_a, w_br_b, w_br_c, w_out):
    d_model = h.shape[1]
    h_a = a_gate_bias.shape[0] // 2
    g_b = b_ws.shape[0]
    h_c = w_br_c.shape[0] // HEAD_DIM
    w_a = h_a * HEAD_DIM
    cols = {"aq": 0, "ak": h_a, "av": 2 * h_a, "ao": 3 * h_a, "bz": 4 * h_a, "cq": 4 * h_a + 2 * g_b}
    cols["ck"] = cols["cq"] + h_c
    cols["cv"] = cols["ck"] + h_c
    cols["gate"] = cols["cv"] + h_c
    g0 = 4 * w_a
    g1 = g0 + 2 * h_a
    w_main = jnp.concatenate([w_in[:, :g0], w_in[:, g1:]], axis=1).astype(BF16)

    n, zg = rmsnorm(h, norm_g, proj=w_in[:, g0:g1])
    z = matmul(n, w_main)
    y_a = mlstm_branch(z, zg, a_gate_bias, a_conv, a_norm, batch, seq, h_a, cols)
    y_b = sgu_branch(z, b_ln_g, b_ln_b, b_ws, b_bs, g_b, cols)
    lam_init = 0.8 - 0.6 * math.exp(-0.3 * li)
    y_c = attn_branch(z, near_bias, c_lambda, c_norm, batch, seq, h_c, cols, lam_init, tq)
    merged = gated_merge(y_a, y_b, y_c, w_br_a.astype(BF16), w_br_b.astype(BF16), w_br_c.astype(BF16), z,
                         d_model, cols)
    return matmul_residual(merged, w_out.astype(BF16), h)


def kernel(x, norm_mix, w_in, a_gate_bias, a_conv, a_norm, b_ln_g, b_ln_b, b_ws, b_bs, c_lambda, c_norm, rel_bias,
           w_br_a, w_br_b, w_br_c, w_out, norm_ffn, ffn_wg, ffn_wu, ffn_wd, router, moe_wg, moe_wu, moe_wd,
           final_norm):
    batch, seq, d_model = x.shape
    depth = w_in.shape[0]
    tq = _divisor(seq, (256, 128))
    near_bias = attn_bias_tiles(rel_bias, tq)
    h = x.reshape(batch * seq, d_model).astype(F32)
    for li in range(depth):
        h = _mixer(h, li, batch, seq, near_bias, tq, norm_mix[li], w_in[li], a_gate_bias[li], a_conv[li],
                   a_norm[li], b_ln_g[li], b_ln_b[li], b_ws[li], b_bs[li], c_lambda[li], c_norm[li],
                   w_br_a[li], w_br_b[li], w_br_c[li], w_out[li])
        j = li // 2
        if li % 2 == 0:
            hn = rmsnorm(h, norm_ffn[li])
            act = swiglu_up(hn, ffn_wg[j].astype(BF16), ffn_wu[j].astype(BF16))
            h = matmul_residual(act, ffn_wd[j].astype(BF16), h)
        else:
            hn, logits = rmsnorm(h, norm_ffn[li], proj=router[j], out_dtype=F32)
            h = moe_ffn(h, hn, logits, moe_wg[j].astype(BF16), moe_wu[j].astype(BF16), moe_wd[j].astype(BF16))
    out = rmsnorm(h, final_norm, out_dtype=F32)
    return out.reshape(batch, seq, d_model).astype(x.dtype)
```

```python
import functools
import math

import numpy as np
import jax
import jax.numpy as jnp
from jax import lax
from jax.experimental import pallas as pl
from jax.experimental.pallas import tpu as pltpu

F32 = jnp.float32
BF16 = jnp.bfloat16

EPS = 1e-6
HEAD_DIM = 128
CHUNK = 128
CONV_W = 4
N_BUCKETS = 32
MAX_DIST = 128
TOP_K = 2
LANES = 128
VMEM_BYTES_V7X = 64 * 1024 * 1024
VMEM_SLACK = 6 * 1024 * 1024
VMEM_BUDGET = VMEM_BYTES_V7X - 8 * 1024 * 1024
WEIGHT_STAGE_ROWS = 256


def _divisor(n, candidates):
    for c in candidates:
        if n % c == 0:
            return c
    raise ValueError(f"no tile size in {candidates} divides {n}")


def _vmem_need(block_bytes, scratch_bytes):
    return 2 * block_bytes + scratch_bytes + VMEM_SLACK


def _params(semantics, block_bytes, scratch_bytes=0):
    limit = int(min(max(_vmem_need(block_bytes, scratch_bytes), 16 * 1024 * 1024), VMEM_BYTES_V7X - 4 * 1024 * 1024))
    return pltpu.CompilerParams(dimension_semantics=semantics, vmem_limit_bytes=limit)


def _row_tile(T, block_bytes, scratch_bytes):
    for tm in (1024, 512, 256, 128):
        if T % tm == 0 and _vmem_need(block_bytes(tm), scratch_bytes(tm)) <= VMEM_BUDGET:
            return tm
    raise ValueError("no row tile fits VMEM")


def _nbytes(shape, dtype):
    return int(np.prod(shape)) * jnp.dtype(dtype).itemsize


def _sigmoid(x):
    return 1.0 / (1.0 + jnp.exp(-x))


def _split_bf16(x):
    hi = x.astype(BF16)
    lo = (x - hi.astype(F32)).astype(BF16)
    return hi, lo


def _dot_nt(a, b):
    return lax.dot_general(a, b, (((1,), (1,)), ((), ())), preferred_element_type=F32)


def _dot_tn(a, b):
    return lax.dot_general(a, b, (((0,), (0,)), ((), ())), preferred_element_type=F32)


def _norm_kernel(*refs, has_proj, out_dtype):
    if has_proj:
        x_ref, g_ref, whi_ref, wlo_ref, y_ref, p_ref = refs
    else:
        x_ref, g_ref, y_ref = refs
    x = x_ref[...]
    y = x * lax.rsqrt(jnp.mean(x * x, axis=-1, keepdims=True) + EPS) * g_ref[...]
    y_ref[...] = y.astype(out_dtype)
    if has_proj:
        y_hi, y_lo = _split_bf16(y)
        w_hi = whi_ref[...]
        p = jnp.dot(y_hi, w_hi, preferred_element_type=F32)
        p += jnp.dot(y_hi, wlo_ref[...], preferred_element_type=F32)
        p += jnp.dot(y_lo, w_hi, preferred_element_type=F32)
        p_ref[...] = p


def rmsnorm(x, g, proj=None, out_dtype=BF16):
    T, D = x.shape
    tm = _divisor(T, (256, 128, 8))
    g2 = g.reshape(1, D).astype(F32)
    in_specs = [pl.BlockSpec((tm, D), lambda i: (i, 0)), pl.BlockSpec((1, D), lambda i: (0, 0))]
    out_shape = [jax.ShapeDtypeStruct((T, D), out_dtype)]
    out_specs = [pl.BlockSpec((tm, D), lambda i: (i, 0))]
    args = [x, g2]
    blk = _nbytes((tm, D), F32) + _nbytes((tm, D), out_dtype)
    if proj is not None:
        n = proj.shape[1]
        assert n <= LANES
        w = jnp.pad(proj.astype(F32), ((0, 0), (0, LANES - n)))
        w_hi, w_lo = _split_bf16(w)
        in_specs += [pl.BlockSpec((D, LANES), lambda i: (0, 0))] * 2
        out_shape.append(jax.ShapeDtypeStruct((T, LANES), F32))
        out_specs.append(pl.BlockSpec((tm, LANES), lambda i: (i, 0)))
        args += [w_hi, w_lo]
        blk += 2 * _nbytes((D, LANES), BF16)
    outs = pl.pallas_call(
        functools.partial(_norm_kernel, has_proj=proj is not None, out_dtype=out_dtype),
        grid=(T // tm,),
        in_specs=in_specs,
        out_specs=out_specs,
        out_shape=out_shape,
        compiler_params=_params(("parallel",), blk, 4 * _nbytes((tm, D), F32)),
        name="rmsnorm",
    )(*args)
    return (outs[0], outs[1]) if proj is not None else outs[0]


def _stage_weight(w_ref, w_scr, wx_ref=None, shift=0):
    K, tn = w_scr.shape
    rc = _divisor(K, (WEIGHT_STAGE_ROWS, LANES))

    def step(r, carry):
        rows = pl.ds(pl.multiple_of(r * rc, rc), rc)
        w = w_ref[0, rows, :]
        if shift:
            wide = jnp.concatenate([w, wx_ref[0, rows, :]], axis=1)
            w = pltpu.roll(wide, wide.shape[1] - shift, 1)[:, :tn]
        w_scr[rows, :] = w.astype(BF16)
        return carry

    lax.fori_loop(0, K // rc, step, 0)


def _matmul_kernel(*refs, has_res, shift):
    a_ref, w_ref = refs[:2]
    wx_ref = refs[2] if shift else None
    r_ref = refs[2 + bool(shift)] if has_res else None
    o_ref, w_scr = refs[-2:]

    @pl.when(pl.program_id(1) == 0)
    def _():
        _stage_weight(w_ref, w_scr, wx_ref, shift)

    d = jnp.dot(a_ref[...], w_scr[...], preferred_element_type=F32)
    if has_res:
        d = d + r_ref[...]
    o_ref[...] = d.astype(o_ref.dtype)


def matmul(a, w, layer, out_dtype, *, a_col=0, k=None, w_row=0, w_col=0, n=None, shift=0, res=None):
    T = a.shape[0]
    k = a.shape[1] if k is None else k
    n = w.shape[2] - w_col if n is None else n
    tn = _divisor(math.gcd(n, w_col) if w_col else n, (512, 256, 128))
    assert a_col % k == 0 and w_row % k == 0 and 0 <= shift < LANES
    extra = _nbytes((k, LANES), F32) if shift else 0
    res_b = (lambda tm: _nbytes((tm, tn), F32)) if res is not None else (lambda tm: 0)

    def blk(tm):
        return _nbytes((tm, k), BF16) + _nbytes((k, tn), F32) + extra + _nbytes((tm, tn), out_dtype) + res_b(tm)

    def scr(tm):
        return _nbytes((k, tn), BF16) + 2 * _nbytes((tm, tn), F32)

    tm = _row_tile(T, blk, scr)
    ab, wrb, wcb = a_col // k, w_row // k, w_col // tn
    in_specs = [pl.BlockSpec((tm, k), lambda j, i: (i, ab)),
                pl.BlockSpec((1, k, tn), lambda j, i: (layer, wrb, wcb + j))]
    args = [a, w]
    if shift:
        per = tn // LANES
        xb = w_col // LANES + per
        in_specs.append(pl.BlockSpec((1, k, LANES), lambda j, i: (layer, wrb, xb + per * j)))
        args.append(w)
    if res is not None:
        in_specs.append(pl.BlockSpec((tm, tn), lambda j, i: (i, j)))
        args.append(res)
    return pl.pallas_call(
        functools.partial(_matmul_kernel, has_res=res is not None, shift=shift),
        grid=(n // tn, T // tm),
        in_specs=in_specs,
        out_specs=pl.BlockSpec((tm, tn), lambda j, i: (i, j)),
        out_shape=jax.ShapeDtypeStruct((T, n), out_dtype),
        scratch_shapes=[pltpu.VMEM((k, tn), BF16)],
        compiler_params=_params(("parallel", "arbitrary"), blk(tm), scr(tm)),
        name="matmul",
    )(*args)


def matmul_residual(a, w, layer, res):
    K = a.shape[1]
    nk = next(n for n in range(1, K // LANES + 1) if K % (n * LANES) == 0 and K // n <= 6144)
    k = K // nk
    out = res
    for p in range(nk):
        out = matmul(a, w, layer, F32, a_col=p * k, k=k, w_row=p * k, res=out)
    return out


def _swiglu_up_kernel(x_ref, wg_ref, wu_ref, o_ref, wg_scr, wu_scr):
    @pl.when(pl.program_id(1) == 0)
    def _():
        _stage_weight(wg_ref, wg_scr)
        _stage_weight(wu_ref, wu_scr)

    x = x_ref[...]
    g = jnp.dot(x, wg_scr[...], preferred_element_type=F32)
    u = jnp.dot(x, wu_scr[...], preferred_element_type=F32)
    o_ref[...] = (g * _sigmoid(g) * u).astype(o_ref.dtype)


def swiglu_up(x, wg, wu, layer):
    T, K = x.shape
    F = wg.shape[2]
    tn = _divisor(F, (512, 256, 128))

    def blk(tm):
        return _nbytes((tm, K), BF16) + 2 * _nbytes((K, tn), F32) + _nbytes((tm, tn), BF16)

    def scr(tm):
        return 2 * _nbytes((K, tn), BF16) + 3 * _nbytes((tm, tn), F32)

    tm = _row_tile(T, blk, scr)
    wspec = pl.BlockSpec((1, K, tn), lambda j, i: (layer, 0, j))
    return pl.pallas_call(
        _swiglu_up_kernel,
        grid=(F // tn, T // tm),
        in_specs=[pl.BlockSpec((tm, K), lambda j, i: (i, 0)), wspec, wspec],
        out_specs=pl.BlockSpec((tm, tn), lambda j, i: (i, j)),
        out_shape=jax.ShapeDtypeStruct((T, F), BF16),
        scratch_shapes=[pltpu.VMEM((K, tn), BF16), pltpu.VMEM((K, tn), BF16)],
        compiler_params=_params(("parallel", "arbitrary"), blk(tm), scr(tm)),
        name="swiglu_up",
    )(x, wg, wu)


def _split_bf16_f32(x):
    hi = x.astype(BF16)
    return hi, x - hi.astype(F32)


def _mlstm_kernel(gb_ref, q_ref, k_ref, v_ref, og_ref, ig_ref, fg_ref, cwq_ref, cwk_ref, an_ref, y_ref,
                  c_scr, n_scr, m_scr, pq_scr, pk_scr, *, n_heads, n_chunks):
    h = pl.program_id(1)
    L = CHUNK
    c_scr[...] = jnp.zeros_like(c_scr)
    n_scr[...] = jnp.zeros_like(n_scr)
    m_scr[...] = jnp.zeros_like(m_scr)
    pq_scr[...] = jnp.zeros_like(pq_scr)
    pk_scr[...] = jnp.zeros_like(pk_scr)
    gb_i = gb_ref[h]
    gb_f = gb_ref[n_heads + h]
    cwq = cwq_ref[...]
    cwk = cwk_ref[...]
    a_norm = an_ref[...]
    ri = lax.broadcasted_iota(jnp.int32, (L, L), 0)
    ci = lax.broadcasted_iota(jnp.int32, (L, L), 1)
    lower = ci <= ri
    strict_lower01 = jnp.where(ri > ci, 1.0, 0.0).astype(BF16)

    def conv_silu(x, prev, w):
        y = x * w[CONV_W - 1:CONV_W, :]
        for s in range(1, CONV_W):
            shifted = jnp.where(ri < s, pltpu.roll(prev, s, 0), pltpu.roll(x, s, 0))
            y = y + shifted * w[CONV_W - 1 - s:CONV_W - s, :]
        return y * _sigmoid(y)

    def chunk(c, carry):
        rows = pl.ds(pl.multiple_of(c * L, L), L)
        xq = q_ref[rows, :].astype(F32)
        xk = k_ref[rows, :].astype(F32)
        q = conv_silu(xq, pq_scr[...], cwq)
        k = conv_silu(xk, pk_scr[...], cwk) * (HEAD_DIM ** -0.5)
        pq_scr[...] = xq
        pk_scr[...] = xk
        v = v_ref[rows, :]
        q_bf = q.astype(BF16)
        k_bf = k.astype(BF16)

        i_row = ig_ref[0, 0, pl.ds(c, 1), :] + gb_i
        f_row = fg_ref[0, 0, pl.ds(c, 1), :] + gb_f
        lf = jnp.minimum(f_row, 0.0) - jnp.log(1.0 + jnp.exp(-jnp.abs(f_row)))
        lf_low = jnp.where(lower, lf, 0.0)
        b_col = jnp.sum(lf_low, axis=1, keepdims=True)
        hi, rest = _split_bf16_f32(lf_low)
        mid, lo = _split_bf16_f32(rest)
        dmat = (jnp.dot(hi, strict_lower01, preferred_element_type=F32)
                + jnp.dot(mid, strict_lower01, preferred_element_type=F32)
                + jnp.dot(lo.astype(BF16), strict_lower01, preferred_element_type=F32))
        g_tot = b_col[L - 1:L, :]
        a_row = dmat[L - 1:L, :] + i_row
        a_col = jnp.sum(jnp.where(ci > ri, lf, 0.0) + jnp.where(ci == ri, i_row, 0.0),
                        axis=1, keepdims=True)

        m_prev = m_scr[...]
        c_prev = c_scr[...]
        n_prev = n_scr[...]

        log_d = jnp.where(lower, dmat + i_row, -jnp.inf)
        log_inter = b_col + m_prev
        m_t = jnp.maximum(log_inter, jnp.max(log_d, axis=1, keepdims=True))
        w = jnp.exp(log_d - m_t) * _dot_nt(q_bf, k_bf)
        e_inter = jnp.exp(log_inter - m_t)
        num = (e_inter * jnp.dot(q_bf, c_prev.astype(BF16), preferred_element_type=F32)
               + jnp.dot(w.astype(BF16), v, preferred_element_type=F32))
        den = e_inter * jnp.sum(q * n_prev, axis=1, keepdims=True) + jnp.sum(w, axis=1, keepdims=True)
        hcell = num / jnp.maximum(jnp.abs(den), jnp.exp(-m_t))

        gated = _sigmoid(og_ref[rows, :].astype(F32)) * hcell
        y = gated * lax.rsqrt(jnp.mean(gated * gated, axis=-1, keepdims=True) + EPS) * a_norm
        y_ref[rows, :] = y.astype(y_ref.dtype)

        m_new = jnp.maximum(g_tot + m_prev, jnp.max(a_row, axis=1, keepdims=True))
        w_col = jnp.exp(a_col - m_new)
        decay = jnp.exp(g_tot + m_prev - m_new)
        c_scr[...] = decay * c_prev + _dot_tn(k_bf, (v.astype(F32) * w_col).astype(BF16))
        n_scr[...] = decay * n_prev + jnp.sum(k * w_col, axis=0, keepdims=True)
        m_scr[...] = m_new
        return carry

    lax.fori_loop(0, n_chunks, chunk, 0)


def mlstm_branch(z, zg, a_gate_bias, a_conv, a_norm, batch, seq, n_heads):
    T = batch * seq
    nc = seq // CHUNK
    W = n_heads * HEAD_DIM

    def gate_rows(g):
        return jnp.transpose(g.reshape(batch, seq, n_heads), (0, 2, 1)).reshape(batch, n_heads, nc, CHUNK)

    ig = gate_rows(zg[:, :n_heads])
    fg = gate_rows(zg[:, n_heads:2 * n_heads])

    def zspec(part):
        return pl.BlockSpec((seq, HEAD_DIM), lambda b, h, gb: (b, part * n_heads + h))

    gspec = pl.BlockSpec((1, 1, nc, CHUNK), lambda b, h, gb: (b, h, 0, 0))
    blk = 5 * _nbytes((seq, HEAD_DIM), BF16) + 2 * _nbytes((nc, CHUNK), F32)
    grid_spec = pltpu.PrefetchScalarGridSpec(
        num_scalar_prefetch=1,
        grid=(batch, n_heads),
        in_specs=[zspec(0), zspec(1), zspec(2), zspec(3), gspec, gspec,
                  pl.BlockSpec((CONV_W, HEAD_DIM), lambda b, h, gb: (0, h)),
                  pl.BlockSpec((CONV_W, HEAD_DIM), lambda b, h, gb: (0, n_heads + h)),
                  pl.BlockSpec((1, HEAD_DIM), lambda b, h, gb: (0, h))],
        out_specs=pl.BlockSpec((seq, HEAD_DIM), lambda b, h, gb: (b, h)),
        scratch_shapes=[pltpu.VMEM((HEAD_DIM, HEAD_DIM), F32), pltpu.VMEM((1, HEAD_DIM), F32),
                        pltpu.VMEM((1, 1), F32), pltpu.VMEM((CHUNK, HEAD_DIM), F32),
                        pltpu.VMEM((CHUNK, HEAD_DIM), F32)],
    )
    return pl.pallas_call(
        functools.partial(_mlstm_kernel, n_heads=n_heads, n_chunks=nc),
        grid_spec=grid_spec,
        out_shape=jax.ShapeDtypeStruct((T, W), BF16),
        compiler_params=_params(("parallel", "parallel"), blk, 64 * _nbytes((CHUNK, HEAD_DIM), F32)),
        name="mlstm",
    )(a_gate_bias.astype(F32), z, z, z, z, ig, fg, a_conv.astype(F32), a_conv.astype(F32),
      a_norm.reshape(1, W).astype(F32))


def _sgu_kernel(u_ref, v_ref, lng_ref, lnb_ref, ws_ref, bst_ref, y_ref, *, n_groups, chunks_per_block):
    L = CHUNK
    c0 = math.sqrt(2.0 / math.pi)

    def gelu(x):
        return x * (0.5 * (1.0 + jnp.tanh(c0 * (x + 0.044715 * (x * x * x)))))

    v = gelu(v_ref[...].astype(F32))
    mu = jnp.mean(v, axis=-1, keepdims=True)
    vc = v - mu
    vn = vc * lax.rsqrt(jnp.mean(vc * vc, axis=-1, keepdims=True) + EPS) * lng_ref[...] + lnb_ref[...]
    vn = vn.astype(BF16)
    ri = lax.broadcasted_iota(jnp.int32, (L, L), 0)
    ci = lax.broadcasted_iota(jnp.int32, (L, L), 1)
    bst = bst_ref[...]
    for g in range(n_groups):
        cs = slice(g * HEAD_DIM, (g + 1) * HEAD_DIM)
        w = jnp.where(ci <= ri, ws_ref[g], 0.0).astype(BF16)
        bias = bst[:, g:g + 1]
        for c in range(chunks_per_block):
            rs = slice(c * L, (c + 1) * L)
            s = jnp.dot(w, vn[rs, cs], preferred_element_type=F32) + bias
            y_ref[rs, cs] = (gelu(u_ref[rs, cs].astype(F32)) * s).astype(y_ref.dtype)


def sgu_branch(z, b_ln_g, b_ln_b, b_ws, b_bs, n_groups):
    T = z.shape[0]
    W = n_groups * HEAD_DIM
    cpb = _divisor(T // CHUNK, (4, 2, 1))
    R = cpb * CHUNK
    blk = 3 * _nbytes((R, W), BF16) + _nbytes((n_groups, CHUNK, CHUNK), F32)
    return pl.pallas_call(
        functools.partial(_sgu_kernel, n_groups=n_groups, chunks_per_block=cpb),
        grid=(T // R,),
        in_specs=[pl.BlockSpec((R, W), lambda i: (i, 0)),
                  pl.BlockSpec((R, W), lambda i: (i, 1)),
                  pl.BlockSpec((1, W), lambda i: (0, 0)),
                  pl.BlockSpec((1, W), lambda i: (0, 0)),
                  pl.BlockSpec((n_groups, CHUNK, CHUNK), lambda i: (0, 0, 0)),
                  pl.BlockSpec((CHUNK, n_groups), lambda i: (0, 0))],
        out_specs=pl.BlockSpec((R, W), lambda i: (i, 0)),
        out_shape=jax.ShapeDtypeStruct((T, W), BF16),
        compiler_params=_params(("parallel",), blk, 6 * _nbytes((R, W), F32)),
        name="spatial_gating",
    )(z, z, b_ln_g.reshape(1, W).astype(F32), b_ln_b.reshape(1, W).astype(F32), b_ws.astype(F32),
      jnp.transpose(b_bs).astype(F32))


def _t5_bucket_table(n):
    d = np.arange(n, dtype=np.int64)
    max_exact = N_BUCKETS // 2
    nf = np.maximum(d, 1).astype(np.float32)
    scaled = (np.log(nf / np.float32(max_exact)) / np.float32(math.log(MAX_DIST / max_exact))
              * np.float32(N_BUCKETS - max_exact))
    large = np.minimum(max_exact + scaled.astype(np.int32), N_BUCKETS - 1)
    return np.where(d < max_exact, d, large).astype(np.int32)


def _bias_kernel(rb_ref, bkt_ref, o_ref, *, n_heads):
    h = pl.program_id(0)
    bkt = bkt_ref[...]
    far = rb_ref[(N_BUCKETS - 1) * n_heads + h]
    out = jnp.full(bkt.shape, -jnp.inf, F32)
    for b in range(N_BUCKETS):
        out = jnp.where(bkt == b, rb_ref[b * n_heads + h] - far, out)
    o_ref[0] = out


def attn_bias_tiles(rel_bias, tq):
    n_heads = rel_bias.shape[1]
    assert tq >= MAX_DIST
    c = np.arange(2 * tq)[:, None]
    r = np.arange(tq)[None, :]
    dist = r - c + tq
    table = _t5_bucket_table(2 * tq)
    bkt = np.where(dist >= 0, table[np.maximum(dist, 0)], -1).astype(np.int32)
    grid_spec = pltpu.PrefetchScalarGridSpec(
        num_scalar_prefetch=1,
        grid=(n_heads,),
        in_specs=[pl.BlockSpec((2 * tq, tq), lambda h, rb: (0, 0))],
        out_specs=pl.BlockSpec((1, 2 * tq, tq), lambda h, rb: (h, 0, 0)),
    )
    return pl.pallas_call(
        functools.partial(_bias_kernel, n_heads=n_heads),
        grid_spec=grid_spec,
        out_shape=jax.ShapeDtypeStruct((n_heads, 2 * tq, tq), F32),
        compiler_params=_params(("arbitrary",), 2 * _nbytes((2 * tq, tq), F32)),
        name="attn_bias_tiles",
    )(rel_bias.astype(F32).reshape(-1), jnp.asarray(bkt))


ONES_ROWS = 16


def _attn_kernel(q_ref, k_ref, v_ref, nb_ref, lam_ref, cn_ref, y_ref,
                 vt_scr, m1_scr, a1_scr, m2_scr, a2_scr, *, tq, seq, lam_init):
    qi = pl.program_id(2)
    dk = HEAD_DIM // 2
    dv = HEAD_DIM
    scale = dk ** -0.5

    @pl.when(qi == 0)
    def _():
        for c in range(seq // tq):
            cs = slice(c * tq, (c + 1) * tq)
            vt_scr[:dv, cs] = v_ref[cs, :].astype(F32).T.astype(BF16)
        vt_scr[dv:, :] = jnp.ones((ONES_ROWS, seq), BF16)

    q = (q_ref[...].astype(F32) * scale).astype(BF16)
    lane = lax.broadcasted_iota(jnp.int32, q.shape, 1)
    zero = jnp.zeros_like(q)
    q1 = jnp.where(lane < dk, q, zero)
    q2 = jnp.where(lane >= dk, q, zero)
    for m_scr, a_scr in ((m1_scr, a1_scr), (m2_scr, a2_scr)):
        m_scr[...] = jnp.full(m_scr.shape, -jnp.inf, F32)
        a_scr[...] = jnp.zeros_like(a_scr)

    def update(s, vtb, m_scr, a_scr):
        m_prev = m_scr[...]
        m_new = jnp.maximum(m_prev, jnp.max(s, axis=0, keepdims=True))
        alpha = jnp.exp(m_prev - m_new)
        p = jnp.exp(s - m_new).astype(BF16)
        a_scr[...] = alpha * a_scr[...] + jnp.dot(vtb, p, preferred_element_type=F32)
        m_scr[...] = m_new

    def block(kstart, nk, bias):
        rows = pl.ds(pl.multiple_of(kstart, tq), nk)
        kb = k_ref[rows, :]
        vtb = vt_scr[:, rows]
        s1 = _dot_nt(kb, q1)
        s2 = _dot_nt(kb, q2)
        if bias is not None:
            s1 = s1 + bias
            s2 = s2 + bias
        update(s1, vtb, m1_scr, a1_scr)
        update(s2, vtb, m2_scr, a2_scr)

    n_far = jnp.maximum(qi - 1, 0)
    n4 = n_far // 4

    def far_block(j, carry):
        block(j * (4 * tq), 4 * tq, None)
        return carry

    lax.fori_loop(0, n4, far_block, 0)
    has2 = (n_far // 2) % 2
    has1 = n_far % 2

    @pl.when(has2 == 1)
    def _():
        block(n4 * (4 * tq), 2 * tq, None)

    @pl.when(has1 == 1)
    def _():
        block((n4 * 4 + has2 * 2) * tq, tq, None)

    @pl.when(qi > 0)
    def _():
        block((qi - 1) * tq, 2 * tq, nb_ref[0])

    @pl.when(qi == 0)
    def _():
        block(0, tq, nb_ref[0, tq:, :])

    lf = lam_ref[...]
    lam = (jnp.exp(jnp.sum(lf[0:1] * lf[1:2], axis=-1, keepdims=True))
           - jnp.exp(jnp.sum(lf[2:3] * lf[3:4], axis=-1, keepdims=True)) + lam_init)
    a1 = a1_scr[...]
    a2 = a2_scr[...]
    o = a1[:dv] / a1[dv:dv + 1] - lam * (a2[:dv] / a2[dv:dv + 1])
    o = o * lax.rsqrt(jnp.mean(o * o, axis=0, keepdims=True) + EPS) * cn_ref[...] * (1.0 - lam_init)
    y_ref[...] = o.T.astype(y_ref.dtype)


def attn_branch(z, col0, near_bias, c_lambda, c_norm, batch, seq, n_heads, lam_init, tq):
    T = batch * seq
    nq = seq // tq
    W = n_heads * HEAD_DIM
    blk = (2 * _nbytes((tq, HEAD_DIM), BF16) + 2 * _nbytes((seq, HEAD_DIM), BF16)
           + _nbytes((tq, 2 * tq), F32))
    return pl.pallas_call(
        functools.partial(_attn_kernel, tq=tq, seq=seq, lam_init=lam_init),
        grid=(batch, n_heads, nq),
        in_specs=[pl.BlockSpec((tq, HEAD_DIM), lambda b, h, i: (b * nq + i, col0 + h)),
                  pl.BlockSpec((seq, HEAD_DIM), lambda b, h, i: (b, col0 + n_heads + h)),
                  pl.BlockSpec((seq, HEAD_DIM), lambda b, h, i: (b, col0 + 2 * n_heads + h)),
                  pl.BlockSpec((1, 2 * tq, tq), lambda b, h, i: (h, 0, 0)),
                  pl.BlockSpec(c_lambda.shape, lambda b, h, i: (0, 0)),
                  pl.BlockSpec((HEAD_DIM, 1), lambda b, h, i: (0, 0))],
        out_specs=pl.BlockSpec((tq, HEAD_DIM), lambda b, h, i: (b * nq + i, h)),
        out_shape=jax.ShapeDtypeStruct((T, W), BF16),
        scratch_shapes=[pltpu.VMEM((HEAD_DIM + ONES_ROWS, seq), BF16),
                        pltpu.VMEM((1, tq), F32), pltpu.VMEM((HEAD_DIM + ONES_ROWS, tq), F32),
                        pltpu.VMEM((1, tq), F32), pltpu.VMEM((HEAD_DIM + ONES_ROWS, tq), F32)],
        compiler_params=_params(("parallel", "parallel", "arbitrary"), blk,
                                _nbytes((HEAD_DIM + ONES_ROWS, seq), BF16) + 48 * _nbytes((tq, tq), F32)),
        name="diff_attention",
    )(z, z, z, near_bias, c_lambda.astype(F32), c_norm.reshape(HEAD_DIM, 1).astype(F32))


def _merge_kernel(ya_ref, yb_ref, yc_ref, wa_ref, wb_ref, wc_ref, g0_ref, g1_ref, g2_ref, o_ref,
                  wa_scr, wb_scr, wc_scr):
    @pl.when(pl.program_id(1) == 0)
    def _():
        _stage_weight(wa_ref, wa_scr)
        _stage_weight(wb_ref, wb_scr)
        _stage_weight(wc_ref, wc_scr)

    da = jnp.dot(ya_ref[...], wa_scr[...], preferred_element_type=F32)
    db = jnp.dot(yb_ref[...], wb_scr[...], preferred_element_type=F32)
    dc = jnp.dot(yc_ref[...], wc_scr[...], preferred_element_type=F32)
    merged = (_sigmoid(g0_ref[...].astype(F32)) * da + _sigmoid(g1_ref[...].astype(F32)) * db
              + _sigmoid(g2_ref[...].astype(F32)) * dc)
    o_ref[...] = merged.astype(o_ref.dtype)


def gated_merge(y_a, y_b, y_c, w_a, w_b, w_c, layer, z, gate_col0, d_model):
    T = z.shape[0]
    goff = gate_col0 * HEAD_DIM
    tn = _divisor(math.gcd(goff, d_model), (512, 256, 128))
    ka, kb, kc = y_a.shape[1], y_b.shape[1], y_c.shape[1]
    ks = ka + kb + kc

    def blk(tm):
        return _nbytes((tm, ks), BF16) + _nbytes((ks, tn), F32) + 4 * _nbytes((tm, tn), BF16)

    def scr(tm):
        return _nbytes((ks, tn), BF16) + 6 * _nbytes((tm, tn), F32)

    tm = _row_tile(T, blk, scr)

    def gspec(j):
        base = (goff + j * d_model) // tn
        return pl.BlockSpec((tm, tn), lambda n, i: (i, base + n))

    def wspec(k):
        return pl.BlockSpec((1, k, tn), lambda n, i: (layer, 0, n))

    return pl.pallas_call(
        _merge_kernel,
        grid=(d_model // tn, T // tm),
        in_specs=[pl.BlockSpec((tm, ka), lambda n, i: (i, 0)),
                  pl.BlockSpec((tm, kb), lambda n, i: (i, 0)),
                  pl.BlockSpec((tm, kc), lambda n, i: (i, 0)),
                  wspec(ka), wspec(kb), wspec(kc), gspec(0), gspec(1), gspec(2)],
        out_specs=pl.BlockSpec((tm, tn), lambda n, i: (i, n)),
        out_shape=jax.ShapeDtypeStruct((T, d_model), BF16),
        scratch_shapes=[pltpu.VMEM((ka, tn), BF16), pltpu.VMEM((kb, tn), BF16), pltpu.VMEM((kc, tn), BF16)],
        compiler_params=_params(("parallel", "arbitrary"), blk(tm), scr(tm)),
        name="gated_merge",
    )(y_a, y_b, y_c, w_a, w_b, w_c, z, z, z)


def _route_kernel(l_ref, idx_ref, w_ref, *, n_experts):
    logits = l_ref[...]
    lane = lax.broadcasted_iota(jnp.int32, logits.shape, 1)
    logits = jnp.where(lane < n_experts, logits, -jnp.inf)
    m1 = jnp.max(logits, axis=-1, keepdims=True)
    i1 = jnp.min(jnp.where(logits == m1, lane, LANES), axis=-1, keepdims=True)
    rest = jnp.where(lane == i1, -jnp.inf, logits)
    m2 = jnp.max(rest, axis=-1, keepdims=True)
    i2 = jnp.min(jnp.where(rest == m2, lane, LANES), axis=-1, keepdims=True)
    e = jnp.exp(m2 - m1)
    w1 = 1.0 / (1.0 + e)
    w2 = e / (1.0 + e)
    idx_ref[...] = jnp.where(lane == 0, i1, jnp.where(lane == 1, i2, 0))
    w_ref[...] = jnp.where(lane == 0, w1, jnp.where(lane == 1, w2, 0.0))


def route_top2(logits, n_experts):
    T = logits.shape[0]
    tm = _divisor(T, (1024, 512, 256, 128, 8))
    spec = pl.BlockSpec((tm, LANES), lambda i: (i, 0))
    idx, w = pl.pallas_call(
        functools.partial(_route_kernel, n_experts=n_experts),
        grid=(T // tm,),
        in_specs=[spec],
        out_specs=[spec, spec],
        out_shape=[jax.ShapeDtypeStruct((T, LANES), jnp.int32), jax.ShapeDtypeStruct((T, LANES), F32)],
        compiler_params=_params(("parallel",), 3 * _nbytes((tm, LANES), F32)),
        name="route_top2",
    )(logits)
    return idx[:, :TOP_K], w


def _moe_up_kernel(te_ref, tv_ref, src_ref, x_hbm, wg_ref, wu_ref, o_ref, xrow_scr, xbf_scr, sem, *, tm):
    i = pl.program_id(0)
    j = pl.program_id(1)
    valid = tv_ref[i] > 0

    def row_copy(r, tok):
        return pltpu.make_async_copy(x_hbm.at[pl.ds(tok, 1), :], xrow_scr.at[pl.ds(r, 1), :], sem)

    @pl.when(jnp.logical_and(valid, j == 0))
    def _():
        def start(r, carry):
            row_copy(r, src_ref[i * tm + r]).start()
            return carry

        def wait(r, carry):
            row_copy(r, 0).wait()
            return carry

        lax.fori_loop(0, tm, start, 0)
        lax.fori_loop(0, tm, wait, 0)
        xbf_scr[...] = xrow_scr[...].astype(BF16)

    @pl.when(valid)
    def _():
        x = xbf_scr[...]
        g = jnp.dot(x, wg_ref[0], preferred_element_type=F32)
        u = jnp.dot(x, wu_ref[0], preferred_element_type=F32)
        o_ref[...] = (g * _sigmoid(g) * u).astype(o_ref.dtype)

    @pl.when(jnp.logical_not(valid))
    def _():
        o_ref[...] = jnp.zeros_like(o_ref)


def _moe_down_kernel(te_ref, tv_ref, a_ref, wd_ref, o_ref):
    i = pl.program_id(0)

    @pl.when(tv_ref[i] > 0)
    def _():
        o_ref[...] = jnp.dot(a_ref[...], wd_ref[0], preferred_element_type=F32)

    @pl.when(tv_ref[i] == 0)
    def _():
        o_ref[...] = jnp.zeros_like(o_ref)


def _moe_combine_kernel(dest_ref, y_hbm, h_ref, w_ref, o_ref, buf0, buf1, sem, *, tm):
    i = pl.program_id(0)

    def row_copy(r, k, row):
        buf = buf0 if k == 0 else buf1
        return pltpu.make_async_copy(y_hbm.at[pl.ds(row, 1), :], buf.at[pl.ds(r, 1), :], sem.at[k])

    def start(r, carry):
        a = TOP_K * (i * tm + r)
        row_copy(r, 0, dest_ref[a]).start()
        row_copy(r, 1, dest_ref[a + 1]).start()
        return carry

    def wait(r, carry):
        row_copy(r, 0, 0).wait()
        row_copy(r, 1, 0).wait()
        return carry

    lax.fori_loop(0, tm, start, 0)
    lax.fori_loop(0, tm, wait, 0)
    w = w_ref[...]
    o_ref[...] = h_ref[...] + (w[:, 0:1] * buf0[...] + w[:, 1:2] * buf1[...])


def moe_ffn(h, hn, logits, wg, wu, wd):
    T, D = hn.shape
    E, _, F = wg.shape
    idx, wts = route_top2(logits, E)
    tm = _divisor(T, (512, 256, 128))
    A = T * TOP_K
    n_tiles = A // tm + E

    e_flat = idx.reshape(-1)
    onehot = (e_flat[:, None] == jnp.arange(E, dtype=jnp.int32)[None, :]).astype(jnp.int32)
    csum = jnp.cumsum(onehot, axis=0)
    pos = jnp.take_along_axis(csum, e_flat[:, None], axis=1)[:, 0] - 1
    counts = csum[-1]
    padded = ((counts + tm - 1) // tm) * tm
    ends = jnp.cumsum(padded)
    starts = ends - padded
    dest = (starts[e_flat] + pos).astype(jnp.int32)
    tile_start = jnp.arange(n_tiles, dtype=jnp.int32) * tm
    tile_expert = jnp.minimum(jnp.sum((tile_start[:, None] >= ends[None, :]).astype(jnp.int32), axis=1), E - 1)
    tile_valid = (tile_start < ends[-1]).astype(jnp.int32)
    src = jnp.zeros((n_tiles * tm,), jnp.int32).at[dest].set(jnp.arange(A, dtype=jnp.int32) // TOP_K)

    tf = _divisor(F, (512, 256, 128))
    blk_up = 2 * _nbytes((D, tf), BF16) + _nbytes((tm, tf), BF16)
    act = pl.pallas_call(
        functools.partial(_moe_up_kernel, tm=tm),
        grid_spec=pltpu.PrefetchScalarGridSpec(
            num_scalar_prefetch=3,
            grid=(n_tiles, F // tf),
            in_specs=[pl.BlockSpec(memory_space=pl.ANY),
                      pl.BlockSpec((1, D, tf), lambda i, j, te, tv, sr: (te[i], 0, j)),
                      pl.BlockSpec((1, D, tf), lambda i, j, te, tv, sr: (te[i], 0, j))],
            out_specs=pl.BlockSpec((tm, tf), lambda i, j, te, tv, sr: (i, j)),
            scratch_shapes=[pltpu.VMEM((tm, D), F32), pltpu.VMEM((tm, D), BF16), pltpu.SemaphoreType.DMA(())],
        ),
        out_shape=jax.ShapeDtypeStruct((n_tiles * tm, F), BF16),
        compiler_params=_params(("arbitrary", "arbitrary"), blk_up,
                                _nbytes((tm, D), F32) + _nbytes((tm, D), BF16) + 3 * _nbytes((tm, tf), F32)),
        name="moe_up",
    )(tile_expert, tile_valid, src, hn, wg, wu)

    tn = _divisor(D, (512, 256, 128))
    blk_dn = _nbytes((tm, F), BF16) + _nbytes((F, tn), BF16) + _nbytes((tm, tn), F32)
    y_sorted = pl.pallas_call(
        _moe_down_kernel,
        grid_spec=pltpu.PrefetchScalarGridSpec(
            num_scalar_prefetch=2,
            grid=(n_tiles, D // tn),
            in_specs=[pl.BlockSpec((tm, F), lambda i, j, te, tv: (i, 0)),
                      pl.BlockSpec((1, F, tn), lambda i, j, te, tv: (te[i], 0, j))],
            out_specs=pl.BlockSpec((tm, tn), lambda i, j, te, tv: (i, j)),
        ),
        out_shape=jax.ShapeDtypeStruct((n_tiles * tm, D), F32),
        compiler_params=_params(("parallel", "parallel"), blk_dn, _nbytes((tm, tn), F32)),
        name="moe_down",
    )(tile_expert, tile_valid, act, wd)

    tc = _divisor(T, (256, 128))
    blk_c = 2 * _nbytes((tc, D), F32) + _nbytes((tc, LANES), F32)
    return pl.pallas_call(
        functools.partial(_moe_combine_kernel, tm=tc),
        grid_spec=pltpu.PrefetchScalarGridSpec(
            num_scalar_prefetch=1,
            grid=(T // tc,),
            in_specs=[pl.BlockSpec(memory_space=pl.ANY),
                      pl.BlockSpec((tc, D), lambda i, de: (i, 0)),
                      pl.BlockSpec((tc, LANES), lambda i, de: (i, 0))],
            out_specs=pl.BlockSpec((tc, D), lambda i, de: (i, 0)),
            scratch_shapes=[pltpu.VMEM((tc, D), F32), pltpu.VMEM((tc, D), F32), pltpu.SemaphoreType.DMA((2,))],
        ),
        out_shape=jax.ShapeDtypeStruct((T, D), F32),
        compiler_params=_params(("arbitrary",), blk_c, 3 * _nbytes((tc, D), F32)),
        name="moe_combine",
    )(dest, y_sorted, h, wts)


def _mixer(h, li, batch, seq, near_bias, tq, norm_mix, w_in, a_gate_bias, a_conv, a_norm, b_ln_g, b_ln_b, b_ws,
           b_bs, c_lambda, c_norm, w_br_a, w_br_b, w_br_c, w_out):
    d_model = h.shape[1]
    h_a = a_gate_bias.shape[1] // 2
    g_b = b_ws.shape[1]
    h_c = w_br_c.shape[1] // HEAD_DIM
    w_a = h_a * HEAD_DIM
    g0 = 4 * w_a
    n_rest = w_in.shape[2] - g0 - 2 * h_a
    n, zg = rmsnorm(h, norm_mix[li], proj=w_in[li, :, g0:g0 + 2 * h_a])
    z_a = matmul(n, w_in, li, BF16, n=g0)
    z_r = matmul(n, w_in, li, BF16, w_col=g0, n=n_rest, shift=2 * h_a)
    y_a = mlstm_branch(z_a, zg, a_gate_bias[li], a_conv[li], a_norm[li], batch, seq, h_a)
    y_b = sgu_branch(z_r, b_ln_g[li], b_ln_b[li], b_ws[li], b_bs[li], g_b)
    lam_init = 0.8 - 0.6 * math.exp(-0.3 * li)
    y_c = attn_branch(z_r, 2 * g_b, near_bias, c_lambda[li], c_norm[li], batch, seq, h_c, lam_init, tq)
    merged = gated_merge(y_a, y_b, y_c, w_br_a, w_br_b, w_br_c, li, z_r, 2 * g_b + 3 * h_c, d_model)
    return matmul_residual(merged, w_out, li, h)


def kernel(x, norm_mix, w_in, a_gate_bias, a_conv, a_norm, b_ln_g, b_ln_b, b_ws, b_bs, c_lambda, c_norm, rel_bias,
           w_br_a, w_br_b, w_br_c, w_out, norm_ffn, ffn_wg, ffn_wu, ffn_wd, router, moe_wg, moe_wu, moe_wd,
           final_norm):
    batch, seq, d_model = x.shape
    depth = w_in.shape[0]
    tq = _divisor(seq, (256, 128))
    near_bias = attn_bias_tiles(rel_bias, tq)
    h = x.reshape(batch * seq, d_model).astype(F32)
    for li in range(depth):
        h = _mixer(h, li, batch, seq, near_bias, tq, norm_mix, w_in, a_gate_bias, a_conv, a_norm, b_ln_g, b_ln_b,
                   b_ws, b_bs, c_lambda, c_norm, w_br_a, w_br_b, w_br_c, w_out)
        j = li // 2
        if li % 2 == 0:
            hn = rmsnorm(h, norm_ffn[li])
            act = swiglu_up(hn, ffn_wg, ffn_wu, j)
            h = matmul_residual(act, ffn_wd, j, h)
        else:
            hn, logits = rmsnorm(h, norm_ffn[li], proj=router[j], out_dtype=F32)
            h = moe_ffn(h, hn, logits, moe_wg[j].astype(BF16), moe_wu[j].astype(BF16), moe_wd[j].astype(BF16))
    out = rmsnorm(h, final_norm, out_dtype=F32)
    return out.reshape(batch, seq, d_model).astype(x.dtype)
```

```python
import functools
import math

import numpy as np
import jax
import jax.numpy as jnp
from jax import lax
from jax.experimental import pallas as pl
from jax.experimental.pallas import tpu as pltpu

F32 = jnp.float32
BF16 = jnp.bfloat16

EPS = 1e-6
HEAD_DIM = 128
CHUNK = 128
CONV_W = 4
N_BUCKETS = 32
MAX_DIST = 128
TOP_K = 2
LANES = 128
VMEM_BYTES_V7X = 64 * 1024 * 1024
VMEM_SLACK = 6 * 1024 * 1024
VMEM_BUDGET = VMEM_BYTES_V7X - 8 * 1024 * 1024
WEIGHT_STAGE_ROWS = 256


def _divisor(n, candidates):
    for c in candidates:
        if n % c == 0:
            return c
    raise ValueError(f"no tile size in {candidates} divides {n}")


def _vmem_need(block_bytes, scratch_bytes):
    return 2 * block_bytes + scratch_bytes + VMEM_SLACK


def _params(semantics, block_bytes, scratch_bytes=0):
    limit = int(min(max(_vmem_need(block_bytes, scratch_bytes), 16 * 1024 * 1024), VMEM_BYTES_V7X - 4 * 1024 * 1024))
    return pltpu.CompilerParams(dimension_semantics=semantics, vmem_limit_bytes=limit)


def _row_tile(T, block_bytes, scratch_bytes):
    for tm in (1024, 512, 256, 128):
        if T % tm == 0 and _vmem_need(block_bytes(tm), scratch_bytes(tm)) <= VMEM_BUDGET:
            return tm
    raise ValueError("no row tile fits VMEM")


def _nbytes(shape, dtype):
    return int(np.prod(shape)) * jnp.dtype(dtype).itemsize


def _sigmoid(x):
    return 1.0 / (1.0 + jnp.exp(-x))


def _split_bf16(x):
    hi = x.astype(BF16)
    lo = (x - hi.astype(F32)).astype(BF16)
    return hi, lo


def _dot_nt(a, b):
    return lax.dot_general(a, b, (((1,), (1,)), ((), ())), preferred_element_type=F32)


def _dot_tn(a, b):
    return lax.dot_general(a, b, (((0,), (0,)), ((), ())), preferred_element_type=F32)


def _norm_kernel(*refs, has_proj, out_dtype):
    if has_proj:
        x_ref, g_ref, whi_ref, wlo_ref, y_ref, p_ref = refs
    else:
        x_ref, g_ref, y_ref = refs
    x = x_ref[...]
    y = x * lax.rsqrt(jnp.mean(x * x, axis=-1, keepdims=True) + EPS) * g_ref[...]
    y_ref[...] = y.astype(out_dtype)
    if has_proj:
        y_hi, y_lo = _split_bf16(y)
        w_hi = whi_ref[...]
        p = jnp.dot(y_hi, w_hi, preferred_element_type=F32)
        p += jnp.dot(y_hi, wlo_ref[...], preferred_element_type=F32)
        p += jnp.dot(y_lo, w_hi, preferred_element_type=F32)
        p_ref[...] = p


def rmsnorm(x, g, proj=None, out_dtype=BF16):
    T, D = x.shape
    tm = _divisor(T, (256, 128, 8))
    g2 = g.reshape(1, D).astype(F32)
    in_specs = [pl.BlockSpec((tm, D), lambda i: (i, 0)), pl.BlockSpec((1, D), lambda i: (0, 0))]
    out_shape = [jax.ShapeDtypeStruct((T, D), out_dtype)]
    out_specs = [pl.BlockSpec((tm, D), lambda i: (i, 0))]
    args = [x, g2]
    blk = _nbytes((tm, D), F32) + _nbytes((tm, D), out_dtype)
    if proj is not None:
        n = proj.shape[1]
        assert n <= LANES
        w = jnp.pad(proj.astype(F32), ((0, 0), (0, LANES - n)))
        w_hi, w_lo = _split_bf16(w)
        in_specs += [pl.BlockSpec((D, LANES), lambda i: (0, 0))] * 2
        out_shape.append(jax.ShapeDtypeStruct((T, LANES), F32))
        out_specs.append(pl.BlockSpec((tm, LANES), lambda i: (i, 0)))
        args += [w_hi, w_lo]
        blk += 2 * _nbytes((D, LANES), BF16)
    outs = pl.pallas_call(
        functools.partial(_norm_kernel, has_proj=proj is not None, out_dtype=out_dtype),
        grid=(T // tm,),
        in_specs=in_specs,
        out_specs=out_specs,
        out_shape=out_shape,
        compiler_params=_params(("parallel",), blk, 4 * _nbytes((tm, D), F32)),
        name="rmsnorm",
    )(*args)
    return (outs[0], outs[1]) if proj is not None else outs[0]


def _stage_weight(w_ref, w_scr, wx_ref=None, shift=0):
    K, tn = w_scr.shape
    rc = _divisor(K, (WEIGHT_STAGE_ROWS, LANES))

    def step(r, carry):
        rows = pl.ds(pl.multiple_of(r * rc, rc), rc)
        w = w_ref[0, rows, :]
        if shift:
            wide = jnp.concatenate([w, wx_ref[0, rows, :]], axis=1)
            w = pltpu.roll(wide, wide.shape[1] - shift, 1)[:, :tn]
        w_scr[rows, :] = w.astype(BF16)
        return carry

    lax.fori_loop(0, K // rc, step, 0)


def _matmul_kernel(*refs, has_res, shift):
    a_ref, w_ref = refs[:2]
    wx_ref = refs[2] if shift else None
    r_ref = refs[2 + bool(shift)] if has_res else None
    o_ref, w_scr = refs[-2:]

    @pl.when(pl.program_id(1) == 0)
    def _():
        _stage_weight(w_ref, w_scr, wx_ref, shift)

    d = jnp.dot(a_ref[...], w_scr[...], preferred_element_type=F32)
    if has_res:
        d = d + r_ref[...]
    o_ref[...] = d.astype(o_ref.dtype)


def matmul(a, w, layer, out_dtype, *, a_col=0, k=None, w_row=0, w_col=0, n=None, shift=0, res=None):
    T = a.shape[0]
    k = a.shape[1] if k is None else k
    n = w.shape[2] - w_col if n is None else n
    tn = _divisor(math.gcd(n, w_col) if w_col else n, (512, 256, 128))
    assert a_col % k == 0 and w_row % k == 0 and 0 <= shift < LANES
    extra = _nbytes((k, LANES), F32) if shift else 0
    res_b = (lambda tm: _nbytes((tm, tn), F32)) if res is not None else (lambda tm: 0)

    def blk(tm):
        return _nbytes((tm, k), BF16) + _nbytes((k, tn), F32) + extra + _nbytes((tm, tn), out_dtype) + res_b(tm)

    def scr(tm):
        return _nbytes((k, tn), BF16) + 2 * _nbytes((tm, tn), F32)

    tm = _row_tile(T, blk, scr)
    ab, wrb, wcb = a_col // k, w_row // k, w_col // tn
    in_specs = [pl.BlockSpec((tm, k), lambda j, i: (i, ab)),
                pl.BlockSpec((1, k, tn), lambda j, i: (layer, wrb, wcb + j))]
    args = [a, w]
    if shift:
        per = tn // LANES
        xb = w_col // LANES + per
        in_specs.append(pl.BlockSpec((1, k, LANES), lambda j, i: (layer, wrb, xb + per * j)))
        args.append(w)
    if res is not None:
        in_specs.append(pl.BlockSpec((tm, tn), lambda j, i: (i, j)))
        args.append(res)
    return pl.pallas_call(
        functools.partial(_matmul_kernel, has_res=res is not None, shift=shift),
        grid=(n // tn, T // tm),
        in_specs=in_specs,
        out_specs=pl.BlockSpec((tm, tn), lambda j, i: (i, j)),
        out_shape=jax.ShapeDtypeStruct((T, n), out_dtype),
        scratch_shapes=[pltpu.VMEM((k, tn), BF16)],
        compiler_params=_params(("parallel", "arbitrary"), blk(tm), scr(tm)),
        name="matmul",
    )(*args)


def matmul_residual(a, w, layer, res):
    K = a.shape[1]
    nk = next(n for n in range(1, K // LANES + 1) if K % (n * LANES) == 0 and K // n <= 6144)
    k = K // nk
    out = res
    for p in range(nk):
        out = matmul(a, w, layer, F32, a_col=p * k, k=k, w_row=p * k, res=out)
    return out


def _swiglu_up_kernel(x_ref, wg_ref, wu_ref, o_ref, wg_scr, wu_scr):
    @pl.when(pl.program_id(1) == 0)
    def _():
        _stage_weight(wg_ref, wg_scr)
        _stage_weight(wu_ref, wu_scr)

    x = x_ref[...]
    g = jnp.dot(x, wg_scr[...], preferred_element_type=F32)
    u = jnp.dot(x, wu_scr[...], preferred_element_type=F32)
    o_ref[...] = (g * _sigmoid(g) * u).astype(o_ref.dtype)


def swiglu_up(x, wg, wu, layer):
    T, K = x.shape
    F = wg.shape[2]
    tn = _divisor(F, (512, 256, 128))

    def blk(tm):
        return _nbytes((tm, K), BF16) + 2 * _nbytes((K, tn), F32) + _nbytes((tm, tn), BF16)

    def scr(tm):
        return 2 * _nbytes((K, tn), BF16) + 3 * _nbytes((tm, tn), F32)

    tm = _row_tile(T, blk, scr)
    wspec = pl.BlockSpec((1, K, tn), lambda j, i: (layer, 0, j))
    return pl.pallas_call(
        _swiglu_up_kernel,
        grid=(F // tn, T // tm),
        in_specs=[pl.BlockSpec((tm, K), lambda j, i: (i, 0)), wspec, wspec],
        out_specs=pl.BlockSpec((tm, tn), lambda j, i: (i, j)),
        out_shape=jax.ShapeDtypeStruct((T, F), BF16),
        scratch_shapes=[pltpu.VMEM((K, tn), BF16), pltpu.VMEM((K, tn), BF16)],
        compiler_params=_params(("parallel", "arbitrary"), blk(tm), scr(tm)),
        name="swiglu_up",
    )(x, wg, wu)


def _split_bf16_f32(x):
    hi = x.astype(BF16)
    return hi, x - hi.astype(F32)


def _mlstm_kernel(gb_ref, q_ref, k_ref, v_ref, og_ref, ig_ref, fg_ref, cwq_ref, cwk_ref, an_ref, y_ref,
                  c_scr, n_scr, m_scr, pq_scr, pk_scr, *, n_heads, n_chunks, group):
    hg = pl.program_id(1)
    L = CHUNK
    c_scr[...] = jnp.zeros_like(c_scr)
    n_scr[...] = jnp.zeros_like(n_scr)
    m_scr[...] = jnp.zeros_like(m_scr)
    pq_scr[...] = jnp.zeros_like(pq_scr)
    pk_scr[...] = jnp.zeros_like(pk_scr)
    ri = lax.broadcasted_iota(jnp.int32, (L, L), 0)
    ci = lax.broadcasted_iota(jnp.int32, (L, L), 1)
    lower = ci <= ri
    strict_lower01 = jnp.where(ri > ci, 1.0, 0.0).astype(BF16)

    def conv_silu(x, prev, w):
        y = x * w[CONV_W - 1:CONV_W, :]
        for s in range(1, CONV_W):
            shifted = jnp.where(ri < s, pltpu.roll(prev, s, 0), pltpu.roll(x, s, 0))
            y = y + shifted * w[CONV_W - 1 - s:CONV_W - s, :]
        return y * _sigmoid(y)

    def head_chunk(c, rows, g):
        cols = slice(g * HEAD_DIM, (g + 1) * HEAD_DIM)
        h = hg * group + g
        xq = q_ref[rows, cols].astype(F32)
        xk = k_ref[rows, cols].astype(F32)
        q = conv_silu(xq, pq_scr[g], cwq_ref[:, cols])
        k = conv_silu(xk, pk_scr[g], cwk_ref[:, cols]) * (HEAD_DIM ** -0.5)
        pq_scr[g] = xq
        pk_scr[g] = xk
        v = v_ref[rows, cols]
        q_bf = q.astype(BF16)
        k_bf = k.astype(BF16)

        i_row = ig_ref[0, g, pl.ds(c, 1), :] + gb_ref[h]
        f_row = fg_ref[0, g, pl.ds(c, 1), :] + gb_ref[n_heads + h]
        lf = jnp.minimum(f_row, 0.0) - jnp.log(1.0 + jnp.exp(-jnp.abs(f_row)))
        lf_low = jnp.where(lower, lf, 0.0)
        b_col = jnp.sum(lf_low, axis=1, keepdims=True)
        hi, rest = _split_bf16_f32(lf_low)
        mid, lo = _split_bf16_f32(rest)
        dmat = (jnp.dot(hi, strict_lower01, preferred_element_type=F32)
                + jnp.dot(mid, strict_lower01, preferred_element_type=F32)
                + jnp.dot(lo.astype(BF16), strict_lower01, preferred_element_type=F32))
        g_tot = b_col[L - 1:L, :]
        a_row = dmat[L - 1:L, :] + i_row
        a_col = jnp.sum(jnp.where(ci > ri, lf, 0.0) + jnp.where(ci == ri, i_row, 0.0),
                        axis=1, keepdims=True)

        m_prev = m_scr[g]
        c_prev = c_scr[g]
        n_prev = n_scr[g]

        log_d = jnp.where(lower, dmat + i_row, -jnp.inf)
        log_inter = b_col + m_prev
        m_t = jnp.maximum(log_inter, jnp.max(log_d, axis=1, keepdims=True))
        w = jnp.exp(log_d - m_t) * _dot_nt(q_bf, k_bf)
        e_inter = jnp.exp(log_inter - m_t)
        num = (e_inter * jnp.dot(q_bf, c_prev.astype(BF16), preferred_element_type=F32)
               + jnp.dot(w.astype(BF16), v, preferred_element_type=F32))
        den = e_inter * jnp.sum(q * n_prev, axis=1, keepdims=True) + jnp.sum(w, axis=1, keepdims=True)
        hcell = num / jnp.maximum(jnp.abs(den), jnp.exp(-m_t))

        gated = _sigmoid(og_ref[rows, cols].astype(F32)) * hcell
        y = gated * lax.rsqrt(jnp.mean(gated * gated, axis=-1, keepdims=True) + EPS) * an_ref[:, cols]
        y_ref[rows, cols] = y.astype(y_ref.dtype)

        m_new = jnp.maximum(g_tot + m_prev, jnp.max(a_row, axis=1, keepdims=True))
        w_col = jnp.exp(a_col - m_new)
        decay = jnp.exp(g_tot + m_prev - m_new)
        c_scr[g] = decay * c_prev + _dot_tn(k_bf, (v.astype(F32) * w_col).astype(BF16))
        n_scr[g] = decay * n_prev + jnp.sum(k * w_col, axis=0, keepdims=True)
        m_scr[g] = m_new

    def chunk(c, carry):
        rows = pl.ds(pl.multiple_of(c * L, L), L)
        for g in range(group):
            head_chunk(c, rows, g)
        return carry

    lax.fori_loop(0, n_chunks, chunk, 0)


def mlstm_branch(z, zg, a_gate_bias, a_conv, a_norm, batch, seq, n_heads):
    T = batch * seq
    nc = seq // CHUNK
    W = n_heads * HEAD_DIM
    group = _divisor(n_heads, (2, 1))
    ng = n_heads // group
    gw = group * HEAD_DIM

    def gate_rows(g):
        return jnp.transpose(g.reshape(batch, seq, n_heads), (0, 2, 1)).reshape(batch, n_heads, nc, CHUNK)

    ig = gate_rows(zg[:, :n_heads])
    fg = gate_rows(zg[:, n_heads:2 * n_heads])

    def zspec(part):
        return pl.BlockSpec((seq, gw), lambda b, h, gb: (b, part * ng + h))

    gspec = pl.BlockSpec((1, group, nc, CHUNK), lambda b, h, gb: (b, h, 0, 0))
    blk = 5 * _nbytes((seq, gw), BF16) + 2 * _nbytes((group, nc, CHUNK), F32)
    grid_spec = pltpu.PrefetchScalarGridSpec(
        num_scalar_prefetch=1,
        grid=(batch, ng),
        in_specs=[zspec(0), zspec(1), zspec(2), zspec(3), gspec, gspec,
                  pl.BlockSpec((CONV_W, gw), lambda b, h, gb: (0, h)),
                  pl.BlockSpec((CONV_W, gw), lambda b, h, gb: (0, ng + h)),
                  pl.BlockSpec((1, gw), lambda b, h, gb: (0, h))],
        out_specs=pl.BlockSpec((seq, gw), lambda b, h, gb: (b, h)),
        scratch_shapes=[pltpu.VMEM((group, HEAD_DIM, HEAD_DIM), F32), pltpu.VMEM((group, 1, HEAD_DIM), F32),
                        pltpu.VMEM((group, 1, 1), F32), pltpu.VMEM((group, CHUNK, HEAD_DIM), F32),
                        pltpu.VMEM((group, CHUNK, HEAD_DIM), F32)],
    )
    return pl.pallas_call(
        functools.partial(_mlstm_kernel, n_heads=n_heads, n_chunks=nc, group=group),
        grid_spec=grid_spec,
        out_shape=jax.ShapeDtypeStruct((T, W), BF16),
        compiler_params=_params(("parallel", "parallel"), blk, 64 * group * _nbytes((CHUNK, HEAD_DIM), F32)),
        name="mlstm",
    )(a_gate_bias.astype(F32), z, z, z, z, ig, fg, a_conv.astype(F32), a_conv.astype(F32),
      a_norm.reshape(1, W).astype(F32))


def _sgu_kernel(u_ref, v_ref, lng_ref, lnb_ref, ws_ref, bst_ref, y_ref, *, n_groups, chunks_per_block):
    L = CHUNK
    c0 = math.sqrt(2.0 / math.pi)

    def gelu(x):
        return x * (0.5 * (1.0 + jnp.tanh(c0 * (x + 0.044715 * (x * x * x)))))

    v = gelu(v_ref[...].astype(F32))
    mu = jnp.mean(v, axis=-1, keepdims=True)
    vc = v - mu
    vn = vc * lax.rsqrt(jnp.mean(vc * vc, axis=-1, keepdims=True) + EPS) * lng_ref[...] + lnb_ref[...]
    vn = vn.astype(BF16)
    ri = lax.broadcasted_iota(jnp.int32, (L, L), 0)
    ci = lax.broadcasted_iota(jnp.int32, (L, L), 1)
    bst = bst_ref[...]
    for g in range(n_groups):
        cs = slice(g * HEAD_DIM, (g + 1) * HEAD_DIM)
        w = jnp.where(ci <= ri, ws_ref[g], 0.0).astype(BF16)
        bias = bst[:, g:g + 1]
        for c in range(chunks_per_block):
            rs = slice(c * L, (c + 1) * L)
            s = jnp.dot(w, vn[rs, cs], preferred_element_type=F32) + bias
            y_ref[rs, cs] = (gelu(u_ref[rs, cs].astype(F32)) * s).astype(y_ref.dtype)


def sgu_branch(z, b_ln_g, b_ln_b, b_ws, b_bs, n_groups):
    T = z.shape[0]
    W = n_groups * HEAD_DIM
    cpb = _divisor(T // CHUNK, (4, 2, 1))
    R = cpb * CHUNK
    blk = 3 * _nbytes((R, W), BF16) + _nbytes((n_groups, CHUNK, CHUNK), F32)
    return pl.pallas_call(
        functools.partial(_sgu_kernel, n_groups=n_groups, chunks_per_block=cpb),
        grid=(T // R,),
        in_specs=[pl.BlockSpec((R, W), lambda i: (i, 0)),
                  pl.BlockSpec((R, W), lambda i: (i, 1)),
                  pl.BlockSpec((1, W), lambda i: (0, 0)),
                  pl.BlockSpec((1, W), lambda i: (0, 0)),
                  pl.BlockSpec((n_groups, CHUNK, CHUNK), lambda i: (0, 0, 0)),
                  pl.BlockSpec((CHUNK, n_groups), lambda i: (0, 0))],
        out_specs=pl.BlockSpec((R, W), lambda i: (i, 0)),
        out_shape=jax.ShapeDtypeStruct((T, W), BF16),
        compiler_params=_params(("parallel",), blk, 6 * _nbytes((R, W), F32)),
        name="spatial_gating",
    )(z, z, b_ln_g.reshape(1, W).astype(F32), b_ln_b.reshape(1, W).astype(F32), b_ws.astype(F32),
      jnp.transpose(b_bs).astype(F32))


def _t5_bucket_table(n):
    d = np.arange(n, dtype=np.int64)
    max_exact = N_BUCKETS // 2
    nf = np.maximum(d, 1).astype(np.float32)
    scaled = (np.log(nf / np.float32(max_exact)) / np.float32(math.log(MAX_DIST / max_exact))
              * np.float32(N_BUCKETS - max_exact))
    large = np.minimum(max_exact + scaled.astype(np.int32), N_BUCKETS - 1)
    return np.where(d < max_exact, d, large).astype(np.int32)


def _bias_kernel(rb_ref, bkt_ref, o_ref, *, n_heads):
    h = pl.program_id(0)
    bkt = bkt_ref[...]
    far = rb_ref[(N_BUCKETS - 1) * n_heads + h]
    out = jnp.full(bkt.shape, -jnp.inf, F32)
    for b in range(N_BUCKETS):
        out = jnp.where(bkt == b, rb_ref[b * n_heads + h] - far, out)
    o_ref[0] = out


def attn_bias_tiles(rel_bias, tq):
    n_heads = rel_bias.shape[1]
    assert tq >= MAX_DIST
    c = np.arange(2 * tq)[:, None]
    r = np.arange(tq)[None, :]
    dist = r - c + tq
    table = _t5_bucket_table(2 * tq)
    bkt = np.where(dist >= 0, table[np.maximum(dist, 0)], -1).astype(np.int32)
    grid_spec = pltpu.PrefetchScalarGridSpec(
        num_scalar_prefetch=1,
        grid=(n_heads,),
        in_specs=[pl.BlockSpec((2 * tq, tq), lambda h, rb: (0, 0))],
        out_specs=pl.BlockSpec((1, 2 * tq, tq), lambda h, rb: (h, 0, 0)),
    )
    return pl.pallas_call(
        functools.partial(_bias_kernel, n_heads=n_heads),
        grid_spec=grid_spec,
        out_shape=jax.ShapeDtypeStruct((n_heads, 2 * tq, tq), F32),
        compiler_params=_params(("arbitrary",), 2 * _nbytes((2 * tq, tq), F32)),
        name="attn_bias_tiles",
    )(rel_bias.astype(F32).reshape(-1), jnp.asarray(bkt))


ONES_ROWS = 16


def _attn_kernel(q_ref, k_ref, v_ref, nb_ref, lam_ref, cn_ref, y_ref,
                 vt_scr, m1_scr, a1_scr, m2_scr, a2_scr, *, tq, seq, lam_init):
    qi = pl.program_id(2)
    dk = HEAD_DIM // 2
    dv = HEAD_DIM
    scale = dk ** -0.5

    @pl.when(qi == 0)
    def _():
        for c in range(seq // tq):
            cs = slice(c * tq, (c + 1) * tq)
            vt_scr[:dv, cs] = v_ref[cs, :].astype(F32).T.astype(BF16)
        vt_scr[dv:, :] = jnp.ones((ONES_ROWS, seq), BF16)

    q = (q_ref[...].astype(F32) * scale).astype(BF16)
    lane = lax.broadcasted_iota(jnp.int32, q.shape, 1)
    zero = jnp.zeros_like(q)
    q1 = jnp.where(lane < dk, q, zero)
    q2 = jnp.where(lane >= dk, q, zero)
    for m_scr, a_scr in ((m1_scr, a1_scr), (m2_scr, a2_scr)):
        m_scr[...] = jnp.full(m_scr.shape, -jnp.inf, F32)
        a_scr[...] = jnp.zeros_like(a_scr)

    def update(s, vtb, m_scr, a_scr):
        m_prev = m_scr[...]
        m_new = jnp.maximum(m_prev, jnp.max(s, axis=0, keepdims=True))
        alpha = jnp.exp(m_prev - m_new)
        p = jnp.exp(s - m_new).astype(BF16)
        a_scr[...] = alpha * a_scr[...] + jnp.dot(vtb, p, preferred_element_type=F32)
        m_scr[...] = m_new

    def block(kstart, nk, bias):
        rows = pl.ds(pl.multiple_of(kstart, tq), nk)
        kb = k_ref[rows, :]
        vtb = vt_scr[:, rows]
        s1 = _dot_nt(kb, q1)
        s2 = _dot_nt(kb, q2)
        if bias is not None:
            s1 = s1 + bias
            s2 = s2 + bias
        update(s1, vtb, m1_scr, a1_scr)
        update(s2, vtb, m2_scr, a2_scr)

    n_far = jnp.maximum(qi - 1, 0)
    n4 = n_far // 4

    def far_block(j, carry):
        block(j * (4 * tq), 4 * tq, None)
        return carry

    lax.fori_loop(0, n4, far_block, 0)
    has2 = (n_far // 2) % 2
    has1 = n_far % 2

    @pl.when(has2 == 1)
    def _():
        block(n4 * (4 * tq), 2 * tq, None)

    @pl.when(has1 == 1)
    def _():
        block((n4 * 4 + has2 * 2) * tq, tq, None)

    @pl.when(qi > 0)
    def _():
        block((qi - 1) * tq, 2 * tq, nb_ref[0])

    @pl.when(qi == 0)
    def _():
        block(0, tq, nb_ref[0, tq:, :])

    lf = lam_ref[...]
    lam = (jnp.exp(jnp.sum(lf[0:1] * lf[1:2], axis=-1, keepdims=True))
           - jnp.exp(jnp.sum(lf[2:3] * lf[3:4], axis=-1, keepdims=True)) + lam_init)
    a1 = a1_scr[...]
    a2 = a2_scr[...]
    o = a1[:dv] / a1[dv:dv + 1] - lam * (a2[:dv] / a2[dv:dv + 1])
    o = o * lax.rsqrt(jnp.mean(o * o, axis=0, keepdims=True) + EPS) * cn_ref[...] * (1.0 - lam_init)
    y_ref[...] = o.T.astype(y_ref.dtype)


def attn_branch(z, col0, near_bias, c_lambda, c_norm, batch, seq, n_heads, lam_init, tq):
    T = batch * seq
    nq = seq // tq
    W = n_heads * HEAD_DIM
    blk = (2 * _nbytes((tq, HEAD_DIM), BF16) + 2 * _nbytes((seq, HEAD_DIM), BF16)
           + _nbytes((tq, 2 * tq), F32))
    return pl.pallas_call(
        functools.partial(_attn_kernel, tq=tq, seq=seq, lam_init=lam_init),
        grid=(batch, n_heads, nq),
        in_specs=[pl.BlockSpec((tq, HEAD_DIM), lambda b, h, i: (b * nq + i, col0 + h)),
                  pl.BlockSpec((seq, HEAD_DIM), lambda b, h, i: (b, col0 + n_heads + h)),
                  pl.BlockSpec((seq, HEAD_DIM), lambda b, h, i: (b, col0 + 2 * n_heads + h)),
                  pl.BlockSpec((1, 2 * tq, tq), lambda b, h, i: (h, 0, 0)),
                  pl.BlockSpec(c_lambda.shape, lambda b, h, i: (0, 0)),
                  pl.BlockSpec((HEAD_DIM, 1), lambda b, h, i: (0, 0))],
        out_specs=pl.BlockSpec((tq, HEAD_DIM), lambda b, h, i: (b * nq + i, h)),
        out_shape=jax.ShapeDtypeStruct((T, W), BF16),
        scratch_shapes=[pltpu.VMEM((HEAD_DIM + ONES_ROWS, seq), BF16),
                        pltpu.VMEM((1, tq), F32), pltpu.VMEM((HEAD_DIM + ONES_ROWS, tq), F32),
                        pltpu.VMEM((1, tq), F32), pltpu.VMEM((HEAD_DIM + ONES_ROWS, tq), F32)],
        compiler_params=_params(("parallel", "parallel", "arbitrary"), blk,
                                _nbytes((HEAD_DIM + ONES_ROWS, seq), BF16) + 48 * _nbytes((tq, tq), F32)),
        name="diff_attention",
    )(z, z, z, near_bias, c_lambda.astype(F32), c_norm.reshape(HEAD_DIM, 1).astype(F32))


def _merge_kernel(ya_ref, yb_ref, yc_ref, wa_ref, wb_ref, wc_ref, g0_ref, g1_ref, g2_ref, o_ref,
                  wa_scr, wb_scr, wc_scr):
    @pl.when(pl.program_id(1) == 0)
    def _():
        _stage_weight(wa_ref, wa_scr)
        _stage_weight(wb_ref, wb_scr)
        _stage_weight(wc_ref, wc_scr)

    da = jnp.dot(ya_ref[...], wa_scr[...], preferred_element_type=F32)
    db = jnp.dot(yb_ref[...], wb_scr[...], preferred_element_type=F32)
    dc = jnp.dot(yc_ref[...], wc_scr[...], preferred_element_type=F32)
    merged = (_sigmoid(g0_ref[...].astype(F32)) * da + _sigmoid(g1_ref[...].astype(F32)) * db
              + _sigmoid(g2_ref[...].astype(F32)) * dc)
    o_ref[...] = merged.astype(o_ref.dtype)


def gated_merge(y_a, y_b, y_c, w_a, w_b, w_c, layer, z, gate_col0, d_model):
    T = z.shape[0]
    goff = gate_col0 * HEAD_DIM
    tn = _divisor(math.gcd(goff, d_model), (512, 256, 128))
    ka, kb, kc = y_a.shape[1], y_b.shape[1], y_c.shape[1]
    ks = ka + kb + kc

    def blk(tm):
        return _nbytes((tm, ks), BF16) + _nbytes((ks, tn), F32) + 4 * _nbytes((tm, tn), BF16)

    def scr(tm):
        return _nbytes((ks, tn), BF16) + 6 * _nbytes((tm, tn), F32)

    tm = _row_tile(T, blk, scr)

    def gspec(j):
        base = (goff + j * d_model) // tn
        return pl.BlockSpec((tm, tn), lambda n, i: (i, base + n))

    def wspec(k):
        return pl.BlockSpec((1, k, tn), lambda n, i: (layer, 0, n))

    return pl.pallas_call(
        _merge_kernel,
        grid=(d_model // tn, T // tm),
        in_specs=[pl.BlockSpec((tm, ka), lambda n, i: (i, 0)),
                  pl.BlockSpec((tm, kb), lambda n, i: (i, 0)),
                  pl.BlockSpec((tm, kc), lambda n, i: (i, 0)),
                  wspec(ka), wspec(kb), wspec(kc), gspec(0), gspec(1), gspec(2)],
        out_specs=pl.BlockSpec((tm, tn), lambda n, i: (i, n)),
        out_shape=jax.ShapeDtypeStruct((T, d_model), BF16),
        scratch_shapes=[pltpu.VMEM((ka, tn), BF16), pltpu.VMEM((kb, tn), BF16), pltpu.VMEM((kc, tn), BF16)],
        compiler_params=_params(("parallel", "arbitrary"), blk(tm), scr(tm)),
        name="gated_merge",
    )(y_a, y_b, y_c, w_a, w_b, w_c, z, z, z)


def _route_kernel(l_ref, idx_ref, w_ref, *, n_experts):
    logits = l_ref[...]
    lane = lax.broadcasted_iota(jnp.int32, logits.shape, 1)
    logits = jnp.where(lane < n_experts, logits, -jnp.inf)
    m1 = jnp.max(logits, axis=-1, keepdims=True)
    i1 = jnp.min(jnp.where(logits == m1, lane, LANES), axis=-1, keepdims=True)
    rest = jnp.where(lane == i1, -jnp.inf, logits)
    m2 = jnp.max(rest, axis=-1, keepdims=True)
    i2 = jnp.min(jnp.where(rest == m2, lane, LANES), axis=-1, keepdims=True)
    e = jnp.exp(m2 - m1)
    w1 = 1.0 / (1.0 + e)
    w2 = e / (1.0 + e)
    idx_ref[...] = jnp.where(lane == 0, i1, jnp.where(lane == 1, i2, 0))
    w_ref[...] = jnp.where(lane == 0, w1, jnp.where(lane == 1, w2, 0.0))


def route_top2(logits, n_experts):
    T = logits.shape[0]
    tm = _divisor(T, (1024, 512, 256, 128, 8))
    spec = pl.BlockSpec((tm, LANES), lambda i: (i, 0))
    idx, w = pl.pallas_call(
        functools.partial(_route_kernel, n_experts=n_experts),
        grid=(T // tm,),
        in_specs=[spec],
        out_specs=[spec, spec],
        out_shape=[jax.ShapeDtypeStruct((T, LANES), jnp.int32), jax.ShapeDtypeStruct((T, LANES), F32)],
        compiler_params=_params(("parallel",), 3 * _nbytes((tm, LANES), F32)),
        name="route_top2",
    )(logits)
    return idx[:, :TOP_K], w


def _moe_gather_kernel(src_ref, nv_ref, x_hbm, o_ref, buf, sem, *, tg):
    i = pl.program_id(0)
    n_valid = nv_ref[0]
    slot = i % 2

    def row_copy(tile, r, sl):
        tok = src_ref[tile * tg + r]
        return pltpu.make_async_copy(x_hbm.at[pl.ds(tok, 1), :], buf.at[sl, pl.ds(r, 1), :], sem.at[sl])

    def start_tile(tile, sl):
        def body(r, carry):
            row_copy(tile, r, sl).start()
            return carry
        lax.fori_loop(0, tg, body, 0)

    @pl.when(i == 0)
    def _():
        start_tile(0, 0)

    @pl.when(i + 1 < n_valid)
    def _():
        start_tile(i + 1, 1 - slot)

    @pl.when(i < n_valid)
    def _():
        def body(r, carry):
            row_copy(i, r, slot).wait()
            return carry
        lax.fori_loop(0, tg, body, 0)
        o_ref[...] = buf[slot].astype(o_ref.dtype)

    @pl.when(i >= n_valid)
    def _():
        o_ref[...] = jnp.zeros_like(o_ref)


def _weight_window(w_hbm, e, j, tn):
    return w_hbm.at[e, :, pl.ds(pl.multiple_of(j * tn, tn), tn)]


def _grouped_weights(s, sched, w_hbms, lands, scrs, sems, tn):
    ew_ref, jw_ref, fl_ref, ne_ref, nj_ref = sched

    def copies(e, j):
        return [pltpu.make_async_copy(_weight_window(w, e, j, tn), land.at[0], sems.at[k])
                for k, (w, land) in enumerate(zip(w_hbms, lands))]

    @pl.when(fl_ref[s] == 2)
    def _():
        @pl.when(s == 0)
        def _():
            for c in copies(ew_ref[0], jw_ref[0]):
                c.start()

        for c in copies(ew_ref[s], jw_ref[s]):
            c.wait()
        for land, scr in zip(lands, scrs):
            _stage_weight(land, scr)

        @pl.when(ne_ref[s] >= 0)
        def _():
            for c in copies(ne_ref[s], nj_ref[s]):
                c.start()


def _moe_up_kernel(t_ref, jo_ref, ew_ref, jw_ref, fl_ref, ne_ref, nj_ref, x_ref, wg_hbm, wu_hbm, o_ref,
                   wg_land, wu_land, wg_scr, wu_scr, sems, *, tf):
    s = pl.program_id(0)
    _grouped_weights(s, (ew_ref, jw_ref, fl_ref, ne_ref, nj_ref), (wg_hbm, wu_hbm), (wg_land, wu_land),
                     (wg_scr, wu_scr), sems, tf)

    @pl.when(fl_ref[s] > 0)
    def _():
        x = x_ref[...]
        g = jnp.dot(x, wg_scr[...], preferred_element_type=F32)
        u = jnp.dot(x, wu_scr[...], preferred_element_type=F32)
        o_ref[...] = (g * _sigmoid(g) * u).astype(o_ref.dtype)

    @pl.when(fl_ref[s] == 0)
    def _():
        o_ref[...] = jnp.zeros_like(o_ref)


def _moe_down_kernel(t_ref, jo_ref, ew_ref, jw_ref, fl_ref, ne_ref, nj_ref, a_ref, wd_hbm, o_ref,
                     wd_land, wd_scr, sems, *, tn):
    s = pl.program_id(0)
    _grouped_weights(s, (ew_ref, jw_ref, fl_ref, ne_ref, nj_ref), (wd_hbm,), (wd_land,), (wd_scr,), sems, tn)

    @pl.when(fl_ref[s] > 0)
    def _():
        o_ref[...] = jnp.dot(a_ref[...], wd_scr[...], preferred_element_type=F32)

    @pl.when(fl_ref[s] == 0)
    def _():
        o_ref[...] = jnp.zeros_like(o_ref)


def _moe_schedule(tiles_per_expert, n_tiles, n_cols):
    E = tiles_per_expert.shape[0]
    cnt = jnp.concatenate([tiles_per_expert, (n_tiles - jnp.sum(tiles_per_expert))[None]]).astype(jnp.int32)
    tile0 = jnp.cumsum(cnt) - cnt
    step_end = jnp.cumsum(cnt * n_cols)
    step0 = step_end - cnt * n_cols
    s = jnp.arange(n_tiles * n_cols, dtype=jnp.int32)
    g = jnp.sum((s[:, None] >= step_end[None, :]).astype(jnp.int32), axis=1)
    within = s - step0[g]
    c = jnp.maximum(cnt[g], 1)
    col = within // c
    tile = tile0[g] + within % c
    valid = g < E
    first = jnp.logical_and(valid, within % c == 0)
    n_valid = step_end[E - 1]
    last = jnp.maximum(n_valid - 1, 0)
    w_e = jnp.where(valid, g, g[last])
    w_j = jnp.where(valid, col, col[last])
    flag = jnp.where(first, 2, jnp.where(valid, 1, 0))
    nxt = s + c
    has_next = jnp.logical_and(first, nxt < n_valid)
    nxt = jnp.minimum(nxt, n_tiles * n_cols - 1)
    n_e = jnp.where(has_next, g[nxt], -1)
    n_j = jnp.where(has_next, col[nxt], 0)
    return [a.astype(jnp.int32) for a in (tile, col, w_e, w_j, flag, n_e, n_j)]


def _moe_combine_kernel(dest_ref, y_hbm, h_ref, w_ref, *rest, tm, final_norm):
    if final_norm:
        g_ref, o_ref, buf0, buf1, sem = rest
    else:
        o_ref, buf0, buf1, sem = rest
    i = pl.program_id(0)

    def row_copy(r, k, row):
        buf = buf0 if k == 0 else buf1
        return pltpu.make_async_copy(y_hbm.at[pl.ds(row, 1), :], buf.at[pl.ds(r, 1), :], sem.at[k])

    def start(r, carry):
        a = TOP_K * (i * tm + r)
        row_copy(r, 0, dest_ref[a]).start()
        row_copy(r, 1, dest_ref[a + 1]).start()
        return carry

    def wait(r, carry):
        row_copy(r, 0, 0).wait()
        row_copy(r, 1, 0).wait()
        return carry

    lax.fori_loop(0, tm, start, 0)
    lax.fori_loop(0, tm, wait, 0)
    w = w_ref[...]
    out = h_ref[...] + (w[:, 0:1] * buf0[...] + w[:, 1:2] * buf1[...])
    if final_norm:
        out = out * lax.rsqrt(jnp.mean(out * out, axis=-1, keepdims=True) + EPS) * g_ref[...]
    o_ref[...] = out


def moe_ffn(h, hn, logits, wg, wu, wd, layer, final_gain=None):
    T, D = hn.shape
    E, F = wg.shape[1], wg.shape[3]
    idx, wts = route_top2(logits, E)
    tm = _divisor(T, (256, 128))
    A = T * TOP_K
    n_tiles = A // tm + E

    e_flat = idx.reshape(-1)
    onehot = (e_flat[:, None] == jnp.arange(E, dtype=jnp.int32)[None, :]).astype(jnp.int32)
    csum = jnp.cumsum(onehot, axis=0)
    pos = jnp.take_along_axis(csum, e_flat[:, None], axis=1)[:, 0] - 1
    counts = csum[-1]
    padded = ((counts + tm - 1) // tm) * tm
    ends = jnp.cumsum(padded)
    starts = ends - padded
    dest = (starts[e_flat] + pos).astype(jnp.int32)
    src = jnp.zeros((n_tiles * tm,), jnp.int32).at[dest].set(jnp.arange(A, dtype=jnp.int32) // TOP_K)
    tiles_per_expert = padded // tm
    n_valid_tiles = jnp.sum(tiles_per_expert).astype(jnp.int32).reshape(1)

    x_sorted = pl.pallas_call(
        functools.partial(_moe_gather_kernel, tg=tm),
        grid_spec=pltpu.PrefetchScalarGridSpec(
            num_scalar_prefetch=2,
            grid=(n_tiles,),
            in_specs=[pl.BlockSpec(memory_space=pl.ANY)],
            out_specs=pl.BlockSpec((tm, D), lambda i, sr, nv: (i, 0)),
            scratch_shapes=[pltpu.VMEM((2, tm, D), F32), pltpu.SemaphoreType.DMA((2,))],
        ),
        out_shape=jax.ShapeDtypeStruct((n_tiles * tm, D), BF16),
        compiler_params=_params(("arbitrary",), _nbytes((tm, D), BF16), 3 * _nbytes((tm, D), F32)),
        name="moe_gather",
    )(src, n_valid_tiles, hn)

    def experts(w):
        return w.reshape((-1,) + w.shape[2:])

    tf = _divisor(F, (512, 256, 128))
    sched = _moe_schedule(tiles_per_expert, n_tiles, F // tf)
    sched[2] = sched[2] + layer * E
    sched[5] = jnp.where(sched[5] >= 0, sched[5] + layer * E, -1)
    hbm = pl.BlockSpec(memory_space=pl.ANY)
    act = pl.pallas_call(
        functools.partial(_moe_up_kernel, tf=tf),
        grid_spec=pltpu.PrefetchScalarGridSpec(
            num_scalar_prefetch=7,
            grid=(n_tiles * (F // tf),),
            in_specs=[pl.BlockSpec((tm, D), lambda s, t, jo, *_: (t[s], 0)), hbm, hbm],
            out_specs=pl.BlockSpec((tm, tf), lambda s, t, jo, *_: (t[s], jo[s])),
            scratch_shapes=[pltpu.VMEM((1, D, tf), F32), pltpu.VMEM((1, D, tf), F32),
                            pltpu.VMEM((D, tf), BF16), pltpu.VMEM((D, tf), BF16), pltpu.SemaphoreType.DMA((2,))],
        ),
        out_shape=jax.ShapeDtypeStruct((n_tiles * tm, F), BF16),
        compiler_params=_params(("arbitrary",), _nbytes((tm, D), BF16) + _nbytes((tm, tf), BF16),
                                2 * _nbytes((D, tf), F32) + 2 * _nbytes((D, tf), BF16) + 3 * _nbytes((tm, tf), F32)),
        name="moe_up",
    )(*sched, x_sorted, experts(wg), experts(wu))

    tn = _divisor(D, (1024, 512, 256, 128))
    sched = _moe_schedule(tiles_per_expert, n_tiles, D // tn)
    sched[2] = sched[2] + layer * E
    sched[5] = jnp.where(sched[5] >= 0, sched[5] + layer * E, -1)
    y_sorted = pl.pallas_call(
        functools.partial(_moe_down_kernel, tn=tn),
        grid_spec=pltpu.PrefetchScalarGridSpec(
            num_scalar_prefetch=7,
            grid=(n_tiles * (D // tn),),
            in_specs=[pl.BlockSpec((tm, F), lambda s, t, jo, *_: (t[s], 0)), hbm],
            out_specs=pl.BlockSpec((tm, tn), lambda s, t, jo, *_: (t[s], jo[s])),
            scratch_shapes=[pltpu.VMEM((1, F, tn), F32), pltpu.VMEM((F, tn), BF16), pltpu.SemaphoreType.DMA((1,))],
        ),
        out_shape=jax.ShapeDtypeStruct((n_tiles * tm, D), F32),
        compiler_params=_params(("arbitrary",), _nbytes((tm, F), BF16) + _nbytes((tm, tn), F32),
                                _nbytes((F, tn), F32) + _nbytes((F, tn), BF16) + _nbytes((tm, tn), F32)),
        name="moe_down",
    )(*sched, act, experts(wd))

    tc = _divisor(T, (256, 128))
    blk_c = 2 * _nbytes((tc, D), F32) + _nbytes((tc, LANES), F32)
    in_specs = [hbm, pl.BlockSpec((tc, D), lambda i, de: (i, 0)), pl.BlockSpec((tc, LANES), lambda i, de: (i, 0))]
    args = [dest, y_sorted, h, wts]
    if final_gain is not None:
        in_specs.append(pl.BlockSpec((1, D), lambda i, de: (0, 0)))
        args.append(final_gain.reshape(1, D).astype(F32))
    return pl.pallas_call(
        functools.partial(_moe_combine_kernel, tm=tc, final_norm=final_gain is not None),
        grid_spec=pltpu.PrefetchScalarGridSpec(
            num_scalar_prefetch=1,
            grid=(T // tc,),
            in_specs=in_specs,
            out_specs=pl.BlockSpec((tc, D), lambda i, de: (i, 0)),
            scratch_shapes=[pltpu.VMEM((tc, D), F32), pltpu.VMEM((tc, D), F32), pltpu.SemaphoreType.DMA((2,))],
        ),
        out_shape=jax.ShapeDtypeStruct((T, D), F32),
        compiler_params=_params(("arbitrary",), blk_c, 3 * _nbytes((tc, D), F32)),
        name="moe_combine",
    )(*args)


def _mixer(h, li, batch, seq, near_bias, tq, norm_mix, w_in, a_gate_bias, a_conv, a_norm, b_ln_g, b_ln_b, b_ws,
           b_bs, c_lambda, c_norm, w_br_a, w_br_b, w_br_c, w_out):
    d_model = h.shape[1]
    h_a = a_gate_bias.shape[1] // 2
    g_b = b_ws.shape[1]
    h_c = w_br_c.shape[1] // HEAD_DIM
    w_a = h_a * HEAD_DIM
    g0 = 4 * w_a
    n_rest = w_in.shape[2] - g0 - 2 * h_a
    n, zg = rmsnorm(h, norm_mix[li], proj=w_in[li, :, g0:g0 + 2 * h_a])
    z_a = matmul(n, w_in, li, BF16, n=g0)
    z_r = matmul(n, w_in, li, BF16, w_col=g0, n=n_rest, shift=2 * h_a)
    y_a = mlstm_branch(z_a, zg, a_gate_bias[li], a_conv[li], a_norm[li], batch, seq, h_a)
    y_b = sgu_branch(z_r, b_ln_g[li], b_ln_b[li], b_ws[li], b_bs[li], g_b)
    lam_init = 0.8 - 0.6 * math.exp(-0.3 * li)
    y_c = attn_branch(z_r, 2 * g_b, near_bias, c_lambda[li], c_norm[li], batch, seq, h_c, lam_init, tq)
    merged = gated_merge(y_a, y_b, y_c, w_br_a, w_br_b, w_br_c, li, z_r, 2 * g_b + 3 * h_c, d_model)
    return matmul_residual(merged, w_out, li, h)


def kernel(x, norm_mix, w_in, a_gate_bias, a_conv, a_norm, b_ln_g, b_ln_b, b_ws, b_bs, c_lambda, c_norm, rel_bias,
           w_br_a, w_br_b, w_br_c, w_out, norm_ffn, ffn_wg, ffn_wu, ffn_wd, router, moe_wg, moe_wu, moe_wd,
           final_norm):
    batch, seq, d_model = x.shape
    depth = w_in.shape[0]
    tq = _divisor(seq, (256, 128))
    near_bias = attn_bias_tiles(rel_bias, tq)
    h = x.reshape(batch * seq, d_model).astype(F32)
    for li in range(depth):
        h = _mixer(h, li, batch, seq, near_bias, tq, norm_mix, w_in, a_gate_bias, a_conv, a_norm, b_ln_g, b_ln_b,
                   b_ws, b_bs, c_lambda, c_norm, w_br_a, w_br_b, w_br_c, w_out)
        j = li // 2
        fused_final = False
        if li % 2 == 0:
            hn = rmsnorm(h, norm_ffn[li])
            act = swiglu_up(hn, ffn_wg, ffn_wu, j)
            h = matmul_residual(act, ffn_wd, j, h)
        else:
            hn, logits = rmsnorm(h, norm_ffn[li], proj=router[j], out_dtype=F32)
            fused_final = li == depth - 1
            h = moe_ffn(h, hn, logits, moe_wg, moe_wu, moe_wd, j, final_norm if fused_final else None)
    out = h if fused_final else rmsnorm(h, final_norm, out_dtype=F32)
    return out.reshape(batch, seq, d_model).astype(x.dtype)
```

```python
import functools
import math

import numpy as np
import jax
import jax.numpy as jnp
from jax import lax
from jax.experimental import pallas as pl
from jax.experimental.pallas import tpu as pltpu

F32 = jnp.float32
BF16 = jnp.bfloat16

EPS = 1e-6
HEAD_DIM = 128
CHUNK = 128
CONV_W = 4
N_BUCKETS = 32
MAX_DIST = 128
TOP_K = 2
LANES = 128
VMEM_BYTES_V7X = 64 * 1024 * 1024
VMEM_SLACK = 6 * 1024 * 1024
VMEM_BUDGET = VMEM_BYTES_V7X - 8 * 1024 * 1024
WEIGHT_STAGE_ROWS = 256


def _divisor(n, candidates):
    for c in candidates:
        if n % c == 0:
            return c
    raise ValueError(f"no tile size in {candidates} divides {n}")


def _vmem_need(block_bytes, scratch_bytes):
    return 2 * block_bytes + scratch_bytes + VMEM_SLACK


def _params(semantics, block_bytes, scratch_bytes=0):
    limit = int(min(max(_vmem_need(block_bytes, scratch_bytes), 16 * 1024 * 1024), VMEM_BYTES_V7X - 4 * 1024 * 1024))
    return pltpu.CompilerParams(dimension_semantics=semantics, vmem_limit_bytes=limit)


def _row_tile(T, block_bytes, scratch_bytes):
    for tm in (1024, 512, 256, 128):
        if T % tm == 0 and _vmem_need(block_bytes(tm), scratch_bytes(tm)) <= VMEM_BUDGET:
            return tm
    raise ValueError("no row tile fits VMEM")


def _nbytes(shape, dtype):
    return int(np.prod(shape)) * jnp.dtype(dtype).itemsize


def _sigmoid(x):
    return 1.0 / (1.0 + jnp.exp(-x))


def _split_bf16(x):
    hi = x.astype(BF16)
    lo = (x - hi.astype(F32)).astype(BF16)
    return hi, lo


def _dot_nt(a, b):
    return lax.dot_general(a, b, (((1,), (1,)), ((), ())), preferred_element_type=F32)


def _dot_tn(a, b):
    return lax.dot_general(a, b, (((0,), (0,)), ((), ())), preferred_element_type=F32)


def _norm_kernel(*refs, has_proj, out_dtype):
    if has_proj:
        x_ref, g_ref, whi_ref, wlo_ref, y_ref, p_ref = refs
    else:
        x_ref, g_ref, y_ref = refs
    x = x_ref[...]
    y = x * lax.rsqrt(jnp.mean(x * x, axis=-1, keepdims=True) + EPS) * g_ref[...]
    y_ref[...] = y.astype(out_dtype)
    if has_proj:
        y_hi, y_lo = _split_bf16(y)
        w_hi = whi_ref[...]
        p = jnp.dot(y_hi, w_hi, preferred_element_type=F32)
        p += jnp.dot(y_hi, wlo_ref[...], preferred_element_type=F32)
        p += jnp.dot(y_lo, w_hi, preferred_element_type=F32)
        p_ref[...] = p


def rmsnorm(x, g, proj=None, out_dtype=BF16):
    T, D = x.shape
    tm = _divisor(T, (256, 128, 8))
    g2 = g.reshape(1, D).astype(F32)
    in_specs = [pl.BlockSpec((tm, D), lambda i: (i, 0)), pl.BlockSpec((1, D), lambda i: (0, 0))]
    out_shape = [jax.ShapeDtypeStruct((T, D), out_dtype)]
    out_specs = [pl.BlockSpec((tm, D), lambda i: (i, 0))]
    args = [x, g2]
    blk = _nbytes((tm, D), F32) + _nbytes((tm, D), out_dtype)
    if proj is not None:
        n = proj.shape[1]
        assert n <= LANES
        w = jnp.pad(proj.astype(F32), ((0, 0), (0, LANES - n)))
        w_hi, w_lo = _split_bf16(w)
        in_specs += [pl.BlockSpec((D, LANES), lambda i: (0, 0))] * 2
        out_shape.append(jax.ShapeDtypeStruct((T, LANES), F32))
        out_specs.append(pl.BlockSpec((tm, LANES), lambda i: (i, 0)))
        args += [w_hi, w_lo]
        blk += 2 * _nbytes((D, LANES), BF16)
    outs = pl.pallas_call(
        functools.partial(_norm_kernel, has_proj=proj is not None, out_dtype=out_dtype),
        grid=(T // tm,),
        in_specs=in_specs,
        out_specs=out_specs,
        out_shape=out_shape,
        compiler_params=_params(("parallel",), blk, 4 * _nbytes((tm, D), F32)),
        name="rmsnorm",
    )(*args)
    return (outs[0], outs[1]) if proj is not None else outs[0]


def _stage_weight(w_ref, w_scr):
    R = w_scr.shape[0]
    rc = _divisor(R, (WEIGHT_STAGE_ROWS, LANES))

    def step(r, carry):
        rows = pl.ds(pl.multiple_of(r * rc, rc), rc)
        w_scr[rows, :] = w_ref[0, rows, :].astype(BF16)
        return carry

    lax.fori_loop(0, R // rc, step, 0)


def _matmul_kernel(*refs, layer, w_row, w_col, k, tn, transposed, has_res):
    a_ref, w_hbm = refs[:2]
    r_ref = refs[2] if has_res else None
    o_ref, w_land, w_scr, sem = refs[-4:]
    j = pl.program_id(0)
    nj = pl.num_programs(0)

    def window(jj):
        if transposed:
            src = w_hbm.at[layer, pl.ds(pl.multiple_of(w_col + jj * tn, 8), tn), pl.ds(w_row, k)]
        else:
            src = w_hbm.at[layer, pl.ds(w_row, k), pl.ds(pl.multiple_of(w_col + jj * tn, LANES), tn)]
        return pltpu.make_async_copy(src, w_land.at[0], sem)

    @pl.when(pl.program_id(1) == 0)
    def _():
        @pl.when(j == 0)
        def _():
            window(0).start()

        window(j).wait()
        _stage_weight(w_land, w_scr)

        @pl.when(j + 1 < nj)
        def _():
            window(j + 1).start()

    if transposed:
        d = _dot_nt(a_ref[...], w_scr[...])
    else:
        d = jnp.dot(a_ref[...], w_scr[...], preferred_element_type=F32)
    if has_res:
        d = d + r_ref[...]
    o_ref[...] = d.astype(o_ref.dtype)


def matmul(a, w, layer, out_dtype, *, a_col=0, k=None, w_row=0, w_col=0, n=None, transposed=False, res=None):
    T = a.shape[0]
    k = a.shape[1] if k is None else k
    n_total = w.shape[1] if transposed else w.shape[2]
    n = n_total - w_col if n is None else n
    tn = _divisor(n, (512, 256, 128))
    assert a_col % k == 0 and w_col % (8 if transposed else LANES) == 0
    wshape = (tn, k) if transposed else (k, tn)
    res_b = (lambda tm: _nbytes((tm, tn), F32)) if res is not None else (lambda tm: 0)

    def blk(tm):
        return _nbytes((tm, k), BF16) + _nbytes((tm, tn), out_dtype) + res_b(tm)

    def scr(tm):
        return _nbytes(wshape, F32) + _nbytes(wshape, BF16) + 2 * _nbytes((tm, tn), F32)

    tm = _row_tile(T, blk, scr)
    ab = a_col // k
    in_specs = [pl.BlockSpec((tm, k), lambda j, i: (i, ab)), pl.BlockSpec(memory_space=pl.ANY)]
    args = [a, w]
    if res is not None:
        in_specs.append(pl.BlockSpec((tm, tn), lambda j, i: (i, j)))
        args.append(res)
    return pl.pallas_call(
        functools.partial(_matmul_kernel, layer=layer, w_row=w_row, w_col=w_col, k=k, tn=tn, transposed=transposed,
                          has_res=res is not None),
        grid=(n // tn, T // tm),
        in_specs=in_specs,
        out_specs=pl.BlockSpec((tm, tn), lambda j, i: (i, j)),
        out_shape=jax.ShapeDtypeStruct((T, n), out_dtype),
        scratch_shapes=[pltpu.VMEM((1,) + wshape, F32), pltpu.VMEM(wshape, BF16), pltpu.SemaphoreType.DMA(())],
        compiler_params=_params(("arbitrary", "arbitrary"), blk(tm), scr(tm)),
        name="matmul",
    )(*args)


def matmul_residual(a, w, layer, res):
    K = a.shape[1]
    nk = next(n for n in range(1, K // LANES + 1) if K % (n * LANES) == 0 and K // n <= 6144)
    k = K // nk
    out = res
    for p in range(nk):
        out = matmul(a, w, layer, F32, a_col=p * k, k=k, w_row=p * k, res=out)
    return out


def _swiglu_up_kernel(x_ref, wg_ref, wu_ref, o_ref, wg_scr, wu_scr):
    @pl.when(pl.program_id(1) == 0)
    def _():
        _stage_weight(wg_ref, wg_scr)
        _stage_weight(wu_ref, wu_scr)

    x = x_ref[...]
    g = jnp.dot(x, wg_scr[...], preferred_element_type=F32)
    u = jnp.dot(x, wu_scr[...], preferred_element_type=F32)
    o_ref[...] = (g * _sigmoid(g) * u).astype(o_ref.dtype)


def swiglu_up(x, wg, wu, layer):
    T, K = x.shape
    F = wg.shape[2]
    tn = _divisor(F, (512, 256, 128))

    def blk(tm):
        return _nbytes((tm, K), BF16) + 2 * _nbytes((K, tn), F32) + _nbytes((tm, tn), BF16)

    def scr(tm):
        return 2 * _nbytes((K, tn), BF16) + 3 * _nbytes((tm, tn), F32)

    tm = _row_tile(T, blk, scr)
    wspec = pl.BlockSpec((1, K, tn), lambda j, i: (layer, 0, j))
    return pl.pallas_call(
        _swiglu_up_kernel,
        grid=(F // tn, T // tm),
        in_specs=[pl.BlockSpec((tm, K), lambda j, i: (i, 0)), wspec, wspec],
        out_specs=pl.BlockSpec((tm, tn), lambda j, i: (i, j)),
        out_shape=jax.ShapeDtypeStruct((T, F), BF16),
        scratch_shapes=[pltpu.VMEM((K, tn), BF16), pltpu.VMEM((K, tn), BF16)],
        compiler_params=_params(("parallel", "arbitrary"), blk(tm), scr(tm)),
        name="swiglu_up",
    )(x, wg, wu)


def _split_bf16_f32(x):
    hi = x.astype(BF16)
    return hi, x - hi.astype(F32)


def _mlstm_kernel(gb_ref, q_ref, k_ref, v_ref, og_ref, ig_ref, fg_ref, cwq_ref, cwk_ref, an_ref, y_ref,
                  c_scr, n_scr, m_scr, pq_scr, pk_scr, *, n_heads, n_chunks, group):
    hg = pl.program_id(1)
    L = CHUNK
    c_scr[...] = jnp.zeros_like(c_scr)
    n_scr[...] = jnp.zeros_like(n_scr)
    m_scr[...] = jnp.zeros_like(m_scr)
    pq_scr[...] = jnp.zeros_like(pq_scr)
    pk_scr[...] = jnp.zeros_like(pk_scr)
    ri = lax.broadcasted_iota(jnp.int32, (L, L), 0)
    ci = lax.broadcasted_iota(jnp.int32, (L, L), 1)
    lower = ci <= ri
    strict_lower01 = jnp.where(ri > ci, 1.0, 0.0).astype(BF16)

    def conv_silu(x, prev, w):
        y = x * w[CONV_W - 1:CONV_W, :]
        for s in range(1, CONV_W):
            shifted = jnp.where(ri < s, pltpu.roll(prev, s, 0), pltpu.roll(x, s, 0))
            y = y + shifted * w[CONV_W - 1 - s:CONV_W - s, :]
        return y * _sigmoid(y)

    def head_chunk(c, rows, g):
        cols = slice(g * HEAD_DIM, (g + 1) * HEAD_DIM)
        h = hg * group + g
        xq = q_ref[rows, cols].astype(F32)
        xk = k_ref[rows, cols].astype(F32)
        q = conv_silu(xq, pq_scr[g], cwq_ref[:, cols])
        k = conv_silu(xk, pk_scr[g], cwk_ref[:, cols]) * (HEAD_DIM ** -0.5)
        pq_scr[g] = xq
        pk_scr[g] = xk
        v = v_ref[rows, cols]
        q_bf = q.astype(BF16)
        k_bf = k.astype(BF16)

        i_row = ig_ref[0, g, pl.ds(c, 1), :] + gb_ref[h]
        f_row = fg_ref[0, g, pl.ds(c, 1), :] + gb_ref[n_heads + h]
        lf = jnp.minimum(f_row, 0.0) - jnp.log(1.0 + jnp.exp(-jnp.abs(f_row)))
        lf_low = jnp.where(lower, lf, 0.0)
        b_col = jnp.sum(lf_low, axis=1, keepdims=True)
        hi, rest = _split_bf16_f32(lf_low)
        mid, lo = _split_bf16_f32(rest)
        dmat = (jnp.dot(hi, strict_lower01, preferred_element_type=F32)
                + jnp.dot(mid, strict_lower01, preferred_element_type=F32)
                + jnp.dot(lo.astype(BF16), strict_lower01, preferred_element_type=F32))
        g_tot = b_col[L - 1:L, :]
        a_row = dmat[L - 1:L, :] + i_row
        a_col = jnp.sum(jnp.where(ci > ri, lf, 0.0) + jnp.where(ci == ri, i_row, 0.0),
                        axis=1, keepdims=True)

        m_prev = m_scr[g]
        c_prev = c_scr[g]
        n_prev = n_scr[g]

        log_d = jnp.where(lower, dmat + i_row, -jnp.inf)
        log_inter = b_col + m_prev
        m_t = jnp.maximum(log_inter, jnp.max(log_d, axis=1, keepdims=True))
        w = jnp.exp(log_d - m_t) * _dot_nt(q_bf, k_bf)
        e_inter = jnp.exp(log_inter - m_t)
        num = (e_inter * jnp.dot(q_bf, c_prev.astype(BF16), preferred_element_type=F32)
               + jnp.dot(w.astype(BF16), v, preferred_element_type=F32))
        den = e_inter * jnp.sum(q * n_prev, axis=1, keepdims=True) + jnp.sum(w, axis=1, keepdims=True)
        hcell = num / jnp.maximum(jnp.abs(den), jnp.exp(-m_t))

        gated = _sigmoid(og_ref[rows, cols].astype(F32)) * hcell
        y = gated * lax.rsqrt(jnp.mean(gated * gated, axis=-1, keepdims=True) + EPS) * an_ref[:, cols]
        y_ref[rows, cols] = y.astype(y_ref.dtype)

        m_new = jnp.maximum(g_tot + m_prev, jnp.max(a_row, axis=1, keepdims=True))
        w_col = jnp.exp(a_col - m_new)
        decay = jnp.exp(g_tot + m_prev - m_new)
        c_scr[g] = decay * c_prev + _dot_tn(k_bf, (v.astype(F32) * w_col).astype(BF16))
        n_scr[g] = decay * n_prev + jnp.sum(k * w_col, axis=0, keepdims=True)
        m_scr[g] = m_new

    def chunk(c, carry):
        rows = pl.ds(pl.multiple_of(c * L, L), L)
        for g in range(group):
            head_chunk(c, rows, g)
        return carry

    lax.fori_loop(0, n_chunks, chunk, 0)


def mlstm_branch(z, zg, a_gate_bias, a_conv, a_norm, batch, seq, n_heads):
    T = batch * seq
    nc = seq // CHUNK
    W = n_heads * HEAD_DIM
    group = _divisor(n_heads, (2, 1))
    ng = n_heads // group
    gw = group * HEAD_DIM

    def gate_rows(g):
        return jnp.transpose(g.reshape(batch, seq, n_heads), (0, 2, 1)).reshape(batch, n_heads, nc, CHUNK)

    ig = gate_rows(zg[:, :n_heads])
    fg = gate_rows(zg[:, n_heads:2 * n_heads])

    def zspec(part):
        return pl.BlockSpec((seq, gw), lambda b, h, gb: (b, part * ng + h))

    gspec = pl.BlockSpec((1, group, nc, CHUNK), lambda b, h, gb: (b, h, 0, 0))
    blk = 5 * _nbytes((seq, gw), BF16) + 2 * _nbytes((group, nc, CHUNK), F32)
    grid_spec = pltpu.PrefetchScalarGridSpec(
        num_scalar_prefetch=1,
        grid=(batch, ng),
        in_specs=[zspec(0), zspec(1), zspec(2), zspec(3), gspec, gspec,
                  pl.BlockSpec((CONV_W, gw), lambda b, h, gb: (0, h)),
                  pl.BlockSpec((CONV_W, gw), lambda b, h, gb: (0, ng + h)),
                  pl.BlockSpec((1, gw), lambda b, h, gb: (0, h))],
        out_specs=pl.BlockSpec((seq, gw), lambda b, h, gb: (b, h)),
        scratch_shapes=[pltpu.VMEM((group, HEAD_DIM, HEAD_DIM), F32), pltpu.VMEM((group, 1, HEAD_DIM), F32),
                        pltpu.VMEM((group, 1, 1), F32), pltpu.VMEM((group, CHUNK, HEAD_DIM), F32),
                        pltpu.VMEM((group, CHUNK, HEAD_DIM), F32)],
    )
    return pl.pallas_call(
        functools.partial(_mlstm_kernel, n_heads=n_heads, n_chunks=nc, group=group),
        grid_spec=grid_spec,
        out_shape=jax.ShapeDtypeStruct((T, W), BF16),
        compiler_params=_params(("parallel", "parallel"), blk, 64 * group * _nbytes((CHUNK, HEAD_DIM), F32)),
        name="mlstm",
    )(a_gate_bias.astype(F32), z, z, z, z, ig, fg, a_conv.astype(F32), a_conv.astype(F32),
      a_norm.reshape(1, W).astype(F32))


def _sgu_kernel(u_ref, v_ref, lng_ref, lnb_ref, ws_ref, bst_ref, y_ref, *, n_groups, chunks_per_block):
    L = CHUNK
    c0 = math.sqrt(2.0 / math.pi)

    def gelu(x):
        return x * (0.5 * (1.0 + jnp.tanh(c0 * (x + 0.044715 * (x * x * x)))))

    v = gelu(v_ref[...].astype(F32))
    mu = jnp.mean(v, axis=-1, keepdims=True)
    vc = v - mu
    vn = vc * lax.rsqrt(jnp.mean(vc * vc, axis=-1, keepdims=True) + EPS) * lng_ref[...] + lnb_ref[...]
    vn = vn.astype(BF16)
    ri = lax.broadcasted_iota(jnp.int32, (L, L), 0)
    ci = lax.broadcasted_iota(jnp.int32, (L, L), 1)
    bst = bst_ref[...]
    for g in range(n_groups):
        cs = slice(g * HEAD_DIM, (g + 1) * HEAD_DIM)
        w = jnp.where(ci <= ri, ws_ref[g], 0.0).astype(BF16)
        bias = bst[:, g:g + 1]
        for c in range(chunks_per_block):
            rs = slice(c * L, (c + 1) * L)
            s = jnp.dot(w, vn[rs, cs], preferred_element_type=F32) + bias
            y_ref[rs, cs] = (gelu(u_ref[rs, cs].astype(F32)) * s).astype(y_ref.dtype)


def sgu_branch(z, b_ln_g, b_ln_b, b_ws, b_bs, n_groups):
    T = z.shape[0]
    W = n_groups * HEAD_DIM
    cpb = _divisor(T // CHUNK, (4, 2, 1))
    R = cpb * CHUNK
    blk = 3 * _nbytes((R, W), BF16) + _nbytes((n_groups, CHUNK, CHUNK), F32)
    return pl.pallas_call(
        functools.partial(_sgu_kernel, n_groups=n_groups, chunks_per_block=cpb),
        grid=(T // R,),
        in_specs=[pl.BlockSpec((R, W), lambda i: (i, 0)),
                  pl.BlockSpec((R, W), lambda i: (i, 1)),
                  pl.BlockSpec((1, W), lambda i: (0, 0)),
                  pl.BlockSpec((1, W), lambda i: (0, 0)),
                  pl.BlockSpec((n_groups, CHUNK, CHUNK), lambda i: (0, 0, 0)),
                  pl.BlockSpec((CHUNK, n_groups), lambda i: (0, 0))],
        out_specs=pl.BlockSpec((R, W), lambda i: (i, 0)),
        out_shape=jax.ShapeDtypeStruct((T, W), BF16),
        compiler_params=_params(("parallel",), blk, 6 * _nbytes((R, W), F32)),
        name="spatial_gating",
    )(z, z, b_ln_g.reshape(1, W).astype(F32), b_ln_b.reshape(1, W).astype(F32), b_ws.astype(F32),
      jnp.transpose(b_bs).astype(F32))


def _t5_bucket_table(n):
    d = np.arange(n, dtype=np.int64)
    max_exact = N_BUCKETS // 2
    nf = np.maximum(d, 1).astype(np.float32)
    scaled = (np.log(nf / np.float32(max_exact)) / np.float32(math.log(MAX_DIST / max_exact))
              * np.float32(N_BUCKETS - max_exact))
    large = np.minimum(max_exact + scaled.astype(np.int32), N_BUCKETS - 1)
    return np.where(d < max_exact, d, large).astype(np.int32)


def _bias_kernel(rb_ref, bkt_ref, o_ref, *, n_heads):
    h = pl.program_id(0)
    bkt = bkt_ref[...]
    far = rb_ref[(N_BUCKETS - 1) * n_heads + h]
    out = jnp.full(bkt.shape, -jnp.inf, F32)
    for b in range(N_BUCKETS):
        out = jnp.where(bkt == b, rb_ref[b * n_heads + h] - far, out)
    o_ref[0] = out


def attn_bias_tiles(rel_bias, tq):
    n_heads = rel_bias.shape[1]
    assert tq >= MAX_DIST
    c = np.arange(2 * tq)[:, None]
    r = np.arange(tq)[None, :]
    dist = r - c + tq
    table = _t5_bucket_table(2 * tq)
    bkt = np.where(dist >= 0, table[np.maximum(dist, 0)], -1).astype(np.int32)
    grid_spec = pltpu.PrefetchScalarGridSpec(
        num_scalar_prefetch=1,
        grid=(n_heads,),
        in_specs=[pl.BlockSpec((2 * tq, tq), lambda h, rb: (0, 0))],
        out_specs=pl.BlockSpec((1, 2 * tq, tq), lambda h, rb: (h, 0, 0)),
    )
    return pl.pallas_call(
        functools.partial(_bias_kernel, n_heads=n_heads),
        grid_spec=grid_spec,
        out_shape=jax.ShapeDtypeStruct((n_heads, 2 * tq, tq), F32),
        compiler_params=_params(("arbitrary",), 2 * _nbytes((2 * tq, tq), F32)),
        name="attn_bias_tiles",
    )(rel_bias.astype(F32).reshape(-1), jnp.asarray(bkt))


ONES_ROWS = 16
ATTN_FAR_KEYS = 1024


def _attn_kernel(q_ref, k_ref, v_ref, nb_ref, lam_ref, cn_ref, y_ref,
                 vt_scr, m1_scr, a1_scr, m2_scr, a2_scr, *, tq, seq, lam_init):
    qi = pl.program_id(2)
    dk = HEAD_DIM // 2
    dv = HEAD_DIM
    scale = dk ** -0.5

    @pl.when(qi == 0)
    def _():
        for c in range(seq // tq):
            cs = slice(c * tq, (c + 1) * tq)
            vt_scr[:dv, cs] = v_ref[cs, :].astype(F32).T.astype(BF16)
        vt_scr[dv:, :] = jnp.ones((ONES_ROWS, seq), BF16)

    q = (q_ref[...].astype(F32) * scale).astype(BF16)
    lane = lax.broadcasted_iota(jnp.int32, q.shape, 1)
    zero = jnp.zeros_like(q)
    q1 = jnp.where(lane < dk, q, zero)
    q2 = jnp.where(lane >= dk, q, zero)
    for m_scr, a_scr in ((m1_scr, a1_scr), (m2_scr, a2_scr)):
        m_scr[...] = jnp.full(m_scr.shape, -jnp.inf, F32)
        a_scr[...] = jnp.zeros_like(a_scr)

    def update(s, vtb, m_scr, a_scr):
        m_prev = m_scr[...]
        m_new = jnp.maximum(m_prev, jnp.max(s, axis=0, keepdims=True))
        alpha = jnp.exp(m_prev - m_new)
        p = jnp.exp(s - m_new).astype(BF16)
        a_scr[...] = alpha * a_scr[...] + jnp.dot(vtb, p, preferred_element_type=F32)
        m_scr[...] = m_new

    def block(kstart, nk, bias):
        rows = pl.ds(pl.multiple_of(kstart, tq), nk)
        kb = k_ref[rows, :]
        vtb = vt_scr[:, rows]
        s1 = _dot_nt(kb, q1)
        s2 = _dot_nt(kb, q2)
        if bias is not None:
            s1 = s1 + bias
            s2 = s2 + bias
        update(s1, vtb, m1_scr, a1_scr)
        update(s2, vtb, m2_scr, a2_scr)

    big = max(ATTN_FAR_KEYS // tq, 1)
    n_far = jnp.maximum(qi - 1, 0)
    n_big = n_far // big

    def far_block(j, carry):
        block(j * (big * tq), big * tq, None)
        return carry

    lax.fori_loop(0, n_big, far_block, 0)
    rem = n_far - n_big * big
    unit = big // 2
    while unit >= 1:
        done = n_big * big + (rem // (2 * unit)) * (2 * unit)

        @pl.when((rem // unit) % 2 == 1)
        def _(done=done, unit=unit):
            block(done * tq, unit * tq, None)

        unit //= 2

    @pl.when(qi > 0)
    def _():
        block((qi - 1) * tq, 2 * tq, nb_ref[0])

    @pl.when(qi == 0)
    def _():
        block(0, tq, nb_ref[0, tq:, :])

    lf = lam_ref[...]
    lam = (jnp.exp(jnp.sum(lf[0:1] * lf[1:2], axis=-1, keepdims=True))
           - jnp.exp(jnp.sum(lf[2:3] * lf[3:4], axis=-1, keepdims=True)) + lam_init)
    a1 = a1_scr[...]
    a2 = a2_scr[...]
    o = a1[:dv] / a1[dv:dv + 1] - lam * (a2[:dv] / a2[dv:dv + 1])
    o = o * lax.rsqrt(jnp.mean(o * o, axis=0, keepdims=True) + EPS) * cn_ref[...] * (1.0 - lam_init)
    y_ref[...] = o.T.astype(y_ref.dtype)


def attn_branch(z, col0, near_bias, c_lambda, c_norm, batch, seq, n_heads, lam_init, tq):
    T = batch * seq
    nq = seq // tq
    W = n_heads * HEAD_DIM
    blk = (2 * _nbytes((tq, HEAD_DIM), BF16) + 2 * _nbytes((seq, HEAD_DIM), BF16)
           + _nbytes((tq, 2 * tq), F32))
    return pl.pallas_call(
        functools.partial(_attn_kernel, tq=tq, seq=seq, lam_init=lam_init),
        grid=(batch, n_heads, nq),
        in_specs=[pl.BlockSpec((tq, HEAD_DIM), lambda b, h, i: (b * nq + i, col0 + h)),
                  pl.BlockSpec((seq, HEAD_DIM), lambda b, h, i: (b, col0 + n_heads + h)),
                  pl.BlockSpec((seq, HEAD_DIM), lambda b, h, i: (b, col0 + 2 * n_heads + h)),
                  pl.BlockSpec((1, 2 * tq, tq), lambda b, h, i: (h, 0, 0)),
                  pl.BlockSpec(c_lambda.shape, lambda b, h, i: (0, 0)),
                  pl.BlockSpec((HEAD_DIM, 1), lambda b, h, i: (0, 0))],
        out_specs=pl.BlockSpec((tq, HEAD_DIM), lambda b, h, i: (b * nq + i, h)),
        out_shape=jax.ShapeDtypeStruct((T, W), BF16),
        scratch_shapes=[pltpu.VMEM((HEAD_DIM + ONES_ROWS, seq), BF16),
                        pltpu.VMEM((1, tq), F32), pltpu.VMEM((HEAD_DIM + ONES_ROWS, tq), F32),
                        pltpu.VMEM((1, tq), F32), pltpu.VMEM((HEAD_DIM + ONES_ROWS, tq), F32)],
        compiler_params=_params(("parallel", "parallel", "arbitrary"), blk,
                                _nbytes((HEAD_DIM + ONES_ROWS, seq), BF16)
                                + 8 * _nbytes((max(ATTN_FAR_KEYS, 2 * tq), tq), F32)),
        name="diff_attention",
    )(z, z, z, near_bias, c_lambda.astype(F32), c_norm.reshape(HEAD_DIM, 1).astype(F32))


def _merge_kernel(ya_ref, yb_ref, yc_ref, wa_ref, wb_ref, wc_ref, g0_ref, g1_ref, g2_ref, o_ref,
                  wa_scr, wb_scr, wc_scr):
    @pl.when(pl.program_id(1) == 0)
    def _():
        _stage_weight(wa_ref, wa_scr)
        _stage_weight(wb_ref, wb_scr)
        _stage_weight(wc_ref, wc_scr)

    da = jnp.dot(ya_ref[...], wa_scr[...], preferred_element_type=F32)
    db = jnp.dot(yb_ref[...], wb_scr[...], preferred_element_type=F32)
    dc = jnp.dot(yc_ref[...], wc_scr[...], preferred_element_type=F32)
    merged = (_sigmoid(g0_ref[...].astype(F32)) * da + _sigmoid(g1_ref[...].astype(F32)) * db
              + _sigmoid(g2_ref[...].astype(F32)) * dc)
    o_ref[...] = merged.astype(o_ref.dtype)


def gated_merge(y_a, y_b, y_c, w_a, w_b, w_c, layer, z, gate_col0, d_model):
    T = z.shape[0]
    goff = gate_col0 * HEAD_DIM
    tn = _divisor(math.gcd(goff, d_model), (512, 256, 128))
    ka, kb, kc = y_a.shape[1], y_b.shape[1], y_c.shape[1]
    ks = ka + kb + kc

    def blk(tm):
        return _nbytes((tm, ks), BF16) + _nbytes((ks, tn), F32) + 4 * _nbytes((tm, tn), BF16)

    def scr(tm):
        return _nbytes((ks, tn), BF16) + 6 * _nbytes((tm, tn), F32)

    tm = _row_tile(T, blk, scr)

    def gspec(j):
        base = (goff + j * d_model) // tn
        return pl.BlockSpec((tm, tn), lambda n, i: (i, base + n))

    def wspec(k):
        return pl.BlockSpec((1, k, tn), lambda n, i: (layer, 0, n))

    return pl.pallas_call(
        _merge_kernel,
        grid=(d_model // tn, T // tm),
        in_specs=[pl.BlockSpec((tm, ka), lambda n, i: (i, 0)),
                  pl.BlockSpec((tm, kb), lambda n, i: (i, 0)),
                  pl.BlockSpec((tm, kc), lambda n, i: (i, 0)),
                  wspec(ka), wspec(kb), wspec(kc), gspec(0), gspec(1), gspec(2)],
        out_specs=pl.BlockSpec((tm, tn), lambda n, i: (i, n)),
        out_shape=jax.ShapeDtypeStruct((T, d_model), BF16),
        scratch_shapes=[pltpu.VMEM((ka, tn), BF16), pltpu.VMEM((kb, tn), BF16), pltpu.VMEM((kc, tn), BF16)],
        compiler_params=_params(("parallel", "arbitrary"), blk(tm), scr(tm)),
        name="gated_merge",
    )(y_a, y_b, y_c, w_a, w_b, w_c, z, z, z)


def _route_kernel(l_ref, idx_ref, w_ref, *, n_experts):
    logits = l_ref[...]
    lane = lax.broadcasted_iota(jnp.int32, logits.shape, 1)
    logits = jnp.where(lane < n_experts, logits, -jnp.inf)
    m1 = jnp.max(logits, axis=-1, keepdims=True)
    i1 = jnp.min(jnp.where(logits == m1, lane, LANES), axis=-1, keepdims=True)
    rest = jnp.where(lane == i1, -jnp.inf, logits)
    m2 = jnp.max(rest, axis=-1, keepdims=True)
    i2 = jnp.min(jnp.where(rest == m2, lane, LANES), axis=-1, keepdims=True)
    e = jnp.exp(m2 - m1)
    w1 = 1.0 / (1.0 + e)
    w2 = e / (1.0 + e)
    idx_ref[...] = jnp.where(lane == 0, i1, jnp.where(lane == 1, i2, 0))
    w_ref[...] = jnp.where(lane == 0, w1, jnp.where(lane == 1, w2, 0.0))


def route_top2(logits, n_experts):
    T = logits.shape[0]
    tm = _divisor(T, (1024, 512, 256, 128, 8))
    spec = pl.BlockSpec((tm, LANES), lambda i: (i, 0))
    idx, w = pl.pallas_call(
        functools.partial(_route_kernel, n_experts=n_experts),
        grid=(T // tm,),
        in_specs=[spec],
        out_specs=[spec, spec],
        out_shape=[jax.ShapeDtypeStruct((T, LANES), jnp.int32), jax.ShapeDtypeStruct((T, LANES), F32)],
        compiler_params=_params(("parallel",), 3 * _nbytes((tm, LANES), F32)),
        name="route_top2",
    )(logits)
    return idx[:, :TOP_K], w


def _moe_gather_kernel(src_ref, nv_ref, x_hbm, o_ref, buf, sem, *, tg):
    i = pl.program_id(0)
    n_valid = nv_ref[0]
    slot = i % 2

    def row_copy(tile, r, sl):
        tok = src_ref[tile * tg + r]
        return pltpu.make_async_copy(x_hbm.at[pl.ds(tok, 1), :], buf.at[sl, pl.ds(r, 1), :], sem.at[sl])

    def start_tile(tile, sl):
        def body(r, carry):
            row_copy(tile, r, sl).start()
            return carry
        lax.fori_loop(0, tg, body, 0)

    @pl.when(i == 0)
    def _():
        start_tile(0, 0)

    @pl.when(i + 1 < n_valid)
    def _():
        start_tile(i + 1, 1 - slot)

    @pl.when(i < n_valid)
    def _():
        def body(r, carry):
            row_copy(i, r, slot).wait()
            return carry
        lax.fori_loop(0, tg, body, 0)
        o_ref[...] = buf[slot].astype(o_ref.dtype)

    @pl.when(i >= n_valid)
    def _():
        o_ref[...] = jnp.zeros_like(o_ref)


def _weight_window(w_hbm, e, j, tn):
    return w_hbm.at[e, :, pl.ds(pl.multiple_of(j * tn, tn), tn)]


def _grouped_weights(s, sched, w_hbms, lands, scrs, sems, tn):
    ew_ref, jw_ref, fl_ref, ne_ref, nj_ref = sched

    def copies(e, j):
        return [pltpu.make_async_copy(_weight_window(w, e, j, tn), land.at[0], sems.at[k])
                for k, (w, land) in enumerate(zip(w_hbms, lands))]

    @pl.when(fl_ref[s] == 2)
    def _():
        @pl.when(s == 0)
        def _():
            for c in copies(ew_ref[0], jw_ref[0]):
                c.start()

        for c in copies(ew_ref[s], jw_ref[s]):
            c.wait()
        for land, scr in zip(lands, scrs):
            _stage_weight(land, scr)

        @pl.when(ne_ref[s] >= 0)
        def _():
            for c in copies(ne_ref[s], nj_ref[s]):
                c.start()


def _moe_up_kernel(t_ref, jo_ref, ew_ref, jw_ref, fl_ref, ne_ref, nj_ref, x_ref, wg_hbm, wu_hbm, o_ref,
                   wg_land, wu_land, wg_scr, wu_scr, sems, *, tf):
    s = pl.program_id(0)
    _grouped_weights(s, (ew_ref, jw_ref, fl_ref, ne_ref, nj_ref), (wg_hbm, wu_hbm), (wg_land, wu_land),
                     (wg_scr, wu_scr), sems, tf)

    @pl.when(fl_ref[s] > 0)
    def _():
        x = x_ref[...]
        g = jnp.dot(x, wg_scr[...], preferred_element_type=F32)
        u = jnp.dot(x, wu_scr[...], preferred_element_type=F32)
        o_ref[...] = (g * _sigmoid(g) * u).astype(o_ref.dtype)

    @pl.when(fl_ref[s] == 0)
    def _():
        o_ref[...] = jnp.zeros_like(o_ref)


def _moe_down_kernel(t_ref, jo_ref, ew_ref, jw_ref, fl_ref, ne_ref, nj_ref, a_ref, wd_hbm, o_ref,
                     wd_land, wd_scr, sems, *, tn):
    s = pl.program_id(0)
    _grouped_weights(s, (ew_ref, jw_ref, fl_ref, ne_ref, nj_ref), (wd_hbm,), (wd_land,), (wd_scr,), sems, tn)

    @pl.when(fl_ref[s] > 0)
    def _():
        o_ref[...] = jnp.dot(a_ref[...], wd_scr[...], preferred_element_type=F32)

    @pl.when(fl_ref[s] == 0)
    def _():
        o_ref[...] = jnp.zeros_like(o_ref)


def _moe_schedule(tiles_per_expert, n_tiles, n_cols):
    E = tiles_per_expert.shape[0]
    cnt = jnp.concatenate([tiles_per_expert, (n_tiles - jnp.sum(tiles_per_expert))[None]]).astype(jnp.int32)
    tile0 = jnp.cumsum(cnt) - cnt
    step_end = jnp.cumsum(cnt * n_cols)
    step0 = step_end - cnt * n_cols
    s = jnp.arange(n_tiles * n_cols, dtype=jnp.int32)
    g = jnp.sum((s[:, None] >= step_end[None, :]).astype(jnp.int32), axis=1)
    within = s - step0[g]
    c = jnp.maximum(cnt[g], 1)
    col = within // c
    tile = tile0[g] + within % c
    valid = g < E
    first = jnp.logical_and(valid, within % c == 0)
    n_valid = step_end[E - 1]
    last = jnp.maximum(n_valid - 1, 0)
    w_e = jnp.where(valid, g, g[last])
    w_j = jnp.where(valid, col, col[last])
    flag = jnp.where(first, 2, jnp.where(valid, 1, 0))
    nxt = s + c
    has_next = jnp.logical_and(first, nxt < n_valid)
    nxt = jnp.minimum(nxt, n_tiles * n_cols - 1)
    n_e = jnp.where(has_next, g[nxt], -1)
    n_j = jnp.where(has_next, col[nxt], 0)
    return [a.astype(jnp.int32) for a in (tile, col, w_e, w_j, flag, n_e, n_j)]


def _moe_combine_kernel(dest_ref, y_hbm, h_ref, w_ref, *rest, tm, final_norm):
    if final_norm:
        g_ref, o_ref, buf0, buf1, sem = rest
    else:
        o_ref, buf0, buf1, sem = rest
    i = pl.program_id(0)

    def row_copy(r, k, row):
        buf = buf0 if k == 0 else buf1
        return pltpu.make_async_copy(y_hbm.at[pl.ds(row, 1), :], buf.at[pl.ds(r, 1), :], sem.at[k])

    def start(r, carry):
        a = TOP_K * (i * tm + r)
        row_copy(r, 0, dest_ref[a]).start()
        row_copy(r, 1, dest_ref[a + 1]).start()
        return carry

    def wait(r, carry):
        row_copy(r, 0, 0).wait()
        row_copy(r, 1, 0).wait()
        return carry

    lax.fori_loop(0, tm, start, 0)
    lax.fori_loop(0, tm, wait, 0)
    w = w_ref[...]
    out = h_ref[...] + (w[:, 0:1] * buf0[...] + w[:, 1:2] * buf1[...])
    if final_norm:
        out = out * lax.rsqrt(jnp.mean(out * out, axis=-1, keepdims=True) + EPS) * g_ref[...]
    o_ref[...] = out


def moe_ffn(h, hn, logits, wg, wu, wd, layer, final_gain=None):
    T, D = hn.shape
    E, F = wg.shape[1], wg.shape[3]
    idx, wts = route_top2(logits, E)
    tm = _divisor(T, (256, 128))
    A = T * TOP_K
    n_tiles = A // tm + E

    e_flat = idx.reshape(-1)
    onehot = (e_flat[:, None] == jnp.arange(E, dtype=jnp.int32)[None, :]).astype(jnp.int32)
    csum = jnp.cumsum(onehot, axis=0)
    pos = jnp.take_along_axis(csum, e_flat[:, None], axis=1)[:, 0] - 1
    counts = csum[-1]
    padded = ((counts + tm - 1) // tm) * tm
    ends = jnp.cumsum(padded)
    starts = ends - padded
    dest = (starts[e_flat] + pos).astype(jnp.int32)
    src = jnp.zeros((n_tiles * tm,), jnp.int32).at[dest].set(jnp.arange(A, dtype=jnp.int32) // TOP_K)
    tiles_per_expert = padded // tm
    n_valid_tiles = jnp.sum(tiles_per_expert).astype(jnp.int32).reshape(1)

    x_sorted = pl.pallas_call(
        functools.partial(_moe_gather_kernel, tg=tm),
        grid_spec=pltpu.PrefetchScalarGridSpec(
            num_scalar_prefetch=2,
            grid=(n_tiles,),
            in_specs=[pl.BlockSpec(memory_space=pl.ANY)],
            out_specs=pl.BlockSpec((tm, D), lambda i, sr, nv: (i, 0)),
            scratch_shapes=[pltpu.VMEM((2, tm, D), F32), pltpu.SemaphoreType.DMA((2,))],
        ),
        out_shape=jax.ShapeDtypeStruct((n_tiles * tm, D), BF16),
        compiler_params=_params(("arbitrary",), _nbytes((tm, D), BF16), 3 * _nbytes((tm, D), F32)),
        name="moe_gather",
    )(src, n_valid_tiles, hn)

    def experts(w):
        return w.reshape((-1,) + w.shape[2:])

    tf = _divisor(F, (512, 256, 128))
    sched = _moe_schedule(tiles_per_expert, n_tiles, F // tf)
    sched[2] = sched[2] + layer * E
    sched[5] = jnp.where(sched[5] >= 0, sched[5] + layer * E, -1)
    hbm = pl.BlockSpec(memory_space=pl.ANY)
    act = pl.pallas_call(
        functools.partial(_moe_up_kernel, tf=tf),
        grid_spec=pltpu.PrefetchScalarGridSpec(
            num_scalar_prefetch=7,
            grid=(n_tiles * (F // tf),),
            in_specs=[pl.BlockSpec((tm, D), lambda s, t, jo, *_: (t[s], 0)), hbm, hbm],
            out_specs=pl.BlockSpec((tm, tf), lambda s, t, jo, *_: (t[s], jo[s])),
            scratch_shapes=[pltpu.VMEM((1, D, tf), F32), pltpu.VMEM((1, D, tf), F32),
                            pltpu.VMEM((D, tf), BF16), pltpu.VMEM((D, tf), BF16), pltpu.SemaphoreType.DMA((2,))],
        ),
        out_shape=jax.ShapeDtypeStruct((n_tiles * tm, F), BF16),
        compiler_params=_params(("arbitrary",), _nbytes((tm, D), BF16) + _nbytes((tm, tf), BF16),
                                2 * _nbytes((D, tf), F32) + 2 * _nbytes((D, tf), BF16) + 3 * _nbytes((tm, tf), F32)),
        name="moe_up",
    )(*sched, x_sorted, experts(wg), experts(wu))

    tn = _divisor(D, (1024, 512, 256, 128))
    sched = _moe_schedule(tiles_per_expert, n_tiles, D // tn)
    sched[2] = sched[2] + layer * E
    sched[5] = jnp.where(sched[5] >= 0, sched[5] + layer * E, -1)
    y_sorted = pl.pallas_call(
        functools.partial(_moe_down_kernel, tn=tn),
        grid_spec=pltpu.PrefetchScalarGridSpec(
            num_scalar_prefetch=7,
            grid=(n_tiles * (D // tn),),
            in_specs=[pl.BlockSpec((tm, F), lambda s, t, jo, *_: (t[s], 0)), hbm],
            out_specs=pl.BlockSpec((tm, tn), lambda s, t, jo, *_: (t[s], jo[s])),
            scratch_shapes=[pltpu.VMEM((1, F, tn), F32), pltpu.VMEM((F, tn), BF16), pltpu.SemaphoreType.DMA((1,))],
        ),
        out_shape=jax.ShapeDtypeStruct((n_tiles * tm, D), F32),
        compiler_params=_params(("arbitrary",), _nbytes((tm, F), BF16) + _nbytes((tm, tn), F32),
                                _nbytes((F, tn), F32) + _nbytes((F, tn), BF16) + _nbytes((tm, tn), F32)),
        name="moe_down",
    )(*sched, act, experts(wd))

    tc = _divisor(T, (256, 128))
    blk_c = 2 * _nbytes((tc, D), F32) + _nbytes((tc, LANES), F32)
    in_specs = [hbm, pl.BlockSpec((tc, D), lambda i, de: (i, 0)), pl.BlockSpec((tc, LANES), lambda i, de: (i, 0))]
    args = [dest, y_sorted, h, wts]
    if final_gain is not None:
        in_specs.append(pl.BlockSpec((1, D), lambda i, de: (0, 0)))
        args.append(final_gain.reshape(1, D).astype(F32))
    return pl.pallas_call(
        functools.partial(_moe_combine_kernel, tm=tc, final_norm=final_gain is not None),
        grid_spec=pltpu.PrefetchScalarGridSpec(
            num_scalar_prefetch=1,
            grid=(T // tc,),
            in_specs=in_specs,
            out_specs=pl.BlockSpec((tc, D), lambda i, de: (i, 0)),
            scratch_shapes=[pltpu.VMEM((tc, D), F32), pltpu.VMEM((tc, D), F32), pltpu.SemaphoreType.DMA((2,))],
        ),
        out_shape=jax.ShapeDtypeStruct((T, D), F32),
        compiler_params=_params(("arbitrary",), blk_c, 3 * _nbytes((tc, D), F32)),
        name="moe_combine",
    )(*args)


def _mixer(h, li, batch, seq, near_bias, tq, norm_mix, w_in, a_gate_bias, a_conv, a_norm, b_ln_g, b_ln_b, b_ws,
           b_bs, c_lambda, c_norm, w_br_a, w_br_b, w_br_c, w_out):
    d_model = h.shape[1]
    h_a = a_gate_bias.shape[1] // 2
    g_b = b_ws.shape[1]
    h_c = w_br_c.shape[1] // HEAD_DIM
    w_a = h_a * HEAD_DIM
    g0 = 4 * w_a
    g1 = g0 + 2 * h_a
    n_rest = w_in.shape[2] - g1
    w_in_t = jnp.swapaxes(w_in, 1, 2)
    n, zg = rmsnorm(h, norm_mix[li], proj=jnp.transpose(w_in_t[li, g0:g1, :]))
    z_a = matmul(n, w_in_t, li, BF16, n=g0, transposed=True)
    z_r = matmul(n, w_in_t, li, BF16, w_col=g1, n=n_rest, transposed=True)
    y_a = mlstm_branch(z_a, zg, a_gate_bias[li], a_conv[li], a_norm[li], batch, seq, h_a)
    y_b = sgu_branch(z_r, b_ln_g[li], b_ln_b[li], b_ws[li], b_bs[li], g_b)
    lam_init = 0.8 - 0.6 * math.exp(-0.3 * li)
    y_c = attn_branch(z_r, 2 * g_b, near_bias, c_lambda[li], c_norm[li], batch, seq, h_c, lam_init, tq)
    merged = gated_merge(y_a, y_b, y_c, w_br_a, w_br_b, w_br_c, li, z_r, 2 * g_b + 3 * h_c, d_model)
    return matmul_residual(merged, w_out, li, h)


def kernel(x, norm_mix, w_in, a_gate_bias, a_conv, a_norm, b_ln_g, b_ln_b, b_ws, b_bs, c_lambda, c_norm, rel_bias,
           w_br_a, w_br_b, w_br_c, w_out, norm_ffn, ffn_wg, ffn_wu, ffn_wd, router, moe_wg, moe_wu, moe_wd,
           final_norm):
    batch, seq, d_model = x.shape
    depth = w_in.shape[0]
    tq = _divisor(seq, (512, 256, 128))
    near_bias = attn_bias_tiles(rel_bias, tq)
    h = x.reshape(batch * seq, d_model).astype(F32)
    for li in range(depth):
        h = _mixer(h, li, batch, seq, near_bias, tq, norm_mix, w_in, a_gate_bias, a_conv, a_norm, b_ln_g, b_ln_b,
                   b_ws, b_bs, c_lambda, c_norm, w_br_a, w_br_b, w_br_c, w_out)
        j = li // 2
        fused_final = False
        if li % 2 == 0:
            hn = rmsnorm(h, norm_ffn[li])
            act = swiglu_up(hn, ffn_wg, ffn_wu, j)
            h = matmul_residual(act, ffn_wd, j, h)
        else:
            hn, logits = rmsnorm(h, norm_ffn[li], proj=router[j], out_dtype=F32)
            fused_final = li == depth - 1
            h = moe_ffn(h, hn, logits, moe_wg, moe_wu, moe_wd, j, final_norm if fused_final else None)
    out = h if fused_final else rmsnorm(h, final_norm, out_dtype=F32)
    return out.reshape(batch, seq, d_model).astype(x.dtype)
```

```python
import functools
import itertools
import math

import numpy as np
import jax
import jax.numpy as jnp
from jax import lax
from jax.experimental import pallas as pl
from jax.experimental.pallas import tpu as pltpu

F32 = jnp.float32
BF16 = jnp.bfloat16

EPS = 1e-6
HEAD_DIM = 128
CHUNK = 128
CONV_W = 4
N_BUCKETS = 32
MAX_DIST = 128
TOP_K = 2
LANES = 128
VMEM_BYTES_V7X = 64 * 1024 * 1024
VMEM_SLACK = 6 * 1024 * 1024
VMEM_BUDGET = VMEM_BYTES_V7X - 8 * 1024 * 1024
WEIGHT_STAGE_ROWS = 256


def _divisor(n, candidates):
    for c in candidates:
        if n % c == 0:
            return c
    raise ValueError(f"no tile size in {candidates} divides {n}")


def _vmem_need(block_bytes, scratch_bytes):
    return 2 * block_bytes + scratch_bytes + VMEM_SLACK


def _params(semantics, block_bytes, scratch_bytes=0):
    limit = int(min(max(_vmem_need(block_bytes, scratch_bytes), 16 * 1024 * 1024), VMEM_BYTES_V7X - 4 * 1024 * 1024))
    return pltpu.CompilerParams(dimension_semantics=semantics, vmem_limit_bytes=limit)


def _row_tile(T, block_bytes, scratch_bytes):
    for tm in (1024, 512, 256, 128):
        if T % tm == 0 and _vmem_need(block_bytes(tm), scratch_bytes(tm)) <= VMEM_BUDGET:
            return tm
    raise ValueError("no row tile fits VMEM")


def _nbytes(shape, dtype):
    return int(np.prod(shape)) * jnp.dtype(dtype).itemsize


def _sigmoid(x):
    return 1.0 / (1.0 + jnp.exp(-x))


def _split_bf16(x):
    hi = x.astype(BF16)
    lo = (x - hi.astype(F32)).astype(BF16)
    return hi, lo


def _dot_nt(a, b):
    return lax.dot_general(a, b, (((1,), (1,)), ((), ())), preferred_element_type=F32)


def _dot_tn(a, b):
    return lax.dot_general(a, b, (((0,), (0,)), ((), ())), preferred_element_type=F32)


def _norm_kernel(*refs, has_proj, out_dtype):
    if has_proj:
        x_ref, g_ref, whi_ref, wlo_ref, y_ref, p_ref = refs
    else:
        x_ref, g_ref, y_ref = refs
    x = x_ref[...]
    y = x * lax.rsqrt(jnp.mean(x * x, axis=-1, keepdims=True) + EPS) * g_ref[...]
    y_ref[...] = y.astype(out_dtype)
    if has_proj:
        y_hi, y_lo = _split_bf16(y)
        w_hi = whi_ref[...]
        p = jnp.dot(y_hi, w_hi, preferred_element_type=F32)
        p += jnp.dot(y_hi, wlo_ref[...], preferred_element_type=F32)
        p += jnp.dot(y_lo, w_hi, preferred_element_type=F32)
        p_ref[...] = p


def rmsnorm(x, g, proj=None, out_dtype=BF16):
    T, D = x.shape
    tm = _divisor(T, (256, 128, 8))
    g2 = g.reshape(1, D).astype(F32)
    in_specs = [pl.BlockSpec((tm, D), lambda i: (i, 0)), pl.BlockSpec((1, D), lambda i: (0, 0))]
    out_shape = [jax.ShapeDtypeStruct((T, D), out_dtype)]
    out_specs = [pl.BlockSpec((tm, D), lambda i: (i, 0))]
    args = [x, g2]
    blk = _nbytes((tm, D), F32) + _nbytes((tm, D), out_dtype)
    if proj is not None:
        n = proj.shape[1]
        assert n <= LANES
        w = jnp.pad(proj.astype(F32), ((0, 0), (0, LANES - n)))
        w_hi, w_lo = _split_bf16(w)
        in_specs += [pl.BlockSpec((D, LANES), lambda i: (0, 0))] * 2
        out_shape.append(jax.ShapeDtypeStruct((T, LANES), F32))
        out_specs.append(pl.BlockSpec((tm, LANES), lambda i: (i, 0)))
        args += [w_hi, w_lo]
        blk += 2 * _nbytes((D, LANES), BF16)
    outs = pl.pallas_call(
        functools.partial(_norm_kernel, has_proj=proj is not None, out_dtype=out_dtype),
        grid=(T // tm,),
        in_specs=in_specs,
        out_specs=out_specs,
        out_shape=out_shape,
        compiler_params=_params(("parallel",), blk, 4 * _nbytes((tm, D), F32)),
        name="rmsnorm",
    )(*args)
    return (outs[0], outs[1]) if proj is not None else outs[0]


def _stage_weight(w_ref, w_scr):
    R = w_scr.shape[0]
    rc = _divisor(R, (WEIGHT_STAGE_ROWS, LANES))

    def step(r, carry):
        rows = pl.ds(pl.multiple_of(r * rc, rc), rc)
        w_scr[rows, :] = w_ref[0, rows, :].astype(BF16)
        return carry

    lax.fori_loop(0, R // rc, step, 0)


def _stationary_weights(j, nj, copies, lands, scrs):
    @pl.when(j == 0)
    def _():
        for c in copies(0):
            c.start()

    for c in copies(j):
        c.wait()
    for land, scr in zip(lands, scrs):
        _stage_weight(land, scr)

    @pl.when(j + 1 < nj)
    def _():
        for c in copies(j + 1):
            c.start()


def _matmul_kernel(*refs, layer, w_row, w_col, k, tn, transposed, has_res):
    a_ref, w_hbm = refs[:2]
    r_ref = refs[2] if has_res else None
    o_ref, w_land, w_scr, sem = refs[-4:]
    j = pl.program_id(0)
    nj = pl.num_programs(0)

    def window(jj):
        if transposed:
            src = w_hbm.at[layer, pl.ds(pl.multiple_of(w_col + jj * tn, 8), tn), pl.ds(w_row, k)]
        else:
            src = w_hbm.at[layer, pl.ds(w_row, k), pl.ds(pl.multiple_of(w_col + jj * tn, LANES), tn)]
        return pltpu.make_async_copy(src, w_land.at[0], sem)

    @pl.when(pl.program_id(1) == 0)
    def _():
        _stationary_weights(j, nj, lambda jj: [window(jj)], (w_land,), (w_scr,))

    if transposed:
        d = _dot_nt(a_ref[...], w_scr[...])
    else:
        d = jnp.dot(a_ref[...], w_scr[...], preferred_element_type=F32)
    if has_res:
        d = d + r_ref[...]
    o_ref[...] = d.astype(o_ref.dtype)


def matmul(a, w, layer, out_dtype, *, a_col=0, k=None, w_row=0, w_col=0, n=None, transposed=False, res=None):
    T = a.shape[0]
    k = a.shape[1] if k is None else k
    n_total = w.shape[1] if transposed else w.shape[2]
    n = n_total - w_col if n is None else n
    tn = _divisor(n, (512, 256, 128))
    assert a_col % k == 0 and w_col % (8 if transposed else LANES) == 0
    wshape = (tn, k) if transposed else (k, tn)
    res_b = (lambda tm: _nbytes((tm, tn), F32)) if res is not None else (lambda tm: 0)

    def blk(tm):
        return _nbytes((tm, k), BF16) + _nbytes((tm, tn), out_dtype) + res_b(tm)

    def scr(tm):
        return _nbytes(wshape, F32) + _nbytes(wshape, BF16) + 2 * _nbytes((tm, tn), F32)

    tm = _row_tile(T, blk, scr)
    ab = a_col // k
    in_specs = [pl.BlockSpec((tm, k), lambda j, i: (i, ab)), pl.BlockSpec(memory_space=pl.ANY)]
    args = [a, w]
    if res is not None:
        in_specs.append(pl.BlockSpec((tm, tn), lambda j, i: (i, j)))
        args.append(res)
    return pl.pallas_call(
        functools.partial(_matmul_kernel, layer=layer, w_row=w_row, w_col=w_col, k=k, tn=tn, transposed=transposed,
                          has_res=res is not None),
        grid=(n // tn, T // tm),
        in_specs=in_specs,
        out_specs=pl.BlockSpec((tm, tn), lambda j, i: (i, j)),
        out_shape=jax.ShapeDtypeStruct((T, n), out_dtype),
        scratch_shapes=[pltpu.VMEM((1,) + wshape, F32), pltpu.VMEM(wshape, BF16), pltpu.SemaphoreType.DMA(())],
        compiler_params=_params(("arbitrary", "arbitrary"), blk(tm), scr(tm)),
        name="matmul",
    )(*args)


def matmul_residual(a, w, layer, res):
    K = a.shape[1]
    nk = next(n for n in range(1, K // LANES + 1) if K % (n * LANES) == 0 and K // n <= 6144)
    k = K // nk
    out = res
    for p in range(nk):
        out = matmul(a, w, layer, F32, a_col=p * k, k=k, w_row=p * k, res=out)
    return out


def _swiglu_up_kernel(x_ref, wg_hbm, wu_hbm, o_ref, wg_land, wu_land, wg_scr, wu_scr, sems, *, layer, tn):
    j = pl.program_id(0)

    def copies(jj):
        cols = pl.ds(pl.multiple_of(jj * tn, LANES), tn)
        return [pltpu.make_async_copy(w.at[layer, :, cols], land.at[0], sems.at[n])
                for n, (w, land) in enumerate(((wg_hbm, wg_land), (wu_hbm, wu_land)))]

    @pl.when(pl.program_id(1) == 0)
    def _():
        _stationary_weights(j, pl.num_programs(0), copies, (wg_land, wu_land), (wg_scr, wu_scr))

    x = x_ref[...]
    g = jnp.dot(x, wg_scr[...], preferred_element_type=F32)
    u = jnp.dot(x, wu_scr[...], preferred_element_type=F32)
    o_ref[...] = (g * _sigmoid(g) * u).astype(o_ref.dtype)


def swiglu_up(x, wg, wu, layer):
    T, K = x.shape
    F = wg.shape[2]
    tn = _divisor(F, (512, 256, 128))

    def blk(tm):
        return _nbytes((tm, K), BF16) + _nbytes((tm, tn), BF16)

    def scr(tm):
        return 2 * _nbytes((K, tn), F32) + 2 * _nbytes((K, tn), BF16) + 3 * _nbytes((tm, tn), F32)

    tm = _row_tile(T, blk, scr)
    hbm = pl.BlockSpec(memory_space=pl.ANY)
    return pl.pallas_call(
        functools.partial(_swiglu_up_kernel, layer=layer, tn=tn),
        grid=(F // tn, T // tm),
        in_specs=[pl.BlockSpec((tm, K), lambda j, i: (i, 0)), hbm, hbm],
        out_specs=pl.BlockSpec((tm, tn), lambda j, i: (i, j)),
        out_shape=jax.ShapeDtypeStruct((T, F), BF16),
        scratch_shapes=[pltpu.VMEM((1, K, tn), F32), pltpu.VMEM((1, K, tn), F32),
                        pltpu.VMEM((K, tn), BF16), pltpu.VMEM((K, tn), BF16), pltpu.SemaphoreType.DMA((2,))],
        compiler_params=_params(("arbitrary", "arbitrary"), blk(tm), scr(tm)),
        name="swiglu_up",
    )(x, wg, wu)


def _split_bf16_f32(x):
    hi = x.astype(BF16)
    return hi, x - hi.astype(F32)


def _mlstm_kernel(gb_ref, q_ref, k_ref, v_ref, og_ref, ig_ref, fg_ref, cwq_ref, cwk_ref, an_ref, y_ref,
                  c_scr, n_scr, m_scr, pq_scr, pk_scr, *, n_heads, n_chunks, group):
    hg = pl.program_id(1)
    L = CHUNK
    c_scr[...] = jnp.zeros_like(c_scr)
    n_scr[...] = jnp.zeros_like(n_scr)
    m_scr[...] = jnp.zeros_like(m_scr)
    pq_scr[...] = jnp.zeros_like(pq_scr)
    pk_scr[...] = jnp.zeros_like(pk_scr)
    ri = lax.broadcasted_iota(jnp.int32, (L, L), 0)
    ci = lax.broadcasted_iota(jnp.int32, (L, L), 1)
    lower = ci <= ri
    strict_lower01 = jnp.where(ri > ci, 1.0, 0.0).astype(BF16)

    def conv_silu(x, prev, w):
        y = x * w[CONV_W - 1:CONV_W, :]
        for s in range(1, CONV_W):
            shifted = jnp.where(ri < s, pltpu.roll(prev, s, 0), pltpu.roll(x, s, 0))
            y = y + shifted * w[CONV_W - 1 - s:CONV_W - s, :]
        return y * _sigmoid(y)

    def head_chunk(c, rows, g):
        cols = slice(g * HEAD_DIM, (g + 1) * HEAD_DIM)
        h = hg * group + g
        xq = q_ref[rows, cols].astype(F32)
        xk = k_ref[rows, cols].astype(F32)
        q = conv_silu(xq, pq_scr[g], cwq_ref[:, cols])
        k = conv_silu(xk, pk_scr[g], cwk_ref[:, cols]) * (HEAD_DIM ** -0.5)
        pq_scr[g] = xq
        pk_scr[g] = xk
        v = v_ref[rows, cols]
        q_bf = q.astype(BF16)
        k_bf = k.astype(BF16)
        yield

        i_row = ig_ref[0, g, pl.ds(c, 1), :] + gb_ref[h]
        f_row = fg_ref[0, g, pl.ds(c, 1), :] + gb_ref[n_heads + h]
        lf = jnp.minimum(f_row, 0.0) - jnp.log(1.0 + jnp.exp(-jnp.abs(f_row)))
        lf_low = jnp.where(lower, lf, 0.0)
        b_col = jnp.sum(lf_low, axis=1, keepdims=True)
        hi, rest = _split_bf16_f32(lf_low)
        mid, lo = _split_bf16_f32(rest)
        yield
        dmat = (jnp.dot(hi, strict_lower01, preferred_element_type=F32)
                + jnp.dot(mid, strict_lower01, preferred_element_type=F32)
                + jnp.dot(lo.astype(BF16), strict_lower01, preferred_element_type=F32))
        g_tot = b_col[L - 1:L, :]
        a_row = dmat[L - 1:L, :] + i_row
        a_col = jnp.sum(jnp.where(ci > ri, lf, 0.0) + jnp.where(ci == ri, i_row, 0.0),
                        axis=1, keepdims=True)
        yield

        m_prev = m_scr[g]
        c_prev = c_scr[g]
        n_prev = n_scr[g]

        log_d = jnp.where(lower, dmat + i_row, -jnp.inf)
        log_inter = b_col + m_prev
        m_t = jnp.maximum(log_inter, jnp.max(log_d, axis=1, keepdims=True))
        qk = _dot_nt(q_bf, k_bf)
        yield
        w = jnp.exp(log_d - m_t) * qk
        e_inter = jnp.exp(log_inter - m_t)
        yield
        num = (e_inter * jnp.dot(q_bf, c_prev.astype(BF16), preferred_element_type=F32)
               + jnp.dot(w.astype(BF16), v, preferred_element_type=F32))
        den = e_inter * jnp.sum(q * n_prev, axis=1, keepdims=True) + jnp.sum(w, axis=1, keepdims=True)
        yield
        hcell = num / jnp.maximum(jnp.abs(den), jnp.exp(-m_t))

        gated = _sigmoid(og_ref[rows, cols].astype(F32)) * hcell
        y = gated * lax.rsqrt(jnp.mean(gated * gated, axis=-1, keepdims=True) + EPS) * an_ref[:, cols]
        y_ref[rows, cols] = y.astype(y_ref.dtype)
        yield

        m_new =jnp.maximum(g_tot + m_prev, jnp.max(a_row, axis=1, keepdims=True))
        w_col = jnp.exp(a_col - m_new)
        decay = jnp.exp(g_tot + m_prev - m_new)
        c_scr[g] = decay * c_prev + _dot_tn(k_bf, (v.astype(F32) * w_col).astype(BF16))
        n_scr[g] = decay * n_prev + jnp.sum(k * w_col, axis=0, keepdims=True)
        m_scr[g] = m_new

    def chunk(c, carry):
        rows = pl.ds(pl.multiple_of(c * L, L), L)
        stages = [head_chunk(c, rows, g) for g in range(group)]
        for _ in itertools.zip_longest(*stages):
            pass
        return carry

    lax.fori_loop(0, n_chunks, chunk, 0)


def mlstm_branch(z, zg, a_gate_bias, a_conv, a_norm, batch, seq, n_heads):
    T = batch * seq
    nc = seq // CHUNK
    W = n_heads * HEAD_DIM
    group = _divisor(n_heads, (2, 1))
    ng = n_heads // group
    gw = group * HEAD_DIM

    def gate_rows(g):
        return jnp.transpose(g.reshape(batch, seq, n_heads), (0, 2, 1)).reshape(batch, n_heads, nc, CHUNK)

    ig = gate_rows(zg[:, :n_heads])
    fg = gate_rows(zg[:, n_heads:2 * n_heads])

    def zspec(part):
        return pl.BlockSpec((seq, gw), lambda b, h, gb: (b, part * ng + h))

    gspec = pl.BlockSpec((1, group, nc, CHUNK), lambda b, h, gb: (b, h, 0, 0))
    blk = 5 * _nbytes((seq, gw), BF16) + 2 * _nbytes((group, nc, CHUNK), F32)
    grid_spec = pltpu.PrefetchScalarGridSpec(
        num_scalar_prefetch=1,
        grid=(batch, ng),
        in_specs=[zspec(0), zspec(1), zspec(2), zspec(3), gspec, gspec,
                  pl.BlockSpec((CONV_W, gw), lambda b, h, gb: (0, h)),
                  pl.BlockSpec((CONV_W, gw), lambda b, h, gb: (0, ng + h)),
                  pl.BlockSpec((1, gw), lambda b, h, gb: (0, h))],
        out_specs=pl.BlockSpec((seq, gw), lambda b, h, gb: (b, h)),
        scratch_shapes=[pltpu.VMEM((group, HEAD_DIM, HEAD_DIM), F32), pltpu.VMEM((group, 1, HEAD_DIM), F32),
                        pltpu.VMEM((group, 1, 1), F32), pltpu.VMEM((group, CHUNK, HEAD_DIM), F32),
                        pltpu.VMEM((group, CHUNK, HEAD_DIM), F32)],
    )
    return pl.pallas_call(
        functools.partial(_mlstm_kernel, n_heads=n_heads, n_chunks=nc, group=group),
        grid_spec=grid_spec,
        out_shape=jax.ShapeDtypeStruct((T, W), BF16),
        compiler_params=_params(("parallel", "parallel"), blk, 64 * group * _nbytes((CHUNK, HEAD_DIM), F32)),
        name="mlstm",
    )(a_gate_bias.astype(F32), z, z, z, z, ig, fg, a_conv.astype(F32), a_conv.astype(F32),
      a_norm.reshape(1, W).astype(F32))


def _sgu_kernel(u_ref, v_ref, lng_ref, lnb_ref, ws_ref, bst_ref, y_ref, *, n_groups, chunks_per_block):
    L = CHUNK
    c0 = math.sqrt(2.0 / math.pi)

    def gelu(x):
        return x * (0.5 * (1.0 + jnp.tanh(c0 * (x + 0.044715 * (x * x * x)))))

    v = gelu(v_ref[...].astype(F32))
    mu = jnp.mean(v, axis=-1, keepdims=True)
    vc = v - mu
    vn = vc * lax.rsqrt(jnp.mean(vc * vc, axis=-1, keepdims=True) + EPS) * lng_ref[...] + lnb_ref[...]
    vn = vn.astype(BF16)
    ri = lax.broadcasted_iota(jnp.int32, (L, L), 0)
    ci = lax.broadcasted_iota(jnp.int32, (L, L), 1)
    bst = bst_ref[...]
    for g in range(n_groups):
        cs = slice(g * HEAD_DIM, (g + 1) * HEAD_DIM)
        w = jnp.where(ci <= ri, ws_ref[g], 0.0).astype(BF16)
        bias = bst[:, g:g + 1]
        for c in range(chunks_per_block):
            rs = slice(c * L, (c + 1) * L)
            s = jnp.dot(w, vn[rs, cs], preferred_element_type=F32) + bias
            y_ref[rs, cs] = (gelu(u_ref[rs, cs].astype(F32)) * s).astype(y_ref.dtype)


def sgu_branch(z, b_ln_g, b_ln_b, b_ws, b_bs, n_groups):
    T = z.shape[0]
    W = n_groups * HEAD_DIM
    cpb = _divisor(T // CHUNK, (4, 2, 1))
    R = cpb * CHUNK
    blk = 3 * _nbytes((R, W), BF16) + _nbytes((n_groups, CHUNK, CHUNK), F32)
    return pl.pallas_call(
        functools.partial(_sgu_kernel, n_groups=n_groups, chunks_per_block=cpb),
        grid=(T // R,),
        in_specs=[pl.BlockSpec((R, W), lambda i: (i, 0)),
                  pl.BlockSpec((R, W), lambda i: (i, 1)),
                  pl.BlockSpec((1, W), lambda i: (0, 0)),
                  pl.BlockSpec((1, W), lambda i: (0, 0)),
                  pl.BlockSpec((n_groups, CHUNK, CHUNK), lambda i: (0, 0, 0)),
                  pl.BlockSpec((CHUNK, n_groups), lambda i: (0, 0))],
        out_specs=pl.BlockSpec((R, W), lambda i: (i, 0)),
        out_shape=jax.ShapeDtypeStruct((T, W), BF16),
        compiler_params=_params(("parallel",), blk, 6 * _nbytes((R, W), F32)),
        name="spatial_gating",
    )(z, z, b_ln_g.reshape(1, W).astype(F32), b_ln_b.reshape(1, W).astype(F32), b_ws.astype(F32),
      jnp.transpose(b_bs).astype(F32))


def _t5_bucket_table(n):
    d = np.arange(n, dtype=np.int64)
    max_exact = N_BUCKETS // 2
    nf = np.maximum(d, 1).astype(np.float32)
    scaled = (np.log(nf / np.float32(max_exact)) / np.float32(math.log(MAX_DIST / max_exact))
              * np.float32(N_BUCKETS - max_exact))
    large = np.minimum(max_exact + scaled.astype(np.int32), N_BUCKETS - 1)
    return np.where(d < max_exact, d, large).astype(np.int32)


def _bias_kernel(rb_ref, bkt_ref, o_ref, *, n_heads):
    h = pl.program_id(0)
    bkt = bkt_ref[...]
    far = rb_ref[(N_BUCKETS - 1) * n_heads + h]
    out = jnp.full(bkt.shape, -jnp.inf, F32)
    for b in range(N_BUCKETS):
        out = jnp.where(bkt == b, rb_ref[b * n_heads + h] - far, out)
    o_ref[0] = out


def attn_bias_tiles(rel_bias, tq):
    n_heads = rel_bias.shape[1]
    assert tq >= MAX_DIST
    c = np.arange(2 * tq)[:, None]
    r = np.arange(tq)[None, :]
    dist = r - c + tq
    table = _t5_bucket_table(2 * tq)
    bkt = np.where(dist >= 0, table[np.maximum(dist, 0)], -1).astype(np.int32)
    grid_spec = pltpu.PrefetchScalarGridSpec(
        num_scalar_prefetch=1,
        grid=(n_heads,),
        in_specs=[pl.BlockSpec((2 * tq, tq), lambda h, rb: (0, 0))],
        out_specs=pl.BlockSpec((1, 2 * tq, tq), lambda h, rb: (h, 0, 0)),
    )
    return pl.pallas_call(
        functools.partial(_bias_kernel, n_heads=n_heads),
        grid_spec=grid_spec,
        out_shape=jax.ShapeDtypeStruct((n_heads, 2 * tq, tq), F32),
        compiler_params=_params(("arbitrary",), 2 * _nbytes((2 * tq, tq), F32)),
        name="attn_bias_tiles",
    )(rel_bias.astype(F32).reshape(-1), jnp.asarray(bkt))


ONES_ROWS = 16
ATTN_FAR_KEYS = 1024


def _attn_kernel(q_ref, k_ref, v_ref, nb_ref, lam_ref, cn_ref, y_ref,
                 vt_scr, m1_scr, a1_scr, m2_scr, a2_scr, *, tq, seq, lam_init):
    qi = pl.program_id(2)
    dk = HEAD_DIM // 2
    dv = HEAD_DIM
    scale = dk ** -0.5

    @pl.when(qi == 0)
    def _():
        for c in range(seq // tq):
            cs = slice(c * tq, (c + 1) * tq)
            vt_scr[:dv, cs] = v_ref[cs, :].astype(F32).T.astype(BF16)
        vt_scr[dv:, :] = jnp.ones((ONES_ROWS, seq), BF16)

    q = (q_ref[...].astype(F32) * scale).astype(BF16)
    lane = lax.broadcasted_iota(jnp.int32, q.shape, 1)
    zero = jnp.zeros_like(q)
    q1 = jnp.where(lane < dk, q, zero)
    q2 = jnp.where(lane >= dk, q, zero)
    for m_scr, a_scr in ((m1_scr, a1_scr), (m2_scr, a2_scr)):
        m_scr[...] = jnp.full(m_scr.shape, -jnp.inf, F32)
        a_scr[...] = jnp.zeros_like(a_scr)

    def update(s, vtb, m_scr, a_scr):
        m_prev = m_scr[...]
        sb = s.astype(BF16)
        m_new = jnp.maximum(m_prev, jnp.max(sb, axis=0, keepdims=True).astype(F32))
        alpha = jnp.exp(m_prev - m_new)
        p = jnp.exp(sb - m_new.astype(BF16))
        a_scr[...] = alpha * a_scr[...] + jnp.dot(vtb, p, preferred_element_type=F32)
        m_scr[...] = m_new

    def block(kstart, nk, bias):
        rows = pl.ds(pl.multiple_of(kstart, tq), nk)
        kb = k_ref[rows, :]
        vtb = vt_scr[:, rows]
        s1 = _dot_nt(kb, q1)
        s2 = _dot_nt(kb, q2)
        if bias is not None:
            s1 = s1 + bias
            s2 = s2 + bias
        update(s1, vtb, m1_scr, a1_scr)
        update(s2, vtb, m2_scr, a2_scr)

    big = max(ATTN_FAR_KEYS // tq, 1)
    n_far = jnp.maximum(qi - 1, 0)
    n_big = n_far // big

    def far_block(j, carry):
        block(j * (big * tq), big * tq, None)
        return carry

    lax.fori_loop(0, n_big, far_block, 0)
    rem = n_far - n_big * big
    unit = big // 2
    while unit >= 1:
        done = n_big * big + (rem // (2 * unit)) * (2 * unit)

        @pl.when((rem // unit) % 2 == 1)
        def _(done=done, unit=unit):
            block(done * tq, unit * tq, None)

        unit //= 2

    @pl.when(qi > 0)
    def _():
        block((qi - 1) * tq, 2 * tq, nb_ref[0])

    @pl.when(qi == 0)
    def _():
        block(0, tq, nb_ref[0, tq:, :])

    lf = lam_ref[...]
    lam = (jnp.exp(jnp.sum(lf[0:1] * lf[1:2], axis=-1, keepdims=True))
           - jnp.exp(jnp.sum(lf[2:3] * lf[3:4], axis=-1, keepdims=True)) + lam_init)
    a1 = a1_scr[...]
    a2 = a2_scr[...]
    o = a1[:dv] / a1[dv:dv + 1] - lam * (a2[:dv] / a2[dv:dv + 1])
    o = o * lax.rsqrt(jnp.mean(o * o, axis=0, keepdims=True) + EPS) * cn_ref[...] * (1.0 - lam_init)
    y_ref[...] = o.T.astype(y_ref.dtype)


def attn_branch(z, col0, near_bias, c_lambda, c_norm, batch, seq, n_heads, lam_init, tq):
    T = batch * seq
    nq = seq // tq
    W = n_heads * HEAD_DIM
    blk = (2 * _nbytes((tq, HEAD_DIM), BF16) + 2 * _nbytes((seq, HEAD_DIM), BF16)
           + _nbytes((tq, 2 * tq), F32))
    return pl.pallas_call(
        functools.partial(_attn_kernel, tq=tq, seq=seq, lam_init=lam_init),
        grid=(batch, n_heads, nq),
        in_specs=[pl.BlockSpec((tq, HEAD_DIM), lambda b, h, i: (b * nq + i, col0 + h)),
                  pl.BlockSpec((seq, HEAD_DIM), lambda b, h, i: (b, col0 + n_heads + h)),
                  pl.BlockSpec((seq, HEAD_DIM), lambda b, h, i: (b, col0 + 2 * n_heads + h)),
                  pl.BlockSpec((1, 2 * tq, tq), lambda b, h, i: (h, 0, 0)),
                  pl.BlockSpec(c_lambda.shape, lambda b, h, i: (0, 0)),
                  pl.BlockSpec((HEAD_DIM, 1), lambda b, h, i: (0, 0))],
        out_specs=pl.BlockSpec((tq, HEAD_DIM), lambda b, h, i: (b * nq + i, h)),
        out_shape=jax.ShapeDtypeStruct((T, W), BF16),
        scratch_shapes=[pltpu.VMEM((HEAD_DIM + ONES_ROWS, seq), BF16),
                        pltpu.VMEM((1, tq), F32), pltpu.VMEM((HEAD_DIM + ONES_ROWS, tq), F32),
                        pltpu.VMEM((1, tq), F32), pltpu.VMEM((HEAD_DIM + ONES_ROWS, tq), F32)],
        compiler_params=_params(("parallel", "parallel", "arbitrary"), blk,
                                _nbytes((HEAD_DIM + ONES_ROWS, seq), BF16)
                                + 8 * _nbytes((max(ATTN_FAR_KEYS, 2 * tq), tq), F32)),
        name="diff_attention",
    )(z, z, z, near_bias, c_lambda.astype(F32), c_norm.reshape(HEAD_DIM, 1).astype(F32))


def _merge_kernel(ya_ref, yb_ref, yc_ref, wa_ref, wb_ref, wc_ref, g0_ref, g1_ref, g2_ref, o_ref,
                  wa_scr, wb_scr, wc_scr):
    @pl.when(pl.program_id(1) == 0)
    def _():
        _stage_weight(wa_ref, wa_scr)
        _stage_weight(wb_ref, wb_scr)
        _stage_weight(wc_ref, wc_scr)

    da = jnp.dot(ya_ref[...], wa_scr[...], preferred_element_type=F32)
    db = jnp.dot(yb_ref[...], wb_scr[...], preferred_element_type=F32)
    dc = jnp.dot(yc_ref[...], wc_scr[...], preferred_element_type=F32)
    merged = (_sigmoid(g0_ref[...].astype(F32)) * da + _sigmoid(g1_ref[...].astype(F32)) * db
              + _sigmoid(g2_ref[...].astype(F32)) * dc)
    o_ref[...] = merged.astype(o_ref.dtype)


def gated_merge(y_a, y_b, y_c, w_a, w_b, w_c, layer, z, gate_col0, d_model):
    T = z.shape[0]
    goff = gate_col0 * HEAD_DIM
    tn = _divisor(math.gcd(goff, d_model), (512, 256, 128))
    ka, kb, kc = y_a.shape[1], y_b.shape[1], y_c.shape[1]
    ks = ka + kb + kc

    def blk(tm):
        return _nbytes((tm, ks), BF16) + _nbytes((ks, tn), F32) + 4 * _nbytes((tm, tn), BF16)

    def scr(tm):
        return _nbytes((ks, tn), BF16) + 6 * _nbytes((tm, tn), F32)

    tm = _row_tile(T, blk, scr)

    def gspec(j):
        base = (goff + j * d_model) // tn
        return pl.BlockSpec((tm, tn), lambda n, i: (i, base + n))

    def wspec(k):
        return pl.BlockSpec((1, k, tn), lambda n, i: (layer, 0, n))

    return pl.pallas_call(
        _merge_kernel,
        grid=(d_model // tn, T // tm),
        in_specs=[pl.BlockSpec((tm, ka), lambda n, i: (i, 0)),
                  pl.BlockSpec((tm, kb), lambda n, i: (i, 0)),
                  pl.BlockSpec((tm, kc), lambda n, i: (i, 0)),
                  wspec(ka), wspec(kb), wspec(kc), gspec(0), gspec(1), gspec(2)],
        out_specs=pl.BlockSpec((tm, tn), lambda n, i: (i, n)),
        out_shape=jax.ShapeDtypeStruct((T, d_model), BF16),
        scratch_shapes=[pltpu.VMEM((ka, tn), BF16), pltpu.VMEM((kb, tn), BF16), pltpu.VMEM((kc, tn), BF16)],
        compiler_params=_params(("parallel", "arbitrary"), blk(tm), scr(tm)),
        name="gated_merge",
    )(y_a, y_b, y_c, w_a, w_b, w_c, z, z, z)


def _route_kernel(l_ref, idx_ref, w_ref, *, n_experts):
    logits = l_ref[...]
    lane = lax.broadcasted_iota(jnp.int32, logits.shape, 1)
    logits = jnp.where(lane < n_experts, logits, -jnp.inf)
    m1 = jnp.max(logits, axis=-1, keepdims=True)
    i1 = jnp.min(jnp.where(logits == m1, lane, LANES), axis=-1, keepdims=True)
    rest = jnp.where(lane == i1, -jnp.inf, logits)
    m2 = jnp.max(rest, axis=-1, keepdims=True)
    i2 = jnp.min(jnp.where(rest == m2, lane, LANES), axis=-1, keepdims=True)
    e = jnp.exp(m2 - m1)
    w1 = 1.0 / (1.0 + e)
    w2 = e / (1.0 + e)
    idx_ref[...] = jnp.where(lane == 0, i1, jnp.where(lane == 1, i2, 0))
    w_ref[...] = jnp.where(lane == 0, w1, jnp.where(lane == 1, w2, 0.0))


def route_top2(logits, n_experts):
    T = logits.shape[0]
    tm = _divisor(T, (1024, 512, 256, 128, 8))
    spec = pl.BlockSpec((tm, LANES), lambda i: (i, 0))
    idx, w = pl.pallas_call(
        functools.partial(_route_kernel, n_experts=n_experts),
        grid=(T // tm,),
        in_specs=[spec],
        out_specs=[spec, spec],
        out_shape=[jax.ShapeDtypeStruct((T, LANES), jnp.int32), jax.ShapeDtypeStruct((T, LANES), F32)],
        compiler_params=_params(("parallel",), 3 * _nbytes((tm, LANES), F32)),
        name="route_top2",
    )(logits)
    return idx[:, :TOP_K], w


def _moe_gather_kernel(src_ref, nv_ref, x_hbm, o_ref, buf, sem, *, tg):
    i = pl.program_id(0)
    n_valid = nv_ref[0]
    slot = i % 2

    def row_copy(tile, r, sl):
        tok = src_ref[tile * tg + r]
        return pltpu.make_async_copy(x_hbm.at[pl.ds(tok, 1), :], buf.at[sl, pl.ds(r, 1), :], sem.at[sl])

    def start_tile(tile, sl):
        def body(r, carry):
            row_copy(tile, r, sl).start()
            return carry
        lax.fori_loop(0, tg, body, 0)

    @pl.when(i == 0)
    def _():
        start_tile(0, 0)

    @pl.when(i + 1 < n_valid)
    def _():
        start_tile(i + 1, 1 - slot)

    @pl.when(i < n_valid)
    def _():
        def body(r, carry):
            row_copy(i, r, slot).wait()
            return carry
        lax.fori_loop(0, tg, body, 0)
        o_ref[...] = buf[slot].astype(o_ref.dtype)

    @pl.when(i >= n_valid)
    def _():
        o_ref[...] = jnp.zeros_like(o_ref)


def _weight_window(w_hbm, e, j, tn):
    return w_hbm.at[e, :, pl.ds(pl.multiple_of(j * tn, tn), tn)]


def _grouped_weights(s, sched, w_hbms, lands, scrs, sems, tn):
    ew_ref, jw_ref, fl_ref, ne_ref, nj_ref = sched

    def copies(e, j):
        return [pltpu.make_async_copy(_weight_window(w, e, j, tn), land.at[0], sems.at[k])
                for k, (w, land) in enumerate(zip(w_hbms, lands))]

    @pl.when(fl_ref[s] == 2)
    def _():
        @pl.when(s == 0)
        def _():
            for c in copies(ew_ref[0], jw_ref[0]):
                c.start()

        for c in copies(ew_ref[s], jw_ref[s]):
            c.wait()
        for land, scr in zip(lands, scrs):
            _stage_weight(land, scr)

        @pl.when(ne_ref[s] >= 0)
        def _():
            for c in copies(ne_ref[s], nj_ref[s]):
                c.start()


def _moe_up_kernel(t_ref, jo_ref, ew_ref, jw_ref, fl_ref, ne_ref, nj_ref, x_ref, wg_hbm, wu_hbm, o_ref,
                   wg_land, wu_land, wg_scr, wu_scr, sems, *, tf):
    s = pl.program_id(0)
    _grouped_weights(s, (ew_ref, jw_ref, fl_ref, ne_ref, nj_ref), (wg_hbm, wu_hbm), (wg_land, wu_land),
                     (wg_scr, wu_scr), sems, tf)

    @pl.when(fl_ref[s] > 0)
    def _():
        x = x_ref[...]
        g = jnp.dot(x, wg_scr[...], preferred_element_type=F32)
        u = jnp.dot(x, wu_scr[...], preferred_element_type=F32)
        o_ref[...] = (g * _sigmoid(g) * u).astype(o_ref.dtype)

    @pl.when(fl_ref[s] == 0)
    def _():
        o_ref[...] = jnp.zeros_like(o_ref)


def _moe_down_kernel(t_ref, jo_ref, ew_ref, jw_ref, fl_ref, ne_ref, nj_ref, a_ref, wd_hbm, o_ref,
                     wd_land, wd_scr, sems, *, tn):
    s = pl.program_id(0)
    _grouped_weights(s, (ew_ref, jw_ref, fl_ref, ne_ref, nj_ref), (wd_hbm,), (wd_land,), (wd_scr,), sems, tn)

    @pl.when(fl_ref[s] > 0)
    def _():
        o_ref[...] = jnp.dot(a_ref[...], wd_scr[...], preferred_element_type=F32)

    @pl.when(fl_ref[s] == 0)
    def _():
        o_ref[...] = jnp.zeros_like(o_ref)


def _moe_schedule(tiles_per_expert, n_tiles, n_cols):
    E = tiles_per_expert.shape[0]
    cnt = jnp.concatenate([tiles_per_expert, (n_tiles - jnp.sum(tiles_per_expert))[None]]).astype(jnp.int32)
    tile0 = jnp.cumsum(cnt) - cnt
    step_end = jnp.cumsum(cnt * n_cols)
    step0 = step_end - cnt * n_cols
    s = jnp.arange(n_tiles * n_cols, dtype=jnp.int32)
    g = jnp.sum((s[:, None] >= step_end[None, :]).astype(jnp.int32), axis=1)
    within = s - step0[g]
    c = jnp.maximum(cnt[g], 1)
    col = within // c
    tile = tile0[g] + within % c
    valid = g < E
    first = jnp.logical_and(valid, within % c == 0)
    n_valid = step_end[E - 1]
    last = jnp.maximum(n_valid - 1, 0)
    w_e = jnp.where(valid, g, g[last])
    w_j = jnp.where(valid, col, col[last])
    flag = jnp.where(first, 2, jnp.where(valid, 1, 0))
    nxt = s + c
    has_next = jnp.logical_and(first, nxt < n_valid)
    nxt = jnp.minimum(nxt, n_tiles * n_cols - 1)
    n_e = jnp.where(has_next, g[nxt], -1)
    n_j = jnp.where(has_next, col[nxt], 0)
    return [a.astype(jnp.int32) for a in (tile, col, w_e, w_j, flag, n_e, n_j)]


def _moe_combine_kernel(dest_ref, y_hbm, h_ref, w_ref, *rest, tm, final_norm):
    if final_norm:
        g_ref, o_ref, buf0, buf1, sem = rest
    else:
        o_ref, buf0, buf1, sem = rest
    i = pl.program_id(0)

    def row_copy(r, k, row):
        buf = buf0 if k == 0 else buf1
        return pltpu.make_async_copy(y_hbm.at[pl.ds(row, 1), :], buf.at[pl.ds(r, 1), :], sem.at[k])

    def start(r, carry):
        a = TOP_K * (i * tm + r)
        row_copy(r, 0, dest_ref[a]).start()
        row_copy(r, 1, dest_ref[a + 1]).start()
        return carry

    def wait(r, carry):
        row_copy(r, 0, 0).wait()
        row_copy(r, 1, 0).wait()
        return carry

    lax.fori_loop(0, tm, start, 0)
    lax.fori_loop(0, tm, wait, 0)
    w = w_ref[...]
    out = h_ref[...] + (w[:, 0:1] * buf0[...] + w[:, 1:2] * buf1[...])
    if final_norm:
        out = out * lax.rsqrt(jnp.mean(out * out, axis=-1, keepdims=True) + EPS) * g_ref[...]
    o_ref[...] = out


def moe_ffn(h, hn, logits, wg, wu, wd, layer, final_gain=None):
    T, D = hn.shape
    E, F = wg.shape[1], wg.shape[3]
    idx, wts = route_top2(logits, E)
    tm = _divisor(T, (256, 128))
    A = T * TOP_K
    n_tiles = A // tm + E

    e_flat = idx.reshape(-1)
    onehot = (e_flat[:, None] == jnp.arange(E, dtype=jnp.int32)[None, :]).astype(jnp.int32)
    csum = jnp.cumsum(onehot, axis=0)
    pos = jnp.take_along_axis(csum, e_flat[:, None], axis=1)[:, 0] - 1
    counts = csum[-1]
    padded = ((counts + tm - 1) // tm) * tm
    ends = jnp.cumsum(padded)
    starts = ends - padded
    dest = (starts[e_flat] + pos).astype(jnp.int32)
    src = jnp.zeros((n_tiles * tm,), jnp.int32).at[dest].set(jnp.arange(A, dtype=jnp.int32) // TOP_K)
    tiles_per_expert = padded // tm
    n_valid_tiles = jnp.sum(tiles_per_expert).astype(jnp.int32).reshape(1)

    x_sorted = pl.pallas_call(
        functools.partial(_moe_gather_kernel, tg=tm),
        grid_spec=pltpu.PrefetchScalarGridSpec(
            num_scalar_prefetch=2,
            grid=(n_tiles,),
            in_specs=[pl.BlockSpec(memory_space=pl.ANY)],
            out_specs=pl.BlockSpec((tm, D), lambda i, sr, nv: (i, 0)),
            scratch_shapes=[pltpu.VMEM((2, tm, D), F32), pltpu.SemaphoreType.DMA((2,))],
        ),
        out_shape=jax.ShapeDtypeStruct((n_tiles * tm, D), BF16),
        compiler_params=_params(("arbitrary",), _nbytes((tm, D), BF16), 3 * _nbytes((tm, D), F32)),
        name="moe_gather",
    )(src, n_valid_tiles, hn)

    def experts(w):
        return w.reshape((-1,) + w.shape[2:])

    tf = _divisor(F, (512, 256, 128))
    sched = _moe_schedule(tiles_per_expert, n_tiles, F // tf)
    sched[2] = sched[2] + layer * E
    sched[5] = jnp.where(sched[5] >= 0, sched[5] + layer * E, -1)
    hbm = pl.BlockSpec(memory_space=pl.ANY)
    act = pl.pallas_call(
        functools.partial(_moe_up_kernel, tf=tf),
        grid_spec=pltpu.PrefetchScalarGridSpec(
            num_scalar_prefetch=7,
            grid=(n_tiles * (F // tf),),
            in_specs=[pl.BlockSpec((tm, D), lambda s, t, jo, *_: (t[s], 0)), hbm, hbm],
            out_specs=pl.BlockSpec((tm, tf), lambda s, t, jo, *_: (t[s], jo[s])),
            scratch_shapes=[pltpu.VMEM((1, D, tf), F32), pltpu.VMEM((1, D, tf), F32),
                            pltpu.VMEM((D, tf), BF16), pltpu.VMEM((D, tf), BF16), pltpu.SemaphoreType.DMA((2,))],
        ),
        out_shape=jax.ShapeDtypeStruct((n_tiles * tm, F), BF16),
        compiler_params=_params(("arbitrary",), _nbytes((tm, D), BF16) + _nbytes((tm, tf), BF16),
                                2 * _nbytes((D, tf), F32) + 2 * _nbytes((D, tf), BF16) + 3 * _nbytes((tm, tf), F32)),
        name="moe_up",
    )(*sched, x_sorted, experts(wg), experts(wu))

    tn = _divisor(D, (1024, 512, 256, 128))
    sched = _moe_schedule(tiles_per_expert, n_tiles, D // tn)
    sched[2] = sched[2] + layer * E
    sched[5] = jnp.where(sched[5] >= 0, sched[5] + layer * E, -1)
    y_sorted = pl.pallas_call(
        functools.partial(_moe_down_kernel, tn=tn),
        grid_spec=pltpu.PrefetchScalarGridSpec(
            num_scalar_prefetch=7,
            grid=(n_tiles * (D // tn),),
            in_specs=[pl.BlockSpec((tm, F), lambda s, t, jo, *_: (t[s], 0)), hbm],
            out_specs=pl.BlockSpec((tm, tn), lambda s, t, jo, *_: (t[s], jo[s])),
            scratch_shapes=[pltpu.VMEM((1, F, tn), F32), pltpu.VMEM((F, tn), BF16), pltpu.SemaphoreType.DMA((1,))],
        ),
        out_shape=jax.ShapeDtypeStruct((n_tiles * tm, D), F32),
        compiler_params=_params(("arbitrary",), _nbytes((tm, F), BF16) + _nbytes((tm, tn), F32),
                                _nbytes((F, tn), F32) + _nbytes((F, tn), BF16) + _nbytes((tm, tn), F32)),
        name="moe_down",
    )(*sched, act, experts(wd))

    tc = _divisor(T, (256, 128))
    blk_c = 2 * _nbytes((tc, D), F32) + _nbytes((tc, LANES), F32)
    in_specs = [hbm, pl.BlockSpec((tc, D), lambda i, de: (i, 0)), pl.BlockSpec((tc, LANES), lambda i, de: (i, 0))]
    args = [dest, y_sorted, h, wts]
    if final_gain is not None:
        in_specs.append(pl.BlockSpec((1, D), lambda i, de: (0, 0)))
        args.append(final_gain.reshape(1, D).astype(F32))
    return pl.pallas_call(
        functools.partial(_moe_combine_kernel, tm=tc, final_norm=final_gain is not None),
        grid_spec=pltpu.PrefetchScalarGridSpec(
            num_scalar_prefetch=1,
            grid=(T // tc,),
            in_specs=in_specs,
            out_specs=pl.BlockSpec((tc, D), lambda i, de: (i, 0)),
            scratch_shapes=[pltpu.VMEM((tc, D), F32), pltpu.VMEM((tc, D), F32), pltpu.SemaphoreType.DMA((2,))],
        ),
        out_shape=jax.ShapeDtypeStruct((T, D), F32),
        compiler_params=_params(("arbitrary",), blk_c, 3 * _nbytes((tc, D), F32)),
        name="moe_combine",
    )(*args)


def _mixer(h, li, batch, seq, near_bias, tq, norm_mix, w_in, a_gate_bias, a_conv, a_norm, b_ln_g, b_ln_b, b_ws,
           b_bs, c_lambda, c_norm, w_br_a, w_br_b, w_br_c, w_out):
    d_model = h.shape[1]
    h_a = a_gate_bias.shape[1] // 2
    g_b = b_ws.shape[1]
    h_c = w_br_c.shape[1] // HEAD_DIM
    w_a = h_a * HEAD_DIM
    g0 = 4 * w_a
    g1 = g0 + 2 * h_a
    n_rest = w_in.shape[2] - g1
    w_in_t = jnp.swapaxes(w_in, 1, 2)
    n, zg = rmsnorm(h, norm_mix[li], proj=jnp.transpose(w_in_t[li, g0:g1, :]))
    z_a = matmul(n, w_in_t, li, BF16, n=g0, transposed=True)
    z_r = matmul(n, w_in_t, li, BF16, w_col=g1, n=n_rest, transposed=True)
    y_a = mlstm_branch(z_a, zg, a_gate_bias[li], a_conv[li], a_norm[li], batch, seq, h_a)
    y_b = sgu_branch(z_r, b_ln_g[li], b_ln_b[li], b_ws[li], b_bs[li], g_b)
    lam_init = 0.8 - 0.6 * math.exp(-0.3 * li)
    y_c = attn_branch(z_r, 2 * g_b, near_bias, c_lambda[li], c_norm[li], batch, seq, h_c, lam_init, tq)
    merged = gated_merge(y_a, y_b, y_c, w_br_a, w_br_b, w_br_c, li, z_r, 2 * g_b + 3 * h_c, d_model)
    return matmul_residual(merged, w_out, li, h)


def kernel(x, norm_mix, w_in, a_gate_bias, a_conv, a_norm, b_ln_g, b_ln_b, b_ws, b_bs, c_lambda, c_norm, rel_bias,
           w_br_a, w_br_b, w_br_c, w_out, norm_ffn, ffn_wg, ffn_wu, ffn_wd, router, moe_wg, moe_wu, moe_wd,
           final_norm):
    batch, seq, d_model = x.shape
    depth = w_in.shape[0]
    tq = _divisor(seq, (512, 256, 128))
    near_bias = attn_bias_tiles(rel_bias, tq)
    h = x.reshape(batch * seq, d_model).astype(F32)
    for li in range(depth):
        h = _mixer(h, li, batch, seq, near_bias, tq, norm_mix, w_in, a_gate_bias, a_conv, a_norm, b_ln_g, b_ln_b,
                   b_ws, b_bs, c_lambda, c_norm, w_br_a, w_br_b, w_br_c, w_out)
        j = li // 2
        fused_final = False
        if li % 2 == 0:
            hn = rmsnorm(h, norm_ffn[li])
            act = swiglu_up(hn, ffn_wg, ffn_wu, j)
            h = matmul_residual(act, ffn_wd, j, h)
        else:
            hn, logits = rmsnorm(h, norm_ffn[li], proj=router[j], out_dtype=F32)
            fused_final = li == depth - 1
            h = moe_ffn(h, hn, logits, moe_wg, moe_wu, moe_wd, j, final_norm if fused_final else None)
    out = h if fused_final else rmsnorm(h, final_norm, out_dtype=F32)
    return out.reshape(batch, seq, d_model).astype(x.dtype)
```

```python
import functools
import itertools
import math

import numpy as np
import jax
import jax.numpy as jnp
from jax import lax
from jax.experimental import pallas as pl
from jax.experimental.pallas import tpu as pltpu

F32 = jnp.float32
BF16 = jnp.bfloat16

EPS = 1e-6
HEAD_DIM = 128
CHUNK = 128
CONV_W = 4
N_BUCKETS = 32
MAX_DIST = 128
TOP_K = 2
LANES = 128
VMEM_BYTES_V7X = 64 * 1024 * 1024
VMEM_SLACK = 6 * 1024 * 1024
VMEM_BUDGET = VMEM_BYTES_V7X - 8 * 1024 * 1024
WEIGHT_STAGE_ROWS = 256
DMA_LOOP_UNROLL = 8


def _divisor(n, candidates):
    for c in candidates:
        if n % c == 0:
            return c
    raise ValueError(f"no tile size in {candidates} divides {n}")


def _vmem_need(block_bytes, scratch_bytes):
    return 2 * block_bytes + scratch_bytes + VMEM_SLACK


def _params(semantics, block_bytes, scratch_bytes=0):
    limit = int(min(max(_vmem_need(block_bytes, scratch_bytes), 16 * 1024 * 1024), VMEM_BYTES_V7X - 4 * 1024 * 1024))
    return pltpu.CompilerParams(dimension_semantics=semantics, vmem_limit_bytes=limit)


def _row_tile(T, block_bytes, scratch_bytes):
    for tm in (1024, 512, 256, 128):
        if T % tm == 0 and _vmem_need(block_bytes(tm), scratch_bytes(tm)) <= VMEM_BUDGET:
            return tm
    raise ValueError("no row tile fits VMEM")


def _nbytes(shape, dtype):
    return int(np.prod(shape)) * jnp.dtype(dtype).itemsize


def _sigmoid(x):
    return 1.0 / (1.0 + jnp.exp(-x))


def _split_bf16(x):
    hi = x.astype(BF16)
    lo = (x - hi.astype(F32)).astype(BF16)
    return hi, lo


def _dot_nt(a, b):
    return lax.dot_general(a, b, (((1,), (1,)), ((), ())), preferred_element_type=F32)


def _dot_tn(a, b):
    return lax.dot_general(a, b, (((0,), (0,)), ((), ())), preferred_element_type=F32)


def _norm_kernel(*refs, has_proj, out_dtype):
    if has_proj:
        x_ref, g_ref, whi_ref, wlo_ref, y_ref, p_ref = refs
    else:
        x_ref, g_ref, y_ref = refs
    x = x_ref[...]
    y = x * lax.rsqrt(jnp.mean(x * x, axis=-1, keepdims=True) + EPS) * g_ref[...]
    y_ref[...] = y.astype(out_dtype)
    if has_proj:
        y_hi, y_lo = _split_bf16(y)
        w_hi = whi_ref[...]
        p = jnp.dot(y_hi, w_hi, preferred_element_type=F32)
        p += jnp.dot(y_hi, wlo_ref[...], preferred_element_type=F32)
        p += jnp.dot(y_lo, w_hi, preferred_element_type=F32)
        p_ref[...] = p


def rmsnorm(x, g, proj=None, out_dtype=BF16):
    T, D = x.shape
    tm = _divisor(T, (256, 128, 8))
    g2 = g.reshape(1, D).astype(F32)
    in_specs = [pl.BlockSpec((tm, D), lambda i: (i, 0)), pl.BlockSpec((1, D), lambda i: (0, 0))]
    out_shape = [jax.ShapeDtypeStruct((T, D), out_dtype)]
    out_specs = [pl.BlockSpec((tm, D), lambda i: (i, 0))]
    args = [x, g2]
    blk = _nbytes((tm, D), F32) + _nbytes((tm, D), out_dtype)
    if proj is not None:
        n = proj.shape[1]
        assert n <= LANES
        w = jnp.pad(proj.astype(F32), ((0, 0), (0, LANES - n)))
        w_hi, w_lo = _split_bf16(w)
        in_specs += [pl.BlockSpec((D, LANES), lambda i: (0, 0))] * 2
        out_shape.append(jax.ShapeDtypeStruct((T, LANES), F32))
        out_specs.append(pl.BlockSpec((tm, LANES), lambda i: (i, 0)))
        args += [w_hi, w_lo]
        blk += 2 * _nbytes((D, LANES), BF16)
    outs = pl.pallas_call(
        functools.partial(_norm_kernel, has_proj=proj is not None, out_dtype=out_dtype),
        grid=(T // tm,),
        in_specs=in_specs,
        out_specs=out_specs,
        out_shape=out_shape,
        compiler_params=_params(("parallel",), blk, 4 * _nbytes((tm, D), F32)),
        name="rmsnorm",
    )(*args)
    return (outs[0], outs[1]) if proj is not None else outs[0]


def _stage_weight(w_ref, w_scr):
    R = w_scr.shape[0]
    rc = _divisor(R, (WEIGHT_STAGE_ROWS, LANES))

    def step(r, carry):
        rows = pl.ds(pl.multiple_of(r * rc, rc), rc)
        w_scr[rows, :] = w_ref[0, rows, :].astype(BF16)
        return carry

    lax.fori_loop(0, R // rc, step, 0)


def _stationary_weights(j, nj, copies, lands, scrs):
    @pl.when(j == 0)
    def _():
        for c in copies(0):
            c.start()

    for c in copies(j):
        c.wait()
    for land, scr in zip(lands, scrs):
        _stage_weight(land, scr)

    @pl.when(j + 1 < nj)
    def _():
        for c in copies(j + 1):
            c.start()


def _matmul_kernel(*refs, layer, w_row, w_col, k, tn, transposed, has_res):
    a_ref, w_hbm = refs[:2]
    r_ref = refs[2] if has_res else None
    o_ref, w_land, w_scr, sem = refs[-4:]
    j = pl.program_id(0)
    nj = pl.num_programs(0)

    def window(jj):
        if transposed:
            src = w_hbm.at[layer, pl.ds(pl.multiple_of(w_col + jj * tn, 8), tn), pl.ds(w_row, k)]
        else:
            src = w_hbm.at[layer, pl.ds(w_row, k), pl.ds(pl.multiple_of(w_col + jj * tn, LANES), tn)]
        return pltpu.make_async_copy(src, w_land.at[0], sem)

    @pl.when(pl.program_id(1) == 0)
    def _():
        _stationary_weights(j, nj, lambda jj: [window(jj)], (w_land,), (w_scr,))

    if transposed:
        d = _dot_nt(a_ref[...], w_scr[...])
    else:
        d = jnp.dot(a_ref[...], w_scr[...], preferred_element_type=F32)
    if has_res:
        d = d + r_ref[...]
    o_ref[...] = d.astype(o_ref.dtype)


def matmul(a, w, layer, out_dtype, *, a_col=0, k=None, w_row=0, w_col=0, n=None, transposed=False, res=None):
    T = a.shape[0]
    k = a.shape[1] if k is None else k
    n_total = w.shape[1] if transposed else w.shape[2]
    n = n_total - w_col if n is None else n
    tn = _divisor(n, (512, 256, 128))
    assert a_col % k == 0 and w_col % (8 if transposed else LANES) == 0
    wshape = (tn, k) if transposed else (k, tn)
    res_b = (lambda tm: _nbytes((tm, tn), F32)) if res is not None else (lambda tm: 0)

    def blk(tm):
        return _nbytes((tm, k), BF16) + _nbytes((tm, tn), out_dtype) + res_b(tm)

    def scr(tm):
        return _nbytes(wshape, F32) + _nbytes(wshape, BF16) + 2 * _nbytes((tm, tn), F32)

    tm = _row_tile(T, blk, scr)
    ab = a_col // k
    in_specs = [pl.BlockSpec((tm, k), lambda j, i: (i, ab)), pl.BlockSpec(memory_space=pl.ANY)]
    args = [a, w]
    if res is not None:
        in_specs.append(pl.BlockSpec((tm, tn), lambda j, i: (i, j)))
        args.append(res)
    return pl.pallas_call(
        functools.partial(_matmul_kernel, layer=layer, w_row=w_row, w_col=w_col, k=k, tn=tn, transposed=transposed,
                          has_res=res is not None),
        grid=(n // tn, T // tm),
        in_specs=in_specs,
        out_specs=pl.BlockSpec((tm, tn), lambda j, i: (i, j)),
        out_shape=jax.ShapeDtypeStruct((T, n), out_dtype),
        scratch_shapes=[pltpu.VMEM((1,) + wshape, F32), pltpu.VMEM(wshape, BF16), pltpu.SemaphoreType.DMA(())],
        compiler_params=_params(("arbitrary", "arbitrary"), blk(tm), scr(tm)),
        name="matmul",
    )(*args)


def matmul_residual(a, w, layer, res):
    K = a.shape[1]
    nk = next(n for n in range(1, K // LANES + 1) if K % (n * LANES) == 0 and K // n <= 6144)
    k = K // nk
    out = res
    for p in range(nk):
        out = matmul(a, w, layer, F32, a_col=p * k, k=k, w_row=p * k, res=out)
    return out


def _swiglu_up_kernel(x_ref, wg_hbm, wu_hbm, o_ref, wg_land, wu_land, wg_scr, wu_scr, sems, *, layer, tn):
    j = pl.program_id(0)

    def copies(jj):
        cols = pl.ds(pl.multiple_of(jj * tn, LANES), tn)
        return [pltpu.make_async_copy(w.at[layer, :, cols], land.at[0], sems.at[n])
                for n, (w, land) in enumerate(((wg_hbm, wg_land), (wu_hbm, wu_land)))]

    @pl.when(pl.program_id(1) == 0)
    def _():
        _stationary_weights(j, pl.num_programs(0), copies, (wg_land, wu_land), (wg_scr, wu_scr))

    x = x_ref[...]
    g = jnp.dot(x, wg_scr[...], preferred_element_type=F32)
    u = jnp.dot(x, wu_scr[...], preferred_element_type=F32)
    o_ref[...] = (g * _sigmoid(g) * u).astype(o_ref.dtype)


def swiglu_up(x, wg, wu, layer):
    T, K = x.shape
    F = wg.shape[2]
    tn = _divisor(F, (512, 256, 128))

    def blk(tm):
        return _nbytes((tm, K), BF16) + _nbytes((tm, tn), BF16)

    def scr(tm):
        return 2 * _nbytes((K, tn), F32) + 2 * _nbytes((K, tn), BF16) + 3 * _nbytes((tm, tn), F32)

    tm = _row_tile(T, blk, scr)
    hbm = pl.BlockSpec(memory_space=pl.ANY)
    return pl.pallas_call(
        functools.partial(_swiglu_up_kernel, layer=layer, tn=tn),
        grid=(F // tn, T // tm),
        in_specs=[pl.BlockSpec((tm, K), lambda j, i: (i, 0)), hbm, hbm],
        out_specs=pl.BlockSpec((tm, tn), lambda j, i: (i, j)),
        out_shape=jax.ShapeDtypeStruct((T, F), BF16),
        scratch_shapes=[pltpu.VMEM((1, K, tn), F32), pltpu.VMEM((1, K, tn), F32),
                        pltpu.VMEM((K, tn), BF16), pltpu.VMEM((K, tn), BF16), pltpu.SemaphoreType.DMA((2,))],
        compiler_params=_params(("arbitrary", "arbitrary"), blk(tm), scr(tm)),
        name="swiglu_up",
    )(x, wg, wu)


def _split_bf16_f32(x):
    hi = x.astype(BF16)
    return hi, x - hi.astype(F32)


def _mlstm_kernel(gb_ref, q_ref, k_ref, v_ref, og_ref, ig_ref, fg_ref, cwq_ref, cwk_ref, an_ref, y_ref,
                  c_scr, n_scr, m_scr, pq_scr, pk_scr, *, n_heads, n_chunks, group):
    hg = pl.program_id(1)
    L = CHUNK
    c_scr[...] = jnp.zeros_like(c_scr)
    n_scr[...] = jnp.zeros_like(n_scr)
    m_scr[...] = jnp.zeros_like(m_scr)
    pq_scr[...] = jnp.zeros_like(pq_scr)
    pk_scr[...] = jnp.zeros_like(pk_scr)
    ri = lax.broadcasted_iota(jnp.int32, (L, L), 0)
    ci = lax.broadcasted_iota(jnp.int32, (L, L), 1)
    lower = ci <= ri
    strict_lower01 = jnp.where(ri > ci, 1.0, 0.0).astype(BF16)

    def conv_silu(x, prev, w):
        y = x * w[CONV_W - 1:CONV_W, :]
        for s in range(1, CONV_W):
            shifted = jnp.where(ri < s, pltpu.roll(prev, s, 0), pltpu.roll(x, s, 0))
            y = y + shifted * w[CONV_W - 1 - s:CONV_W - s, :]
        return y * _sigmoid(y)

    def head_chunk(c, rows, g):
        cols = slice(g * HEAD_DIM, (g + 1) * HEAD_DIM)
        h = hg * group + g
        xq = q_ref[rows, cols].astype(F32)
        xk = k_ref[rows, cols].astype(F32)
        q = conv_silu(xq, pq_scr[g], cwq_ref[:, cols])
        k = conv_silu(xk, pk_scr[g], cwk_ref[:, cols]) * (HEAD_DIM ** -0.5)
        pq_scr[g] = xq
        pk_scr[g] = xk
        v = v_ref[rows, cols]
        q_bf = q.astype(BF16)
        k_bf = k.astype(BF16)
        yield

        i_row = ig_ref[0, g, pl.ds(c, 1), :] + gb_ref[h]
        f_row = fg_ref[0, g, pl.ds(c, 1), :] + gb_ref[n_heads + h]
        lf = jnp.minimum(f_row, 0.0) - jnp.log(1.0 + jnp.exp(-jnp.abs(f_row)))
        lf_low = jnp.where(lower, lf, 0.0)
        b_col = jnp.sum(lf_low, axis=1, keepdims=True)
        hi, rest = _split_bf16_f32(lf_low)
        mid, lo = _split_bf16_f32(rest)
        yield
        dmat = (jnp.dot(hi, strict_lower01, preferred_element_type=F32)
                + jnp.dot(mid, strict_lower01, preferred_element_type=F32)
                + jnp.dot(lo.astype(BF16), strict_lower01, preferred_element_type=F32))
        g_tot = b_col[L - 1:L, :]
        a_row = dmat[L - 1:L, :] + i_row
        a_col = jnp.sum(jnp.where(ci > ri, lf, 0.0) + jnp.where(ci == ri, i_row, 0.0),
                        axis=1, keepdims=True)
        yield

        m_prev = m_scr[g]
        c_prev = c_scr[g]
        n_prev = n_scr[g]

        log_d = jnp.where(lower, dmat + i_row, -jnp.inf)
        log_inter = b_col + m_prev
        m_t = jnp.maximum(log_inter, jnp.max(log_d, axis=1, keepdims=True))
        qk = _dot_nt(q_bf, k_bf)
        yield
        w = jnp.exp(log_d - m_t) * qk
        e_inter = jnp.exp(log_inter - m_t)
        yield
        num = (e_inter * jnp.dot(q_bf, c_prev.astype(BF16), preferred_element_type=F32)
               + jnp.dot(w.astype(BF16), v, preferred_element_type=F32))
        den = e_inter * jnp.sum(q * n_prev, axis=1, keepdims=True) + jnp.sum(w, axis=1, keepdims=True)
        yield
        hcell = num / jnp.maximum(jnp.abs(den), jnp.exp(-m_t))

        gated = _sigmoid(og_ref[rows, cols].astype(F32)) * hcell
        y = gated * lax.rsqrt(jnp.mean(gated * gated, axis=-1, keepdims=True) + EPS) * an_ref[:, cols]
        y_ref[rows, cols] = y.astype(y_ref.dtype)
        yield

        m_new =jnp.maximum(g_tot + m_prev, jnp.max(a_row, axis=1, keepdims=True))
        w_col = jnp.exp(a_col - m_new)
        decay = jnp.exp(g_tot + m_prev - m_new)
        c_scr[g] = decay * c_prev + _dot_tn(k_bf, (v.astype(F32) * w_col).astype(BF16))
        n_scr[g] = decay * n_prev + jnp.sum(k * w_col, axis=0, keepdims=True)
        m_scr[g] = m_new

    def chunk(c, carry):
        rows = pl.ds(pl.multiple_of(c * L, L), L)
        stages = [head_chunk(c, rows, g) for g in range(group)]
        for _ in itertools.zip_longest(*stages):
            pass
        return carry

    lax.fori_loop(0, n_chunks, chunk, 0)


def mlstm_branch(z, zg, a_gate_bias, a_conv, a_norm, batch, seq, n_heads):
    T = batch * seq
    nc = seq // CHUNK
    W = n_heads * HEAD_DIM
    group = _divisor(n_heads, (2, 1))
    ng = n_heads // group
    gw = group * HEAD_DIM

    def gate_rows(g):
        return jnp.transpose(g.reshape(batch, seq, n_heads), (0, 2, 1)).reshape(batch, n_heads, nc, CHUNK)

    ig = gate_rows(zg[:, :n_heads])
    fg = gate_rows(zg[:, n_heads:2 * n_heads])

    def zspec(part):
        return pl.BlockSpec((seq, gw), lambda b, h, gb: (b, part * ng + h))

    gspec = pl.BlockSpec((1, group, nc, CHUNK), lambda b, h, gb: (b, h, 0, 0))
    blk = 5 * _nbytes((seq, gw), BF16) + 2 * _nbytes((group, nc, CHUNK), F32)
    grid_spec = pltpu.PrefetchScalarGridSpec(
        num_scalar_prefetch=1,
        grid=(batch, ng),
        in_specs=[zspec(0), zspec(1), zspec(2), zspec(3), gspec, gspec,
                  pl.BlockSpec((CONV_W, gw), lambda b, h, gb: (0, h)),
                  pl.BlockSpec((CONV_W, gw), lambda b, h, gb: (0, ng + h)),
                  pl.BlockSpec((1, gw), lambda b, h, gb: (0, h))],
        out_specs=pl.BlockSpec((seq, gw), lambda b, h, gb: (b, h)),
        scratch_shapes=[pltpu.VMEM((group, HEAD_DIM, HEAD_DIM), F32), pltpu.VMEM((group, 1, HEAD_DIM), F32),
                        pltpu.VMEM((group, 1, 1), F32), pltpu.VMEM((group, CHUNK, HEAD_DIM), F32),
                        pltpu.VMEM((group, CHUNK, HEAD_DIM), F32)],
    )
    return pl.pallas_call(
        functools.partial(_mlstm_kernel, n_heads=n_heads, n_chunks=nc, group=group),
        grid_spec=grid_spec,
        out_shape=jax.ShapeDtypeStruct((T, W), BF16),
        compiler_params=_params(("parallel", "parallel"), blk, 64 * group * _nbytes((CHUNK, HEAD_DIM), F32)),
        name="mlstm",
    )(a_gate_bias.astype(F32), z, z, z, z, ig, fg, a_conv.astype(F32), a_conv.astype(F32),
      a_norm.reshape(1, W).astype(F32))


def _sgu_kernel(u_ref, v_ref, lng_ref, lnb_ref, ws_ref, bst_ref, y_ref, *, n_groups, chunks_per_block):
    L = CHUNK
    c0 = math.sqrt(2.0 / math.pi)

    def gelu(x):
        return x * (0.5 * (1.0 + jnp.tanh(c0 * (x + 0.044715 * (x * x * x)))))

    v = gelu(v_ref[...].astype(F32))
    mu = jnp.mean(v, axis=-1, keepdims=True)
    vc = v - mu
    vn = vc * lax.rsqrt(jnp.mean(vc * vc, axis=-1, keepdims=True) + EPS) * lng_ref[...] + lnb_ref[...]
    vn = vn.astype(BF16)
    ri = lax.broadcasted_iota(jnp.int32, (L, L), 0)
    ci = lax.broadcasted_iota(jnp.int32, (L, L), 1)
    bst = bst_ref[...]
    for g in range(n_groups):
        cs = slice(g * HEAD_DIM, (g + 1) * HEAD_DIM)
        w = jnp.where(ci <= ri, ws_ref[g], 0.0).astype(BF16)
        bias = bst[:, g:g + 1]
        for c in range(chunks_per_block):
            rs = slice(c * L, (c + 1) * L)
            s = jnp.dot(w, vn[rs, cs], preferred_element_type=F32) + bias
            y_ref[rs, cs] = (gelu(u_ref[rs, cs].astype(F32)) * s).astype(y_ref.dtype)


def sgu_branch(z, b_ln_g, b_ln_b, b_ws, b_bs, n_groups):
    T = z.shape[0]
    W = n_groups * HEAD_DIM
    cpb = _divisor(T // CHUNK, (4, 2, 1))
    R = cpb * CHUNK
    blk = 3 * _nbytes((R, W), BF16) + _nbytes((n_groups, CHUNK, CHUNK), F32)
    return pl.pallas_call(
        functools.partial(_sgu_kernel, n_groups=n_groups, chunks_per_block=cpb),
        grid=(T // R,),
        in_specs=[pl.BlockSpec((R, W), lambda i: (i, 0)),
                  pl.BlockSpec((R, W), lambda i: (i, 1)),
                  pl.BlockSpec((1, W), lambda i: (0, 0)),
                  pl.BlockSpec((1, W), lambda i: (0, 0)),
                  pl.BlockSpec((n_groups, CHUNK, CHUNK), lambda i: (0, 0, 0)),
                  pl.BlockSpec((CHUNK, n_groups), lambda i: (0, 0))],
        out_specs=pl.BlockSpec((R, W), lambda i: (i, 0)),
        out_shape=jax.ShapeDtypeStruct((T, W), BF16),
        compiler_params=_params(("parallel",), blk, 6 * _nbytes((R, W), F32)),
        name="spatial_gating",
    )(z, z, b_ln_g.reshape(1, W).astype(F32), b_ln_b.reshape(1, W).astype(F32), b_ws.astype(F32),
      jnp.transpose(b_bs).astype(F32))


def _t5_bucket_table(n):
    d = np.arange(n, dtype=np.int64)
    max_exact = N_BUCKETS // 2
    nf = np.maximum(d, 1).astype(np.float32)
    scaled = (np.log(nf / np.float32(max_exact)) / np.float32(math.log(MAX_DIST / max_exact))
              * np.float32(N_BUCKETS - max_exact))
    large = np.minimum(max_exact + scaled.astype(np.int32), N_BUCKETS - 1)
    return np.where(d < max_exact, d, large).astype(np.int32)


def _bias_kernel(rb_ref, bkt_ref, o_ref, *, n_heads):
    h = pl.program_id(0)
    bkt = bkt_ref[...]
    far = rb_ref[(N_BUCKETS - 1) * n_heads + h]
    out = jnp.full(bkt.shape, -jnp.inf, F32)
    for b in range(N_BUCKETS):
        out = jnp.where(bkt == b, rb_ref[b * n_heads + h] - far, out)
    o_ref[0] = out


def attn_bias_tiles(rel_bias, tq):
    n_heads = rel_bias.shape[1]
    assert tq >= MAX_DIST
    c = np.arange(2 * tq)[:, None]
    r = np.arange(tq)[None, :]
    dist = r - c + tq
    table = _t5_bucket_table(2 * tq)
    bkt = np.where(dist >= 0, table[np.maximum(dist, 0)], -1).astype(np.int32)
    grid_spec = pltpu.PrefetchScalarGridSpec(
        num_scalar_prefetch=1,
        grid=(n_heads,),
        in_specs=[pl.BlockSpec((2 * tq, tq), lambda h, rb: (0, 0))],
        out_specs=pl.BlockSpec((1, 2 * tq, tq), lambda h, rb: (h, 0, 0)),
    )
    return pl.pallas_call(
        functools.partial(_bias_kernel, n_heads=n_heads),
        grid_spec=grid_spec,
        out_shape=jax.ShapeDtypeStruct((n_heads, 2 * tq, tq), F32),
        compiler_params=_params(("arbitrary",), 2 * _nbytes((2 * tq, tq), F32)),
        name="attn_bias_tiles",
    )(rel_bias.astype(F32).reshape(-1), jnp.asarray(bkt))


ONES_ROWS = 16
ATTN_FAR_KEYS = 1024


def _attn_kernel(q_ref, k_ref, v_ref, nb_ref, lam_ref, cn_ref, y_ref,
                 vt_scr, m1_scr, a1_scr, m2_scr, a2_scr, *, tq, seq, lam_init):
    qi = pl.program_id(2)
    dk = HEAD_DIM // 2
    dv = HEAD_DIM
    scale = dk ** -0.5

    @pl.when(qi == 0)
    def _():
        for c in range(seq // tq):
            cs = slice(c * tq, (c + 1) * tq)
            vt_scr[:dv, cs] = v_ref[cs, :].astype(F32).T.astype(BF16)
        vt_scr[dv:, :] = jnp.ones((ONES_ROWS, seq), BF16)

    q = (q_ref[...].astype(F32) * scale).astype(BF16)
    lane = lax.broadcasted_iota(jnp.int32, q.shape, 1)
    zero = jnp.zeros_like(q)
    q1 = jnp.where(lane < dk, q, zero)
    q2 = jnp.where(lane >= dk, q, zero)
    for m_scr, a_scr in ((m1_scr, a1_scr), (m2_scr, a2_scr)):
        m_scr[...] = jnp.full(m_scr.shape, -jnp.inf, F32)
        a_scr[...] = jnp.zeros_like(a_scr)

    def update(s, vtb, m_scr, a_scr):
        m_prev = m_scr[...]
        sb = s.astype(BF16)
        m_new = jnp.maximum(m_prev, jnp.max(sb, axis=0, keepdims=True).astype(F32))
        alpha = jnp.exp(m_prev - m_new)
        p = jnp.exp(sb - m_new.astype(BF16))
        a_scr[...] = alpha * a_scr[...] + jnp.dot(vtb, p, preferred_element_type=F32)
        m_scr[...] = m_new

    def block(kstart, nk, bias):
        rows = pl.ds(pl.multiple_of(kstart, tq), nk)
        kb = k_ref[rows, :]
        vtb = vt_scr[:, rows]
        s1 = _dot_nt(kb, q1)
        s2 = _dot_nt(kb, q2)
        if bias is not None:
            s1 = s1 + bias
            s2 = s2 + bias
        update(s1, vtb, m1_scr, a1_scr)
        update(s2, vtb, m2_scr, a2_scr)

    big = max(ATTN_FAR_KEYS // tq, 1)
    n_far = jnp.maximum(qi - 1, 0)
    n_big = n_far // big

    def far_block(j, carry):
        block(j * (big * tq), big * tq, None)
        return carry

    lax.fori_loop(0, n_big, far_block, 0)
    rem = n_far - n_big * big
    unit = big // 2
    while unit >= 1:
        done = n_big * big + (rem // (2 * unit)) * (2 * unit)

        @pl.when((rem // unit) % 2 == 1)
        def _(done=done, unit=unit):
            block(done * tq, unit * tq, None)

        unit //= 2

    @pl.when(qi > 0)
    def _():
        block((qi - 1) * tq, 2 * tq, nb_ref[0])

    @pl.when(qi == 0)
    def _():
        block(0, tq, nb_ref[0, tq:, :])

    lf = lam_ref[...]
    lam = (jnp.exp(jnp.sum(lf[0:1] * lf[1:2], axis=-1, keepdims=True))
           - jnp.exp(jnp.sum(lf[2:3] * lf[3:4], axis=-1, keepdims=True)) + lam_init)
    a1 = a1_scr[...]
    a2 = a2_scr[...]
    o = a1[:dv] / a1[dv:dv + 1] - lam * (a2[:dv] / a2[dv:dv + 1])
    o = o * lax.rsqrt(jnp.mean(o * o, axis=0, keepdims=True) + EPS) * cn_ref[...] * (1.0 - lam_init)
    y_ref[...] = o.T.astype(y_ref.dtype)


def attn_branch(z, col0, near_bias, c_lambda, c_norm, batch, seq, n_heads, lam_init, tq):
    T = batch * seq
    nq = seq // tq
    W = n_heads * HEAD_DIM
    blk = (2 * _nbytes((tq, HEAD_DIM), BF16) + 2 * _nbytes((seq, HEAD_DIM), BF16)
           + _nbytes((tq, 2 * tq), F32))
    return pl.pallas_call(
        functools.partial(_attn_kernel, tq=tq, seq=seq, lam_init=lam_init),
        grid=(batch, n_heads, nq),
        in_specs=[pl.BlockSpec((tq, HEAD_DIM), lambda b, h, i: (b * nq + i, col0 + h)),
                  pl.BlockSpec((seq, HEAD_DIM), lambda b, h, i: (b, col0 + n_heads + h)),
                  pl.BlockSpec((seq, HEAD_DIM), lambda b, h, i: (b, col0 + 2 * n_heads + h)),
                  pl.BlockSpec((1, 2 * tq, tq), lambda b, h, i: (h, 0, 0)),
                  pl.BlockSpec(c_lambda.shape, lambda b, h, i: (0, 0)),
                  pl.BlockSpec((HEAD_DIM, 1), lambda b, h, i: (0, 0))],
        out_specs=pl.BlockSpec((tq, HEAD_DIM), lambda b, h, i: (b * nq + i, h)),
        out_shape=jax.ShapeDtypeStruct((T, W), BF16),
        scratch_shapes=[pltpu.VMEM((HEAD_DIM + ONES_ROWS, seq), BF16),
                        pltpu.VMEM((1, tq), F32), pltpu.VMEM((HEAD_DIM + ONES_ROWS, tq), F32),
                        pltpu.VMEM((1, tq), F32), pltpu.VMEM((HEAD_DIM + ONES_ROWS, tq), F32)],
        compiler_params=_params(("parallel", "parallel", "arbitrary"), blk,
                                _nbytes((HEAD_DIM + ONES_ROWS, seq), BF16)
                                + 8 * _nbytes((max(ATTN_FAR_KEYS, 2 * tq), tq), F32)),
        name="diff_attention",
    )(z, z, z, near_bias, c_lambda.astype(F32), c_norm.reshape(HEAD_DIM, 1).astype(F32))


def _merge_kernel(ya_ref, yb_ref, yc_ref, wa_hbm, wb_hbm, wc_hbm, g0_ref, g1_ref, g2_ref, o_ref,
                  wa_land, wb_land, wc_land, wa_scr, wb_scr, wc_scr, sems, *, layer, tn):
    j = pl.program_id(0)
    lands = (wa_land, wb_land, wc_land)

    def copies(jj):
        cols = pl.ds(pl.multiple_of(jj * tn, LANES), tn)
        return [pltpu.make_async_copy(w.at[layer, :, cols], land.at[0], sems.at[n])
                for n, (w, land) in enumerate(zip((wa_hbm, wb_hbm, wc_hbm), lands))]

    @pl.when(pl.program_id(1) == 0)
    def _():
        _stationary_weights(j, pl.num_programs(0), copies, lands, (wa_scr, wb_scr, wc_scr))

    da = jnp.dot(ya_ref[...], wa_scr[...], preferred_element_type=F32)
    db = jnp.dot(yb_ref[...], wb_scr[...], preferred_element_type=F32)
    dc = jnp.dot(yc_ref[...], wc_scr[...], preferred_element_type=F32)
    merged = (_sigmoid(g0_ref[...].astype(F32)) * da + _sigmoid(g1_ref[...].astype(F32)) * db
              + _sigmoid(g2_ref[...].astype(F32)) * dc)
    o_ref[...] = merged.astype(o_ref.dtype)


def gated_merge(y_a, y_b, y_c, w_a, w_b, w_c, layer, z, gate_col0, d_model):
    T = z.shape[0]
    goff = gate_col0 * HEAD_DIM
    tn = _divisor(math.gcd(goff, d_model), (512, 256, 128))
    ka, kb, kc = y_a.shape[1], y_b.shape[1], y_c.shape[1]
    ks = ka + kb + kc

    def blk(tm):
        return _nbytes((tm, ks), BF16) + 4 * _nbytes((tm, tn), BF16)

    def scr(tm):
        return _nbytes((ks, tn), F32) + _nbytes((ks, tn), BF16) + 4 * _nbytes((tm, tn), F32)

    tm = _row_tile(T, blk, scr)

    def gspec(j):
        base = (goff + j * d_model) // tn
        return pl.BlockSpec((tm, tn), lambda n, i: (i, base + n))

    hbm = pl.BlockSpec(memory_space=pl.ANY)
    lands = [pltpu.VMEM((1, k, tn), F32) for k in (ka, kb, kc)]
    scrs = [pltpu.VMEM((k, tn), BF16) for k in (ka, kb, kc)]
    return pl.pallas_call(
        functools.partial(_merge_kernel, layer=layer, tn=tn),
        grid=(d_model // tn, T // tm),
        in_specs=[pl.BlockSpec((tm, ka), lambda n, i: (i, 0)),
                  pl.BlockSpec((tm, kb), lambda n, i: (i, 0)),
                  pl.BlockSpec((tm, kc), lambda n, i: (i, 0)),
                  hbm, hbm, hbm, gspec(0), gspec(1), gspec(2)],
        out_specs=pl.BlockSpec((tm, tn), lambda n, i: (i, n)),
        out_shape=jax.ShapeDtypeStruct((T, d_model), BF16),
        scratch_shapes=lands + scrs + [pltpu.SemaphoreType.DMA((3,))],
        compiler_params=_params(("arbitrary", "arbitrary"), blk(tm), scr(tm)),
        name="gated_merge",
    )(y_a, y_b, y_c, w_a, w_b, w_c, z, z, z)


def _route_kernel(l_ref, idx_ref, w_ref, *, n_experts):
    logits = l_ref[...]
    lane = lax.broadcasted_iota(jnp.int32, logits.shape, 1)
    logits = jnp.where(lane < n_experts, logits, -jnp.inf)
    m1 = jnp.max(logits, axis=-1, keepdims=True)
    i1 = jnp.min(jnp.where(logits == m1, lane, LANES), axis=-1, keepdims=True)
    rest = jnp.where(lane == i1, -jnp.inf, logits)
    m2 = jnp.max(rest, axis=-1, keepdims=True)
    i2 = jnp.min(jnp.where(rest == m2, lane, LANES), axis=-1, keepdims=True)
    e = jnp.exp(m2 - m1)
    w1 = 1.0 / (1.0 + e)
    w2 = e / (1.0 + e)
    idx_ref[...] = jnp.where(lane == 0, i1, jnp.where(lane == 1, i2, 0))
    w_ref[...] = jnp.where(lane == 0, w1, jnp.where(lane == 1, w2, 0.0))


def route_top2(logits, n_experts):
    T = logits.shape[0]
    tm = _divisor(T, (1024, 512, 256, 128, 8))
    spec = pl.BlockSpec((tm, LANES), lambda i: (i, 0))
    idx, w = pl.pallas_call(
        functools.partial(_route_kernel, n_experts=n_experts),
        grid=(T // tm,),
        in_specs=[spec],
        out_specs=[spec, spec],
        out_shape=[jax.ShapeDtypeStruct((T, LANES), jnp.int32), jax.ShapeDtypeStruct((T, LANES), F32)],
        compiler_params=_params(("parallel",), 3 * _nbytes((tm, LANES), F32)),
        name="route_top2",
    )(logits)
    return idx[:, :TOP_K], w


def _moe_gather_kernel(src_ref, nv_ref, x_hbm, o_ref, buf, sem, *, tg):
    i = pl.program_id(0)
    n_valid = nv_ref[0]
    slot = i % 2

    def row_copy(tile, r, sl):
        tok = src_ref[tile * tg + r]
        return pltpu.make_async_copy(x_hbm.at[pl.ds(tok, 1), :], buf.at[sl, pl.ds(r, 1), :], sem.at[sl])

    def start_tile(tile, sl):
        def body(r, carry):
            row_copy(tile, r, sl).start()
            return carry
        lax.fori_loop(0, tg, body, 0, unroll=DMA_LOOP_UNROLL)

    @pl.when(i == 0)
    def _():
        start_tile(0, 0)

    @pl.when(i + 1 < n_valid)
    def _():
        start_tile(i + 1, 1 - slot)

    @pl.when(i < n_valid)
    def _():
        def body(r, carry):
            row_copy(i, r, slot).wait()
            return carry
        lax.fori_loop(0, tg, body, 0, unroll=DMA_LOOP_UNROLL)
        o_ref[...] = buf[slot].astype(o_ref.dtype)

    @pl.when(i >= n_valid)
    def _():
        o_ref[...] = jnp.zeros_like(o_ref)


def _weight_window(w_hbm, e, j, tn):
    return w_hbm.at[e, :, pl.ds(pl.multiple_of(j * tn, tn), tn)]


def _grouped_weights(s, sched, w_hbms, lands, scrs, sems, tn):
    ew_ref, jw_ref, fl_ref, ne_ref, nj_ref = sched

    def copies(e, j):
        return [pltpu.make_async_copy(_weight_window(w, e, j, tn), land.at[0], sems.at[k])
                for k, (w, land) in enumerate(zip(w_hbms, lands))]

    @pl.when(fl_ref[s] == 2)
    def _():
        @pl.when(s == 0)
        def _():
            for c in copies(ew_ref[0], jw_ref[0]):
                c.start()

        for c in copies(ew_ref[s], jw_ref[s]):
            c.wait()
        for land, scr in zip(lands, scrs):
            _stage_weight(land, scr)

        @pl.when(ne_ref[s] >= 0)
        def _():
            for c in copies(ne_ref[s], nj_ref[s]):
                c.start()


def _moe_up_kernel(t_ref, jo_ref, ew_ref, jw_ref, fl_ref, ne_ref, nj_ref, x_ref, wg_hbm, wu_hbm, o_ref,
                   wg_land, wu_land, wg_scr, wu_scr, sems, *, tf):
    s = pl.program_id(0)
    _grouped_weights(s, (ew_ref, jw_ref, fl_ref, ne_ref, nj_ref), (wg_hbm, wu_hbm), (wg_land, wu_land),
                     (wg_scr, wu_scr), sems, tf)

    @pl.when(fl_ref[s] > 0)
    def _():
        x = x_ref[...]
        g = jnp.dot(x, wg_scr[...], preferred_element_type=F32)
        u = jnp.dot(x, wu_scr[...], preferred_element_type=F32)
        o_ref[...] = (g * _sigmoid(g) * u).astype(o_ref.dtype)

    @pl.when(fl_ref[s] == 0)
    def _():
        o_ref[...] = jnp.zeros_like(o_ref)


def _moe_down_kernel(t_ref, jo_ref, ew_ref, jw_ref, fl_ref, ne_ref, nj_ref, a_ref, wd_hbm, o_ref,
                     wd_land, wd_scr, sems, *, tn):
    s = pl.program_id(0)
    _grouped_weights(s, (ew_ref, jw_ref, fl_ref, ne_ref, nj_ref), (wd_hbm,), (wd_land,), (wd_scr,), sems, tn)

    @pl.when(fl_ref[s] > 0)
    def _():
        o_ref[...] = jnp.dot(a_ref[...], wd_scr[...], preferred_element_type=F32)

    @pl.when(fl_ref[s] == 0)
    def _():
        o_ref[...] = jnp.zeros_like(o_ref)


def _moe_schedule(tiles_per_expert, n_tiles, n_cols):
    E = tiles_per_expert.shape[0]
    cnt = jnp.concatenate([tiles_per_expert, (n_tiles - jnp.sum(tiles_per_expert))[None]]).astype(jnp.int32)
    tile0 = jnp.cumsum(cnt) - cnt
    step_end = jnp.cumsum(cnt * n_cols)
    step0 = step_end - cnt * n_cols
    s = jnp.arange(n_tiles * n_cols, dtype=jnp.int32)
    g = jnp.sum((s[:, None] >= step_end[None, :]).astype(jnp.int32), axis=1)
    within = s - step0[g]
    c = jnp.maximum(cnt[g], 1)
    col = within // c
    tile = tile0[g] + within % c
    valid = g < E
    first = jnp.logical_and(valid, within % c == 0)
    n_valid = step_end[E - 1]
    last = jnp.maximum(n_valid - 1, 0)
    w_e = jnp.where(valid, g, g[last])
    w_j = jnp.where(valid, col, col[last])
    flag = jnp.where(first, 2, jnp.where(valid, 1, 0))
    nxt = s + c
    has_next = jnp.logical_and(first, nxt < n_valid)
    nxt = jnp.minimum(nxt, n_tiles * n_cols - 1)
    n_e = jnp.where(has_next, g[nxt], -1)
    n_j = jnp.where(has_next, col[nxt], 0)
    return [a.astype(jnp.int32) for a in (tile, col, w_e, w_j, flag, n_e, n_j)]


def _moe_combine_kernel(dest_ref, y_hbm, h_ref, w_ref, *rest, tm, final_norm):
    if final_norm:
        g_ref, o_ref, buf0, buf1, sem = rest
    else:
        o_ref, buf0, buf1, sem = rest
    i = pl.program_id(0)

    def row_copy(r, k, row):
        buf = buf0 if k == 0 else buf1
        return pltpu.make_async_copy(y_hbm.at[pl.ds(row, 1), :], buf.at[pl.ds(r, 1), :], sem.at[k])

    def start(r, carry):
        a = TOP_K * (i * tm + r)
        row_copy(r, 0, dest_ref[a]).start()
        row_copy(r, 1, dest_ref[a + 1]).start()
        return carry

    def wait(r, carry):
        row_copy(r, 0, 0).wait()
        row_copy(r, 1, 0).wait()
        return carry

    lax.fori_loop(0, tm, start, 0, unroll=DMA_LOOP_UNROLL)
    lax.fori_loop(0, tm, wait, 0, unroll=DMA_LOOP_UNROLL)
    w = w_ref[...]
    out = h_ref[...] + (w[:, 0:1] * buf0[...] + w[:, 1:2] * buf1[...])
    if final_norm:
        out = out * lax.rsqrt(jnp.mean(out * out, axis=-1, keepdims=True) + EPS) * g_ref[...]
    o_ref[...] = out


def moe_ffn(h, hn, logits, wg, wu, wd, layer, final_gain=None):
    T, D = hn.shape
    E, F = wg.shape[1], wg.shape[3]
    idx, wts = route_top2(logits, E)
    tm = _divisor(T, (256, 128))
    A = T * TOP_K
    n_tiles = A // tm + E

    e_flat = idx.reshape(-1)
    onehot = (e_flat[:, None] == jnp.arange(E, dtype=jnp.int32)[None, :]).astype(jnp.int32)
    csum = jnp.cumsum(onehot, axis=0)
    pos = jnp.take_along_axis(csum, e_flat[:, None], axis=1)[:, 0] - 1
    counts = csum[-1]
    padded = ((counts + tm - 1) // tm) * tm
    ends = jnp.cumsum(padded)
    starts = ends - padded
    dest = (starts[e_flat] + pos).astype(jnp.int32)
    src = jnp.zeros((n_tiles * tm,), jnp.int32).at[dest].set(jnp.arange(A, dtype=jnp.int32) // TOP_K)
    tiles_per_expert = padded // tm
    n_valid_tiles = jnp.sum(tiles_per_expert).astype(jnp.int32).reshape(1)

    x_sorted = pl.pallas_call(
        functools.partial(_moe_gather_kernel, tg=tm),
        grid_spec=pltpu.PrefetchScalarGridSpec(
            num_scalar_prefetch=2,
            grid=(n_tiles,),
            in_specs=[pl.BlockSpec(memory_space=pl.ANY)],
            out_specs=pl.BlockSpec((tm, D), lambda i, sr, nv: (i, 0)),
            scratch_shapes=[pltpu.VMEM((2, tm, D), F32), pltpu.SemaphoreType.DMA((2,))],
        ),
        out_shape=jax.ShapeDtypeStruct((n_tiles * tm, D), BF16),
        compiler_params=_params(("arbitrary",), _nbytes((tm, D), BF16), 3 * _nbytes((tm, D), F32)),
        name="moe_gather",
    )(src, n_valid_tiles, hn)

    def experts(w):
        return w.reshape((-1,) + w.shape[2:])

    tf = _divisor(F, (512, 256, 128))
    sched = _moe_schedule(tiles_per_expert, n_tiles, F // tf)
    sched[2] = sched[2] + layer * E
    sched[5] = jnp.where(sched[5] >= 0, sched[5] + layer * E, -1)
    hbm = pl.BlockSpec(memory_space=pl.ANY)
    act = pl.pallas_call(
        functools.partial(_moe_up_kernel, tf=tf),
        grid_spec=pltpu.PrefetchScalarGridSpec(
            num_scalar_prefetch=7,
            grid=(n_tiles * (F // tf),),
            in_specs=[pl.BlockSpec((tm, D), lambda s, t, jo, *_: (t[s], 0)), hbm, hbm],
            out_specs=pl.BlockSpec((tm, tf), lambda s, t, jo, *_: (t[s], jo[s])),
            scratch_shapes=[pltpu.VMEM((1, D, tf), F32), pltpu.VMEM((1, D, tf), F32),
                            pltpu.VMEM((D, tf), BF16), pltpu.VMEM((D, tf), BF16), pltpu.SemaphoreType.DMA((2,))],
        ),
        out_shape=jax.ShapeDtypeStruct((n_tiles * tm, F), BF16),
        compiler_params=_params(("arbitrary",), _nbytes((tm, D), BF16) + _nbytes((tm, tf), BF16),
                                2 * _nbytes((D, tf), F32) + 2 * _nbytes((D, tf), BF16) + 3 * _nbytes((tm, tf), F32)),
        name="moe_up",
    )(*sched, x_sorted, experts(wg), experts(wu))

    tn = _divisor(D, (1024, 512, 256, 128))
    sched = _moe_schedule(tiles_per_expert, n_tiles, D // tn)
    sched[2] = sched[2] + layer * E
    sched[5] = jnp.where(sched[5] >= 0, sched[5] + layer * E, -1)
    y_sorted = pl.pallas_call(
        functools.partial(_moe_down_kernel, tn=tn),
        grid_spec=pltpu.PrefetchScalarGridSpec(
            num_scalar_prefetch=7,
            grid=(n_tiles * (D // tn),),
            in_specs=[pl.BlockSpec((tm, F), lambda s, t, jo, *_: (t[s], 0)), hbm],
            out_specs=pl.BlockSpec((tm, tn), lambda s, t, jo, *_: (t[s], jo[s])),
            scratch_shapes=[pltpu.VMEM((1, F, tn), F32), pltpu.VMEM((F, tn), BF16), pltpu.SemaphoreType.DMA((1,))],
        ),
        out_shape=jax.ShapeDtypeStruct((n_tiles * tm, D), F32),
        compiler_params=_params(("arbitrary",), _nbytes((tm, F), BF16) + _nbytes((tm, tn), F32),
                                _nbytes((F, tn), F32) + _nbytes((F, tn), BF16) + _nbytes((tm, tn), F32)),
        name="moe_down",
    )(*sched, act, experts(wd))

    tc = _divisor(T, (256, 128))
    blk_c = 2 * _nbytes((tc, D), F32) + _nbytes((tc, LANES), F32)
    in_specs = [hbm, pl.BlockSpec((tc, D), lambda i, de: (i, 0)), pl.BlockSpec((tc, LANES), lambda i, de: (i, 0))]
    args = [dest, y_sorted, h, wts]
    if final_gain is not None:
        in_specs.append(pl.BlockSpec((1, D), lambda i, de: (0, 0)))
        args.append(final_gain.reshape(1, D).astype(F32))
    return pl.pallas_call(
        functools.partial(_moe_combine_kernel, tm=tc, final_norm=final_gain is not None),
        grid_spec=pltpu.PrefetchScalarGridSpec(
            num_scalar_prefetch=1,
            grid=(T // tc,),
            in_specs=in_specs,
            out_specs=pl.BlockSpec((tc, D), lambda i, de: (i, 0)),
            scratch_shapes=[pltpu.VMEM((tc, D), F32), pltpu.VMEM((tc, D), F32), pltpu.SemaphoreType.DMA((2,))],
        ),
        out_shape=jax.ShapeDtypeStruct((T, D), F32),
        compiler_params=_params(("arbitrary",), blk_c, 3 * _nbytes((tc, D), F32)),
        name="moe_combine",
    )(*args)


def _mixer(h, li, batch, seq, near_bias, tq, norm_mix, w_in, a_gate_bias, a_conv, a_norm, b_ln_g, b_ln_b, b_ws,
           b_bs, c_lambda, c_norm, w_br_a, w_br_b, w_br_c, w_out):
    d_model = h.shape[1]
    h_a = a_gate_bias.shape[1] // 2
    g_b = b_ws.shape[1]
    h_c = w_br_c.shape[1] // HEAD_DIM
    w_a = h_a * HEAD_DIM
    g0 = 4 * w_a
    g1 = g0 + 2 * h_a
    n_rest = w_in.shape[2] - g1
    w_in_t = jnp.swapaxes(w_in, 1, 2)
    n, zg = rmsnorm(h, norm_mix[li], proj=jnp.transpose(w_in_t[li, g0:g1, :]))
    z_a = matmul(n, w_in_t, li, BF16, n=g0, transposed=True)
    z_r = matmul(n, w_in_t, li, BF16, w_col=g1, n=n_rest, transposed=True)
    y_a = mlstm_branch(z_a, zg, a_gate_bias[li], a_conv[li], a_norm[li], batch, seq, h_a)
    y_b = sgu_branch(z_r, b_ln_g[li], b_ln_b[li], b_ws[li], b_bs[li], g_b)
    lam_init = 0.8 - 0.6 * math.exp(-0.3 * li)
    y_c = attn_branch(z_r, 2 * g_b, near_bias, c_lambda[li], c_norm[li], batch, seq, h_c, lam_init, tq)
    merged = gated_merge(y_a, y_b, y_c, w_br_a, w_br_b, w_br_c, li, z_r, 2 * g_b + 3 * h_c, d_model)
    return matmul_residual(merged, w_out, li, h)


def kernel(x, norm_mix, w_in, a_gate_bias, a_conv, a_norm, b_ln_g, b_ln_b, b_ws, b_bs, c_lambda, c_norm, rel_bias,
           w_br_a, w_br_b, w_br_c, w_out, norm_ffn, ffn_wg, ffn_wu, ffn_wd, router, moe_wg, moe_wu, moe_wd,
           final_norm):
    batch, seq, d_model = x.shape
    depth = w_in.shape[0]
    tq = _divisor(seq, (512, 256, 128))
    near_bias = attn_bias_tiles(rel_bias, tq)
    h = x.reshape(batch * seq, d_model).astype(F32)
    for li in range(depth):
        h = _mixer(h, li, batch, seq, near_bias, tq, norm_mix, w_in, a_gate_bias, a_conv, a_norm, b_ln_g, b_ln_b,
                   b_ws, b_bs, c_lambda, c_norm, w_br_a, w_br_b, w_br_c, w_out)
        j = li // 2
        fused_final = False
        if li % 2 == 0:
            hn = rmsnorm(h, norm_ffn[li])
            act = swiglu_up(hn, ffn_wg, ffn_wu, j)
            h = matmul_residual(act, ffn_wd, j, h)
        else:
            hn, logits = rmsnorm(h, norm_ffn[li], proj=router[j], out_dtype=F32)
            fused_final = li == depth - 1
            h = moe_ffn(h, hn, logits, moe_wg, moe_wu, moe_wd, j, final_norm if fused_final else None)
    out = h if fused_final else rmsnorm(h, final_norm, out_dtype=F32)
    return out.reshape(batch, seq, d_model).astype(x.dtype)
```

```python
import functools
import itertools
import math

import numpy as np
import jax
import jax.numpy as jnp
from jax import lax
from jax.experimental import pallas as pl
from jax.experimental.pallas import tpu as pltpu

F32 = jnp.float32
BF16 = jnp.bfloat16

EPS = 1e-6
HEAD_DIM = 128
CHUNK = 128
CONV_W = 4
N_BUCKETS = 32
MAX_DIST = 128
TOP_K = 2
LANES = 128
VMEM_BYTES_V7X = 64 * 1024 * 1024
VMEM_SLACK = 6 * 1024 * 1024
VMEM_BUDGET = VMEM_BYTES_V7X - 8 * 1024 * 1024
WEIGHT_STAGE_ROWS = 256
DMA_LOOP_UNROLL = 8


def _divisor(n, candidates):
    for c in candidates:
        if n % c == 0:
            return c
    raise ValueError(f"no tile size in {candidates} divides {n}")


def _vmem_need(block_bytes, scratch_bytes):
    return 2 * block_bytes + scratch_bytes + VMEM_SLACK


def _params(semantics, block_bytes, scratch_bytes=0):
    limit = int(min(max(_vmem_need(block_bytes, scratch_bytes), 16 * 1024 * 1024), VMEM_BYTES_V7X - 4 * 1024 * 1024))
    return pltpu.CompilerParams(dimension_semantics=semantics, vmem_limit_bytes=limit)


def _row_tile(T, block_bytes, scratch_bytes):
    for tm in (1024, 512, 256, 128):
        if T % tm == 0 and _vmem_need(block_bytes(tm), scratch_bytes(tm)) <= VMEM_BUDGET:
            return tm
    raise ValueError("no row tile fits VMEM")


def _nbytes(shape, dtype):
    return int(np.prod(shape)) * jnp.dtype(dtype).itemsize


def _sigmoid(x):
    return 1.0 / (1.0 + jnp.exp(-x))


def _split_bf16(x):
    hi = x.astype(BF16)
    lo = (x - hi.astype(F32)).astype(BF16)
    return hi, lo


def _dot_nt(a, b):
    return lax.dot_general(a, b, (((1,), (1,)), ((), ())), preferred_element_type=F32)


def _dot_tn(a, b):
    return lax.dot_general(a, b, (((0,), (0,)), ((), ())), preferred_element_type=F32)


def _norm_kernel(*refs, has_proj, out_dtype):
    if has_proj:
        x_ref, g_ref, w_ref, y_ref, p_ref = refs
    else:
        x_ref, g_ref, y_ref = refs
    x = x_ref[...]
    y = x * lax.rsqrt(jnp.mean(x * x, axis=-1, keepdims=True) + EPS) * g_ref[...]
    y_ref[...] = y.astype(out_dtype)
    if has_proj:
        y_hi, y_lo = _split_bf16(y)
        w = w_ref[...]
        both = jnp.dot(y_hi, w, preferred_element_type=F32)
        p_ref[...] = (both[:, :LANES] + both[:, LANES:]
                      + jnp.dot(y_lo, w[:, :LANES], preferred_element_type=F32))


def rmsnorm(x, g, proj=None, out_dtype=BF16):
    T, D = x.shape
    tm = _divisor(T, (512, 256, 128, 8))
    g2 = g.reshape(1, D).astype(F32)
    in_specs = [pl.BlockSpec((tm, D), lambda i: (i, 0)), pl.BlockSpec((1, D), lambda i: (0, 0))]
    out_shape = [jax.ShapeDtypeStruct((T, D), out_dtype)]
    out_specs = [pl.BlockSpec((tm, D), lambda i: (i, 0))]
    args = [x, g2]
    blk = _nbytes((tm, D), F32) + _nbytes((tm, D), out_dtype)
    if proj is not None:
        n = proj.shape[1]
        assert n <= LANES
        w = jnp.pad(proj.astype(F32), ((0, 0), (0, LANES - n)))
        in_specs.append(pl.BlockSpec((D, 2 * LANES), lambda i: (0, 0)))
        out_shape.append(jax.ShapeDtypeStruct((T, LANES), F32))
        out_specs.append(pl.BlockSpec((tm, LANES), lambda i: (i, 0)))
        args.append(jnp.concatenate(_split_bf16(w), axis=1))
        blk += _nbytes((D, 2 * LANES), BF16)
    outs = pl.pallas_call(
        functools.partial(_norm_kernel, has_proj=proj is not None, out_dtype=out_dtype),
        grid=(T // tm,),
        in_specs=in_specs,
        out_specs=out_specs,
        out_shape=out_shape,
        compiler_params=_params(("parallel",), blk, 2 * _nbytes((tm, D), F32)),
        name="rmsnorm",
    )(*args)
    return (outs[0], outs[1]) if proj is not None else outs[0]


def _stage_weight(w_ref, w_scr):
    R = w_scr.shape[0]
    rc = _divisor(R, (WEIGHT_STAGE_ROWS, LANES))

    def step(r, carry):
        rows = pl.ds(pl.multiple_of(r * rc, rc), rc)
        w_scr[rows, :] = w_ref[0, rows, :].astype(BF16)
        return carry

    lax.fori_loop(0, R // rc, step, 0)


def _stationary_weights(j, nj, copies, lands, scrs):
    @pl.when(j == 0)
    def _():
        for c in copies(0):
            c.start()

    for c in copies(j):
        c.wait()
    for land, scr in zip(lands, scrs):
        _stage_weight(land, scr)

    @pl.when(j + 1 < nj)
    def _():
        for c in copies(j + 1):
            c.start()


def _matmul_kernel(*refs, layer, w_row, w_col, k, tn, transposed, has_res):
    a_ref, w_hbm = refs[:2]
    r_ref = refs[2] if has_res else None
    o_ref, w_land, w_scr, sem = refs[-4:]
    j = pl.program_id(0)
    nj = pl.num_programs(0)

    def window(jj):
        if transposed:
            src = w_hbm.at[layer, pl.ds(pl.multiple_of(w_col + jj * tn, 8), tn), pl.ds(w_row, k)]
        else:
            src = w_hbm.at[layer, pl.ds(w_row, k), pl.ds(pl.multiple_of(w_col + jj * tn, LANES), tn)]
        return pltpu.make_async_copy(src, w_land.at[0], sem)

    @pl.when(pl.program_id(1) == 0)
    def _():
        _stationary_weights(j, nj, lambda jj: [window(jj)], (w_land,), (w_scr,))

    if transposed:
        d = _dot_nt(a_ref[...], w_scr[...])
    else:
        d = jnp.dot(a_ref[...], w_scr[...], preferred_element_type=F32)
    if has_res:
        d = d + r_ref[...]
    o_ref[...] = d.astype(o_ref.dtype)


def matmul(a, w, layer, out_dtype, *, a_col=0, k=None, w_row=0, w_col=0, n=None, transposed=False, res=None):
    T = a.shape[0]
    k = a.shape[1] if k is None else k
    n_total = w.shape[1] if transposed else w.shape[2]
    n = n_total - w_col if n is None else n
    tn = _divisor(n, (512, 256, 128))
    assert a_col % k == 0 and w_col % (8 if transposed else LANES) == 0
    wshape = (tn, k) if transposed else (k, tn)
    res_b = (lambda tm: _nbytes((tm, tn), F32)) if res is not None else (lambda tm: 0)

    def blk(tm):
        return _nbytes((tm, k), BF16) + _nbytes((tm, tn), out_dtype) + res_b(tm)

    def scr(tm):
        return _nbytes(wshape, F32) + _nbytes(wshape, BF16) + 2 * _nbytes((tm, tn), F32)

    tm = _row_tile(T, blk, scr)
    ab = a_col // k
    in_specs = [pl.BlockSpec((tm, k), lambda j, i: (i, ab)), pl.BlockSpec(memory_space=pl.ANY)]
    args = [a, w]
    if res is not None:
        in_specs.append(pl.BlockSpec((tm, tn), lambda j, i: (i, j)))
        args.append(res)
    return pl.pallas_call(
        functools.partial(_matmul_kernel, layer=layer, w_row=w_row, w_col=w_col, k=k, tn=tn, transposed=transposed,
                          has_res=res is not None),
        grid=(n // tn, T // tm),
        in_specs=in_specs,
        out_specs=pl.BlockSpec((tm, tn), lambda j, i: (i, j)),
        out_shape=jax.ShapeDtypeStruct((T, n), out_dtype),
        scratch_shapes=[pltpu.VMEM((1,) + wshape, F32), pltpu.VMEM(wshape, BF16), pltpu.SemaphoreType.DMA(())],
        compiler_params=_params(("arbitrary", "arbitrary"), blk(tm), scr(tm)),
        name="matmul",
    )(*args)


def matmul_residual(a, w, layer, res):
    K = a.shape[1]
    nk = next(n for n in range(1, K // LANES + 1) if K % (n * LANES) == 0 and K // n <= 6144)
    k = K // nk
    out = res
    for p in range(nk):
        out = matmul(a, w, layer, F32, a_col=p * k, k=k, w_row=p * k, res=out)
    return out


def _swiglu_up_kernel(x_ref, wg_hbm, wu_hbm, o_ref, wg_land, wu_land, wg_scr, wu_scr, sems, *, layer, tn):
    j = pl.program_id(0)

    def copies(jj):
        cols = pl.ds(pl.multiple_of(jj * tn, LANES), tn)
        return [pltpu.make_async_copy(w.at[layer, :, cols], land.at[0], sems.at[n])
                for n, (w, land) in enumerate(((wg_hbm, wg_land), (wu_hbm, wu_land)))]

    @pl.when(pl.program_id(1) == 0)
    def _():
        _stationary_weights(j, pl.num_programs(0), copies, (wg_land, wu_land), (wg_scr, wu_scr))

    x = x_ref[...]
    g = jnp.dot(x, wg_scr[...], preferred_element_type=F32)
    u = jnp.dot(x, wu_scr[...], preferred_element_type=F32)
    o_ref[...] = (g * _sigmoid(g) * u).astype(o_ref.dtype)


def swiglu_up(x, wg, wu, layer):
    T, K = x.shape
    F = wg.shape[2]
    tn = _divisor(F, (512, 256, 128))

    def blk(tm):
        return _nbytes((tm, K), BF16) + _nbytes((tm, tn), BF16)

    def scr(tm):
        return 2 * _nbytes((K, tn), F32) + 2 * _nbytes((K, tn), BF16) + 3 * _nbytes((tm, tn), F32)

    tm = _row_tile(T, blk, scr)
    hbm = pl.BlockSpec(memory_space=pl.ANY)
    return pl.pallas_call(
        functools.partial(_swiglu_up_kernel, layer=layer, tn=tn),
        grid=(F // tn, T // tm),
        in_specs=[pl.BlockSpec((tm, K), lambda j, i: (i, 0)), hbm, hbm],
        out_specs=pl.BlockSpec((tm, tn), lambda j, i: (i, j)),
        out_shape=jax.ShapeDtypeStruct((T, F), BF16),
        scratch_shapes=[pltpu.VMEM((1, K, tn), F32), pltpu.VMEM((1, K, tn), F32),
                        pltpu.VMEM((K, tn), BF16), pltpu.VMEM((K, tn), BF16), pltpu.SemaphoreType.DMA((2,))],
        compiler_params=_params(("arbitrary", "arbitrary"), blk(tm), scr(tm)),
        name="swiglu_up",
    )(x, wg, wu)


def _split_bf16_f32(x):
    hi = x.astype(BF16)
    return hi, x - hi.astype(F32)


def _mlstm_kernel(gb_ref, q_ref, k_ref, v_ref, og_ref, ig_ref, fg_ref, cwq_ref, cwk_ref, an_ref, y_ref,
                  c_scr, n_scr, m_scr, pq_scr, pk_scr, *, n_heads, n_chunks, group):
    hg = pl.program_id(1)
    L = CHUNK
    c_scr[...] = jnp.zeros_like(c_scr)
    n_scr[...] = jnp.zeros_like(n_scr)
    m_scr[...] = jnp.zeros_like(m_scr)
    pq_scr[...] = jnp.zeros_like(pq_scr)
    pk_scr[...] = jnp.zeros_like(pk_scr)
    ri = lax.broadcasted_iota(jnp.int32, (L, L), 0)
    ci = lax.broadcasted_iota(jnp.int32, (L, L), 1)
    lower = ci <= ri
    strict_lower01 = jnp.where(ri > ci, 1.0, 0.0).astype(BF16)

    def conv_silu(x, prev, w):
        y = x * w[CONV_W - 1:CONV_W, :]
        for s in range(1, CONV_W):
            shifted = jnp.where(ri < s, pltpu.roll(prev, s, 0), pltpu.roll(x, s, 0))
            y = y + shifted * w[CONV_W - 1 - s:CONV_W - s, :]
        return y * _sigmoid(y)

    def head_chunk(c, rows, g):
        cols = slice(g * HEAD_DIM, (g + 1) * HEAD_DIM)
        h = hg * group + g
        xq = q_ref[rows, cols].astype(F32)
        xk = k_ref[rows, cols].astype(F32)
        q = conv_silu(xq, pq_scr[g], cwq_ref[:, cols])
        k = conv_silu(xk, pk_scr[g], cwk_ref[:, cols]) * (HEAD_DIM ** -0.5)
        pq_scr[g] = xq
        pk_scr[g] = xk
        v = v_ref[rows, cols]
        q_bf = q.astype(BF16)
        k_bf = k.astype(BF16)
        yield

        i_row = ig_ref[0, g, pl.ds(c, 1), :] + gb_ref[h]
        f_row = fg_ref[0, g, pl.ds(c, 1), :] + gb_ref[n_heads + h]
        lf = jnp.minimum(f_row, 0.0) - jnp.log(1.0 + jnp.exp(-jnp.abs(f_row)))
        lf_low = jnp.where(lower, lf, 0.0)
        b_col = jnp.sum(lf_low, axis=1, keepdims=True)
        hi, rest = _split_bf16_f32(lf_low)
        mid, lo = _split_bf16_f32(rest)
        yield
        dmat = (jnp.dot(hi, strict_lower01, preferred_element_type=F32)
                + jnp.dot(mid, strict_lower01, preferred_element_type=F32)
                + jnp.dot(lo.astype(BF16), strict_lower01, preferred_element_type=F32))
        g_tot = b_col[L - 1:L, :]
        a_row = dmat[L - 1:L, :] + i_row
        a_col = jnp.sum(jnp.where(ci > ri, lf, 0.0) + jnp.where(ci == ri, i_row, 0.0),
                        axis=1, keepdims=True)
        yield

        m_prev = m_scr[g]
        c_prev = c_scr[g]
        n_prev = n_scr[g]

        log_d = jnp.where(lower, dmat + i_row, -jnp.inf)
        log_inter = b_col + m_prev
        m_t = jnp.maximum(log_inter, jnp.max(log_d, axis=1, keepdims=True))
        qk = _dot_nt(q_bf, k_bf)
        yield
        w = jnp.exp(log_d - m_t) * qk
        e_inter = jnp.exp(log_inter - m_t)
        yield
        num = (e_inter * jnp.dot(q_bf, c_prev.astype(BF16), preferred_element_type=F32)
               + jnp.dot(w.astype(BF16), v, preferred_element_type=F32))
        den = e_inter * jnp.sum(q * n_prev, axis=1, keepdims=True) + jnp.sum(w, axis=1, keepdims=True)
        yield
        hcell = num / jnp.maximum(jnp.abs(den), jnp.exp(-m_t))

        gated = _sigmoid(og_ref[rows, cols].astype(F32)) * hcell
        y = gated * lax.rsqrt(jnp.mean(gated * gated, axis=-1, keepdims=True) + EPS) * an_ref[:, cols]
        y_ref[rows, cols] = y.astype(y_ref.dtype)
        yield

        m_new =jnp.maximum(g_tot + m_prev, jnp.max(a_row, axis=1, keepdims=True))
        w_col = jnp.exp(a_col - m_new)
        decay = jnp.exp(g_tot + m_prev - m_new)
        c_scr[g] = decay * c_prev + _dot_tn(k_bf, (v.astype(F32) * w_col).astype(BF16))
        n_scr[g] = decay * n_prev + jnp.sum(k * w_col, axis=0, keepdims=True)
        m_scr[g] = m_new

    def chunk(c, carry):
        rows = pl.ds(pl.multiple_of(c * L, L), L)
        stages = [head_chunk(c, rows, g) for g in range(group)]
        for _ in itertools.zip_longest(*stages):
            pass
        return carry

    lax.fori_loop(0, n_chunks, chunk, 0)


def mlstm_branch(z, zg, a_gate_bias, a_conv, a_norm, batch, seq, n_heads):
    T = batch * seq
    nc = seq // CHUNK
    W = n_heads * HEAD_DIM
    group = _divisor(n_heads, (2, 1))
    ng = n_heads // group
    gw = group * HEAD_DIM

    def gate_rows(g):
        return jnp.transpose(g.reshape(batch, seq, n_heads), (0, 2, 1)).reshape(batch, n_heads, nc, CHUNK)

    ig = gate_rows(zg[:, :n_heads])
    fg = gate_rows(zg[:, n_heads:2 * n_heads])

    def zspec(part):
        return pl.BlockSpec((seq, gw), lambda b, h, gb: (b, part * ng + h))

    gspec = pl.BlockSpec((1, group, nc, CHUNK), lambda b, h, gb: (b, h, 0, 0))
    blk = 5 * _nbytes((seq, gw), BF16) + 2 * _nbytes((group, nc, CHUNK), F32)
    grid_spec = pltpu.PrefetchScalarGridSpec(
        num_scalar_prefetch=1,
        grid=(batch, ng),
        in_specs=[zspec(0), zspec(1), zspec(2), zspec(3), gspec, gspec,
                  pl.BlockSpec((CONV_W, gw), lambda b, h, gb: (0, h)),
                  pl.BlockSpec((CONV_W, gw), lambda b, h, gb: (0, ng + h)),
                  pl.BlockSpec((1, gw), lambda b, h, gb: (0, h))],
        out_specs=pl.BlockSpec((seq, gw), lambda b, h, gb: (b, h)),
        scratch_shapes=[pltpu.VMEM((group, HEAD_DIM, HEAD_DIM), F32), pltpu.VMEM((group, 1, HEAD_DIM), F32),
                        pltpu.VMEM((group, 1, 1), F32), pltpu.VMEM((group, CHUNK, HEAD_DIM), F32),
                        pltpu.VMEM((group, CHUNK, HEAD_DIM), F32)],
    )
    return pl.pallas_call(
        functools.partial(_mlstm_kernel, n_heads=n_heads, n_chunks=nc, group=group),
        grid_spec=grid_spec,
        out_shape=jax.ShapeDtypeStruct((T, W), BF16),
        compiler_params=_params(("parallel", "parallel"), blk, 64 * group * _nbytes((CHUNK, HEAD_DIM), F32)),
        name="mlstm",
    )(a_gate_bias.astype(F32), z, z, z, z, ig, fg, a_conv.astype(F32), a_conv.astype(F32),
      a_norm.reshape(1, W).astype(F32))


def _sgu_kernel(u_ref, v_ref, lng_ref, lnb_ref, ws_ref, bst_ref, y_ref, *, n_groups, chunks_per_block):
    L = CHUNK
    c0 = math.sqrt(2.0 / math.pi)

    def gelu(x):
        return x * (0.5 * (1.0 + jnp.tanh(c0 * (x + 0.044715 * (x * x * x)))))

    v = gelu(v_ref[...].astype(F32))
    mu = jnp.mean(v, axis=-1, keepdims=True)
    vc = v - mu
    vn = vc * lax.rsqrt(jnp.mean(vc * vc, axis=-1, keepdims=True) + EPS) * lng_ref[...] + lnb_ref[...]
    vn = vn.astype(BF16)
    ri = lax.broadcasted_iota(jnp.int32, (L, L), 0)
    ci = lax.broadcasted_iota(jnp.int32, (L, L), 1)
    bst = bst_ref[...]
    for g in range(n_groups):
        cs = slice(g * HEAD_DIM, (g + 1) * HEAD_DIM)
        w = jnp.where(ci <= ri, ws_ref[g], 0.0).astype(BF16)
        bias = bst[:, g:g + 1]
        for c in range(chunks_per_block):
            rs = slice(c * L, (c + 1) * L)
            s = jnp.dot(w, vn[rs, cs], preferred_element_type=F32) + bias
            y_ref[rs, cs] = (gelu(u_ref[rs, cs].astype(F32)) * s).astype(y_ref.dtype)


def sgu_branch(z, b_ln_g, b_ln_b, b_ws, b_bs, n_groups):
    T = z.shape[0]
    W = n_groups * HEAD_DIM
    cpb = _divisor(T // CHUNK, (4, 2, 1))
    R = cpb * CHUNK
    blk = 3 * _nbytes((R, W), BF16) + _nbytes((n_groups, CHUNK, CHUNK), F32)
    return pl.pallas_call(
        functools.partial(_sgu_kernel, n_groups=n_groups, chunks_per_block=cpb),
        grid=(T // R,),
        in_specs=[pl.BlockSpec((R, W), lambda i: (i, 0)),
                  pl.BlockSpec((R, W), lambda i: (i, 1)),
                  pl.BlockSpec((1, W), lambda i: (0, 0)),
                  pl.BlockSpec((1, W), lambda i: (0, 0)),
                  pl.BlockSpec((n_groups, CHUNK, CHUNK), lambda i: (0, 0, 0)),
                  pl.BlockSpec((CHUNK, n_groups), lambda i: (0, 0))],
        out_specs=pl.BlockSpec((R, W), lambda i: (i, 0)),
        out_shape=jax.ShapeDtypeStruct((T, W), BF16),
        compiler_params=_params(("parallel",), blk, 6 * _nbytes((R, W), F32)),
        name="spatial_gating",
    )(z, z, b_ln_g.reshape(1, W).astype(F32), b_ln_b.reshape(1, W).astype(F32), b_ws.astype(F32),
      jnp.transpose(b_bs).astype(F32))


def _t5_bucket_table(n):
    d = np.arange(n, dtype=np.int64)
    max_exact = N_BUCKETS // 2
    nf = np.maximum(d, 1).astype(np.float32)
    scaled = (np.log(nf / np.float32(max_exact)) / np.float32(math.log(MAX_DIST / max_exact))
              * np.float32(N_BUCKETS - max_exact))
    large = np.minimum(max_exact + scaled.astype(np.int32), N_BUCKETS - 1)
    return np.where(d < max_exact, d, large).astype(np.int32)


def _bias_kernel(rb_ref, bkt_ref, o_ref, *, n_heads):
    h = pl.program_id(0)
    bkt = bkt_ref[...]
    far = rb_ref[(N_BUCKETS - 1) * n_heads + h]
    out = jnp.full(bkt.shape, -jnp.inf, F32)
    for b in range(N_BUCKETS):
        out = jnp.where(bkt == b, rb_ref[b * n_heads + h] - far, out)
    o_ref[0] = out


def attn_bias_tiles(rel_bias, tq):
    n_heads = rel_bias.shape[1]
    assert tq >= MAX_DIST
    c = np.arange(2 * tq)[:, None]
    r = np.arange(tq)[None, :]
    dist = r - c + tq
    table = _t5_bucket_table(2 * tq)
    bkt = np.where(dist >= 0, table[np.maximum(dist, 0)], -1).astype(np.int32)
    grid_spec = pltpu.PrefetchScalarGridSpec(
        num_scalar_prefetch=1,
        grid=(n_heads,),
        in_specs=[pl.BlockSpec((2 * tq, tq), lambda h, rb: (0, 0))],
        out_specs=pl.BlockSpec((1, 2 * tq, tq), lambda h, rb: (h, 0, 0)),
    )
    return pl.pallas_call(
        functools.partial(_bias_kernel, n_heads=n_heads),
        grid_spec=grid_spec,
        out_shape=jax.ShapeDtypeStruct((n_heads, 2 * tq, tq), F32),
        compiler_params=_params(("arbitrary",), 2 * _nbytes((2 * tq, tq), F32)),
        name="attn_bias_tiles",
    )(rel_bias.astype(F32).reshape(-1), jnp.asarray(bkt))


ONES_ROWS = 16
ATTN_FAR_KEYS = 1024


def _attn_kernel(q_ref, k_ref, v_ref, nb_ref, lam_ref, cn_ref, y_ref,
                 vt_scr, m1_scr, a1_scr, m2_scr, a2_scr, *, tq, seq, lam_init):
    qi = pl.program_id(2)
    dk = HEAD_DIM // 2
    dv = HEAD_DIM
    scale = dk ** -0.5

    @pl.when(qi == 0)
    def _():
        for c in range(seq // tq):
            cs = slice(c * tq, (c + 1) * tq)
            vt_scr[:dv, cs] = v_ref[cs, :].astype(F32).T.astype(BF16)
        vt_scr[dv:, :] = jnp.ones((ONES_ROWS, seq), BF16)

    q = (q_ref[...].astype(F32) * scale).astype(BF16)
    lane = lax.broadcasted_iota(jnp.int32, q.shape, 1)
    zero = jnp.zeros_like(q)
    q1 = jnp.where(lane < dk, q, zero)
    q2 = jnp.where(lane >= dk, q, zero)
    for m_scr, a_scr in ((m1_scr, a1_scr), (m2_scr, a2_scr)):
        m_scr[...] = jnp.full(m_scr.shape, -jnp.inf, F32)
        a_scr[...] = jnp.zeros_like(a_scr)

    def update(s, vtb, m_scr, a_scr):
        m_prev = m_scr[...]
        sb = s.astype(BF16)
        m_new = jnp.maximum(m_prev, jnp.max(sb, axis=0, keepdims=True).astype(F32))
        alpha = jnp.exp(m_prev - m_new)
        p = jnp.exp(sb - m_new.astype(BF16))
        a_scr[...] = alpha * a_scr[...] + jnp.dot(vtb, p, preferred_element_type=F32)
        m_scr[...] = m_new

    def block(kstart, nk, bias):
        rows = pl.ds(pl.multiple_of(kstart, tq), nk)
        kb = k_ref[rows, :]
        vtb = vt_scr[:, rows]
        s1 = _dot_nt(kb, q1)
        s2 = _dot_nt(kb, q2)
        if bias is not None:
            s1 = s1 + bias
            s2 = s2 + bias
        update(s1, vtb, m1_scr, a1_scr)
        update(s2, vtb, m2_scr, a2_scr)

    big = max(ATTN_FAR_KEYS // tq, 1)
    n_far = jnp.maximum(qi - 1, 0)
    n_big = n_far // big

    def far_block(j, carry):
        block(j * (big * tq), big * tq, None)
        return carry

    lax.fori_loop(0, n_big, far_block, 0)
    rem = n_far - n_big * big
    unit = big // 2
    while unit >= 1:
        done = n_big * big + (rem // (2 * unit)) * (2 * unit)

        @pl.when((rem // unit) % 2 == 1)
        def _(done=done, unit=unit):
            block(done * tq, unit * tq, None)

        unit //= 2

    @pl.when(qi > 0)
    def _():
        block((qi - 1) * tq, 2 * tq, nb_ref[0])

    @pl.when(qi == 0)
    def _():
        block(0, tq, nb_ref[0, tq:, :])

    lf = lam_ref[...]
    lam = (jnp.exp(jnp.sum(lf[0:1] * lf[1:2], axis=-1, keepdims=True))
           - jnp.exp(jnp.sum(lf[2:3] * lf[3:4], axis=-1, keepdims=True)) + lam_init)
    a1 = a1_scr[...]
    a2 = a2_scr[...]
    o = a1[:dv] / a1[dv:dv + 1] - lam * (a2[:dv] / a2[dv:dv + 1])
    o = o * lax.rsqrt(jnp.mean(o * o, axis=0, keepdims=True) + EPS) * cn_ref[...] * (1.0 - lam_init)
    y_ref[...] = o.T.astype(y_ref.dtype)


def attn_branch(z, col0, near_bias, c_lambda, c_norm, batch, seq, n_heads, lam_init, tq):
    T = batch * seq
    nq = seq // tq
    W = n_heads * HEAD_DIM
    blk = (2 * _nbytes((tq, HEAD_DIM), BF16) + 2 * _nbytes((seq, HEAD_DIM), BF16)
           + _nbytes((tq, 2 * tq), F32))
    return pl.pallas_call(
        functools.partial(_attn_kernel, tq=tq, seq=seq, lam_init=lam_init),
        grid=(batch, n_heads, nq),
        in_specs=[pl.BlockSpec((tq, HEAD_DIM), lambda b, h, i: (b * nq + i, col0 + h)),
                  pl.BlockSpec((seq, HEAD_DIM), lambda b, h, i: (b, col0 + n_heads + h)),
                  pl.BlockSpec((seq, HEAD_DIM), lambda b, h, i: (b, col0 + 2 * n_heads + h)),
                  pl.BlockSpec((1, 2 * tq, tq), lambda b, h, i: (h, 0, 0)),
                  pl.BlockSpec(c_lambda.shape, lambda b, h, i: (0, 0)),
                  pl.BlockSpec((HEAD_DIM, 1), lambda b, h, i: (0, 0))],
        out_specs=pl.BlockSpec((tq, HEAD_DIM), lambda b, h, i: (b * nq + i, h)),
        out_shape=jax.ShapeDtypeStruct((T, W), BF16),
        scratch_shapes=[pltpu.VMEM((HEAD_DIM + ONES_ROWS, seq), BF16),
                        pltpu.VMEM((1, tq), F32), pltpu.VMEM((HEAD_DIM + ONES_ROWS, tq), F32),
                        pltpu.VMEM((1, tq), F32), pltpu.VMEM((HEAD_DIM + ONES_ROWS, tq), F32)],
        compiler_params=_params(("parallel", "parallel", "arbitrary"), blk,
                                _nbytes((HEAD_DIM + ONES_ROWS, seq), BF16)
                                + 8 * _nbytes((max(ATTN_FAR_KEYS, 2 * tq), tq), F32)),
        name="diff_attention",
    )(z, z, z, near_bias, c_lambda.astype(F32), c_norm.reshape(HEAD_DIM, 1).astype(F32))


def _merge_kernel(ya_ref, yb_ref, yc_ref, wa_hbm, wb_hbm, wc_hbm, g0_ref, g1_ref, g2_ref, o_ref,
                  wa_land, wb_land, wc_land, wa_scr, wb_scr, wc_scr, sems, *, layer, tn):
    j = pl.program_id(0)
    lands = (wa_land, wb_land, wc_land)

    def copies(jj):
        cols = pl.ds(pl.multiple_of(jj * tn, LANES), tn)
        return [pltpu.make_async_copy(w.at[layer, :, cols], land.at[0], sems.at[n])
                for n, (w, land) in enumerate(zip((wa_hbm, wb_hbm, wc_hbm), lands))]

    @pl.when(pl.program_id(1) == 0)
    def _():
        _stationary_weights(j, pl.num_programs(0), copies, lands, (wa_scr, wb_scr, wc_scr))

    da = jnp.dot(ya_ref[...], wa_scr[...], preferred_element_type=F32)
    db = jnp.dot(yb_ref[...], wb_scr[...], preferred_element_type=F32)
    dc = jnp.dot(yc_ref[...], wc_scr[...], preferred_element_type=F32)
    merged = (_sigmoid(g0_ref[...].astype(F32)) * da + _sigmoid(g1_ref[...].astype(F32)) * db
              + _sigmoid(g2_ref[...].astype(F32)) * dc)
    o_ref[...] = merged.astype(o_ref.dtype)


def gated_merge(y_a, y_b, y_c, w_a, w_b, w_c, layer, z, gate_col0, d_model):
    T = z.shape[0]
    goff = gate_col0 * HEAD_DIM
    tn = _divisor(math.gcd(goff, d_model), (512, 256, 128))
    ka, kb, kc = y_a.shape[1], y_b.shape[1], y_c.shape[1]
    ks = ka + kb + kc

    def blk(tm):
        return _nbytes((tm, ks), BF16) + 4 * _nbytes((tm, tn), BF16)

    def scr(tm):
        return _nbytes((ks, tn), F32) + _nbytes((ks, tn), BF16) + 4 * _nbytes((tm, tn), F32)

    tm = _row_tile(T, blk, scr)

    def gspec(j):
        base = (goff + j * d_model) // tn
        return pl.BlockSpec((tm, tn), lambda n, i: (i, base + n))

    hbm = pl.BlockSpec(memory_space=pl.ANY)
    lands = [pltpu.VMEM((1, k, tn), F32) for k in (ka, kb, kc)]
    scrs = [pltpu.VMEM((k, tn), BF16) for k in (ka, kb, kc)]
    return pl.pallas_call(
        functools.partial(_merge_kernel, layer=layer, tn=tn),
        grid=(d_model // tn, T // tm),
        in_specs=[pl.BlockSpec((tm, ka), lambda n, i: (i, 0)),
                  pl.BlockSpec((tm, kb), lambda n, i: (i, 0)),
                  pl.BlockSpec((tm, kc), lambda n, i: (i, 0)),
                  hbm, hbm, hbm, gspec(0), gspec(1), gspec(2)],
        out_specs=pl.BlockSpec((tm, tn), lambda n, i: (i, n)),
        out_shape=jax.ShapeDtypeStruct((T, d_model), BF16),
        scratch_shapes=lands + scrs + [pltpu.SemaphoreType.DMA((3,))],
        compiler_params=_params(("arbitrary", "arbitrary"), blk(tm), scr(tm)),
        name="gated_merge",
    )(y_a, y_b, y_c, w_a, w_b, w_c, z, z, z)


def _route_kernel(l_ref, idx_ref, w_ref, *, n_experts):
    logits = l_ref[...]
    lane = lax.broadcasted_iota(jnp.int32, logits.shape, 1)
    logits = jnp.where(lane < n_experts, logits, -jnp.inf)
    m1 = jnp.max(logits, axis=-1, keepdims=True)
    i1 = jnp.min(jnp.where(logits == m1, lane, LANES), axis=-1, keepdims=True)
    rest = jnp.where(lane == i1, -jnp.inf, logits)
    m2 = jnp.max(rest, axis=-1, keepdims=True)
    i2 = jnp.min(jnp.where(rest == m2, lane, LANES), axis=-1, keepdims=True)
    e = jnp.exp(m2 - m1)
    w1 = 1.0 / (1.0 + e)
    w2 = e / (1.0 + e)
    idx_ref[...] = jnp.where(lane == 0, i1, jnp.where(lane == 1, i2, 0))
    w_ref[...] = jnp.where(lane == 0, w1, jnp.where(lane == 1, w2, 0.0))


def route_top2(logits, n_experts):
    T = logits.shape[0]
    tm = _divisor(T, (1024, 512, 256, 128, 8))
    spec = pl.BlockSpec((tm, LANES), lambda i: (i, 0))
    idx, w = pl.pallas_call(
        functools.partial(_route_kernel, n_experts=n_experts),
        grid=(T // tm,),
        in_specs=[spec],
        out_specs=[spec, spec],
        out_shape=[jax.ShapeDtypeStruct((T, LANES), jnp.int32), jax.ShapeDtypeStruct((T, LANES), F32)],
        compiler_params=_params(("parallel",), 3 * _nbytes((tm, LANES), F32)),
        name="route_top2",
    )(logits)
    return idx[:, :TOP_K], w


def _moe_gather_kernel(src_ref, nv_ref, x_hbm, o_ref, buf, sem, *, tg):
    i = pl.program_id(0)
    n_valid = nv_ref[0]
    slot = i % 2

    def row_copy(tile, r, sl):
        tok = src_ref[tile * tg + r]
        return pltpu.make_async_copy(x_hbm.at[pl.ds(tok, 1), :], buf.at[sl, pl.ds(r, 1), :], sem.at[sl])

    def start_tile(tile, sl):
        def body(g, carry):
            for u in range(DMA_LOOP_UNROLL):
                row_copy(tile, g * DMA_LOOP_UNROLL + u, sl).start(priority=u % 2)
            return carry
        lax.fori_loop(0, tg // DMA_LOOP_UNROLL, body, 0)

    @pl.when(i == 0)
    def _():
        start_tile(0, 0)

    @pl.when(i + 1 < n_valid)
    def _():
        start_tile(i + 1, 1 - slot)

    @pl.when(i < n_valid)
    def _():
        def body(r, carry):
            row_copy(i, r, slot).wait()
            return carry
        lax.fori_loop(0, tg, body, 0, unroll=DMA_LOOP_UNROLL)
        o_ref[...] = buf[slot].astype(o_ref.dtype)

    @pl.when(i >= n_valid)
    def _():
        o_ref[...] = jnp.zeros_like(o_ref)


def _weight_window(w_hbm, e, j, tn):
    return w_hbm.at[e, :, pl.ds(pl.multiple_of(j * tn, tn), tn)]


def _grouped_weights(s, sched, w_hbms, lands, scrs, sems, tn):
    ew_ref, jw_ref, fl_ref, ne_ref, nj_ref = sched

    def copies(e, j):
        return [pltpu.make_async_copy(_weight_window(w, e, j, tn), land.at[0], sems.at[k])
                for k, (w, land) in enumerate(zip(w_hbms, lands))]

    @pl.when(fl_ref[s] == 2)
    def _():
        @pl.when(s == 0)
        def _():
            for c in copies(ew_ref[0], jw_ref[0]):
                c.start()

        for c in copies(ew_ref[s], jw_ref[s]):
            c.wait()
        for land, scr in zip(lands, scrs):
            _stage_weight(land, scr)

        @pl.when(ne_ref[s] >= 0)
        def _():
            for c in copies(ne_ref[s], nj_ref[s]):
                c.start()


def _moe_up_kernel(t_ref, jo_ref, ew_ref, jw_ref, fl_ref, ne_ref, nj_ref, x_ref, wg_hbm, wu_hbm, o_ref,
                   wg_land, wu_land, wg_scr, wu_scr, sems, *, tf):
    s = pl.program_id(0)
    _grouped_weights(s, (ew_ref, jw_ref, fl_ref, ne_ref, nj_ref), (wg_hbm, wu_hbm), (wg_land, wu_land),
                     (wg_scr, wu_scr), sems, tf)

    @pl.when(fl_ref[s] > 0)
    def _():
        x = x_ref[...]
        g = jnp.dot(x, wg_scr[...], preferred_element_type=F32)
        u = jnp.dot(x, wu_scr[...], preferred_element_type=F32)
        o_ref[...] = (g * _sigmoid(g) * u).astype(o_ref.dtype)

    @pl.when(fl_ref[s] == 0)
    def _():
        o_ref[...] = jnp.zeros_like(o_ref)


def _moe_down_kernel(t_ref, jo_ref, ew_ref, jw_ref, fl_ref, ne_ref, nj_ref, a_ref, wd_hbm, o_ref,
                     wd_land, wd_scr, sems, *, tn):
    s = pl.program_id(0)
    _grouped_weights(s, (ew_ref, jw_ref, fl_ref, ne_ref, nj_ref), (wd_hbm,), (wd_land,), (wd_scr,), sems, tn)

    @pl.when(fl_ref[s] > 0)
    def _():
        o_ref[...] = jnp.dot(a_ref[...], wd_scr[...], preferred_element_type=F32)

    @pl.when(fl_ref[s] == 0)
    def _():
        o_ref[...] = jnp.zeros_like(o_ref)


def _moe_schedule(tiles_per_expert, n_tiles, n_cols):
    E = tiles_per_expert.shape[0]
    cnt = jnp.concatenate([tiles_per_expert, (n_tiles - jnp.sum(tiles_per_expert))[None]]).astype(jnp.int32)
    tile0 = jnp.cumsum(cnt) - cnt
    step_end = jnp.cumsum(cnt * n_cols)
    step0 = step_end - cnt * n_cols
    s = jnp.arange(n_tiles * n_cols, dtype=jnp.int32)
    g = jnp.sum((s[:, None] >= step_end[None, :]).astype(jnp.int32), axis=1)
    within = s - step0[g]
    c = jnp.maximum(cnt[g], 1)
    col = within // c
    tile = tile0[g] + within % c
    valid = g < E
    first = jnp.logical_and(valid, within % c == 0)
    n_valid = step_end[E - 1]
    last = jnp.maximum(n_valid - 1, 0)
    w_e = jnp.where(valid, g, g[last])
    w_j = jnp.where(valid, col, col[last])
    flag = jnp.where(first, 2, jnp.where(valid, 1, 0))
    nxt = s + c
    has_next = jnp.logical_and(first, nxt < n_valid)
    nxt = jnp.minimum(nxt, n_tiles * n_cols - 1)
    n_e = jnp.where(has_next, g[nxt], -1)
    n_j = jnp.where(has_next, col[nxt], 0)
    return [a.astype(jnp.int32) for a in (tile, col, w_e, w_j, flag, n_e, n_j)]


def _moe_combine_kernel(dest_ref, y_hbm, h_ref, w_ref, *rest, tm, final_norm):
    if final_norm:
        g_ref, o_ref, buf0, buf1, sem = rest
    else:
        o_ref, buf0, buf1, sem = rest
    i = pl.program_id(0)

    def row_copy(r, k, row):
        buf = buf0 if k == 0 else buf1
        return pltpu.make_async_copy(y_hbm.at[pl.ds(row, 1), :], buf.at[pl.ds(r, 1), :], sem.at[k])

    def start(r, carry):
        a = TOP_K * (i * tm + r)
        row_copy(r, 0, dest_ref[a]).start(priority=0)
        row_copy(r, 1, dest_ref[a + 1]).start(priority=1)
        return carry

    def wait(r, carry):
        row_copy(r, 0, 0).wait()
        row_copy(r, 1, 0).wait()
        return carry

    lax.fori_loop(0, tm, start, 0, unroll=DMA_LOOP_UNROLL)
    lax.fori_loop(0, tm, wait, 0, unroll=DMA_LOOP_UNROLL)
    w = w_ref[...]
    out = h_ref[...] + (w[:, 0:1] * buf0[...] + w[:, 1:2] * buf1[...])
    if final_norm:
        out = out * lax.rsqrt(jnp.mean(out * out, axis=-1, keepdims=True) + EPS) * g_ref[...]
    o_ref[...] = out


def moe_ffn(h, hn, logits, wg, wu, wd, layer, final_gain=None):
    T, D = hn.shape
    E, F = wg.shape[1], wg.shape[3]
    idx, wts = route_top2(logits, E)
    tm = _divisor(T, (256, 128))
    A = T * TOP_K
    n_tiles = A // tm + E

    e_flat = idx.reshape(-1)
    onehot = (e_flat[:, None] == jnp.arange(E, dtype=jnp.int32)[None, :]).astype(jnp.int32)
    csum = jnp.cumsum(onehot, axis=0)
    pos = jnp.take_along_axis(csum, e_flat[:, None], axis=1)[:, 0] - 1
    counts = csum[-1]
    padded = ((counts + tm - 1) // tm) * tm
    ends = jnp.cumsum(padded)
    starts = ends - padded
    dest = (starts[e_flat] + pos).astype(jnp.int32)
    src = jnp.zeros((n_tiles * tm,), jnp.int32).at[dest].set(jnp.arange(A, dtype=jnp.int32) // TOP_K)
    tiles_per_expert = padded // tm
    n_valid_tiles = jnp.sum(tiles_per_expert).astype(jnp.int32).reshape(1)

    x_sorted = pl.pallas_call(
        functools.partial(_moe_gather_kernel, tg=tm),
        grid_spec=pltpu.PrefetchScalarGridSpec(
            num_scalar_prefetch=2,
            grid=(n_tiles,),
            in_specs=[pl.BlockSpec(memory_space=pl.ANY)],
            out_specs=pl.BlockSpec((tm, D), lambda i, sr, nv: (i, 0)),
            scratch_shapes=[pltpu.VMEM((2, tm, D), F32), pltpu.SemaphoreType.DMA((2,))],
        ),
        out_shape=jax.ShapeDtypeStruct((n_tiles * tm, D), BF16),
        compiler_params=_params(("arbitrary",), _nbytes((tm, D), BF16), 3 * _nbytes((tm, D), F32)),
        name="moe_gather",
    )(src, n_valid_tiles, hn)

    def experts(w):
        return w.reshape((-1,) + w.shape[2:])

    tf = _divisor(F, (512, 256, 128))
    sched = _moe_schedule(tiles_per_expert, n_tiles, F // tf)
    sched[2] = sched[2] + layer * E
    sched[5] = jnp.where(sched[5] >= 0, sched[5] + layer * E, -1)
    hbm = pl.BlockSpec(memory_space=pl.ANY)
    act = pl.pallas_call(
        functools.partial(_moe_up_kernel, tf=tf),
        grid_spec=pltpu.PrefetchScalarGridSpec(
            num_scalar_prefetch=7,
            grid=(n_tiles * (F // tf),),
            in_specs=[pl.BlockSpec((tm, D), lambda s, t, jo, *_: (t[s], 0)), hbm, hbm],
            out_specs=pl.BlockSpec((tm, tf), lambda s, t, jo, *_: (t[s], jo[s])),
            scratch_shapes=[pltpu.VMEM((1, D, tf), F32), pltpu.VMEM((1, D, tf), F32),
                            pltpu.VMEM((D, tf), BF16), pltpu.VMEM((D, tf), BF16), pltpu.SemaphoreType.DMA((2,))],
        ),
        out_shape=jax.ShapeDtypeStruct((n_tiles * tm, F), BF16),
        compiler_params=_params(("arbitrary",), _nbytes((tm, D), BF16) + _nbytes((tm, tf), BF16),
                                2 * _nbytes((D, tf), F32) + 2 * _nbytes((D, tf), BF16) + 3 * _nbytes((tm, tf), F32)),
        name="moe_up",
    )(*sched, x_sorted, experts(wg), experts(wu))

    tn = _divisor(D, (1024, 512, 256, 128))
    sched = _moe_schedule(tiles_per_expert, n_tiles, D // tn)
    sched[2] = sched[2] + layer * E
    sched[5] = jnp.where(sched[5] >= 0, sched[5] + layer * E, -1)
    y_sorted = pl.pallas_call(
        functools.partial(_moe_down_kernel, tn=tn),
        grid_spec=pltpu.PrefetchScalarGridSpec(
            num_scalar_prefetch=7,
            grid=(n_tiles * (D // tn),),
            in_specs=[pl.BlockSpec((tm, F), lambda s, t, jo, *_: (t[s], 0)), hbm],
            out_specs=pl.BlockSpec((tm, tn), lambda s, t, jo, *_: (t[s], jo[s])),
            scratch_shapes=[pltpu.VMEM((1, F, tn), F32), pltpu.VMEM((F, tn), BF16), pltpu.SemaphoreType.DMA((1,))],
        ),
        out_shape=jax.ShapeDtypeStruct((n_tiles * tm, D), F32),
        compiler_params=_params(("arbitrary",), _nbytes((tm, F), BF16) + _nbytes((tm, tn), F32),
                                _nbytes((F, tn), F32) + _nbytes((F, tn), BF16) + _nbytes((tm, tn), F32)),
        name="moe_down",
    )(*sched, act, experts(wd))

    tc = _divisor(T, (256, 128))
    blk_c = 2 * _nbytes((tc, D), F32) + _nbytes((tc, LANES), F32)
    in_specs = [hbm, pl.BlockSpec((tc, D), lambda i, de: (i, 0)), pl.BlockSpec((tc, LANES), lambda i, de: (i, 0))]
    args = [dest, y_sorted, h, wts]
    if final_gain is not None:
        in_specs.append(pl.BlockSpec((1, D), lambda i, de: (0, 0)))
        args.append(final_gain.reshape(1, D).astype(F32))
    return pl.pallas_call(
        functools.partial(_moe_combine_kernel, tm=tc, final_norm=final_gain is not None),
        grid_spec=pltpu.PrefetchScalarGridSpec(
            num_scalar_prefetch=1,
            grid=(T // tc,),
            in_specs=in_specs,
            out_specs=pl.BlockSpec((tc, D), lambda i, de: (i, 0)),
            scratch_shapes=[pltpu.VMEM((tc, D), F32), pltpu.VMEM((tc, D), F32), pltpu.SemaphoreType.DMA((2,))],
        ),
        out_shape=jax.ShapeDtypeStruct((T, D), F32),
        compiler_params=_params(("arbitrary",), blk_c, 3 * _nbytes((tc, D), F32)),
        name="moe_combine",
    )(*args)


def _mixer(h, li, batch, seq, near_bias, tq, norm_mix, w_in, a_gate_bias, a_conv, a_norm, b_ln_g, b_ln_b, b_ws,
           b_bs, c_lambda, c_norm, w_br_a, w_br_b, w_br_c, w_out):
    d_model = h.shape[1]
    h_a = a_gate_bias.shape[1] // 2
    g_b = b_ws.shape[1]
    h_c = w_br_c.shape[1] // HEAD_DIM
    w_a = h_a * HEAD_DIM
    g0 = 4 * w_a
    g1 = g0 + 2 * h_a
    n_rest = w_in.shape[2] - g1
    w_in_t = jnp.swapaxes(w_in, 1, 2)
    n, zg = rmsnorm(h, norm_mix[li], proj=jnp.transpose(w_in_t[li, g0:g1, :]))
    z_a = matmul(n, w_in_t, li, BF16, n=g0, transposed=True)
    z_r = matmul(n, w_in_t, li, BF16, w_col=g1, n=n_rest, transposed=True)
    y_a = mlstm_branch(z_a, zg, a_gate_bias[li], a_conv[li], a_norm[li], batch, seq, h_a)
    y_b = sgu_branch(z_r, b_ln_g[li], b_ln_b[li], b_ws[li], b_bs[li], g_b)
    lam_init = 0.8 - 0.6 * math.exp(-0.3 * li)
    y_c = attn_branch(z_r, 2 * g_b, near_bias, c_lambda[li], c_norm[li], batch, seq, h_c, lam_init, tq)
    merged = gated_merge(y_a, y_b, y_c, w_br_a, w_br_b, w_br_c, li, z_r, 2 * g_b + 3 * h_c, d_model)
    return matmul_residual(merged, w_out, li, h)


def kernel(x, norm_mix, w_in, a_gate_bias, a_conv, a_norm, b_ln_g, b_ln_b, b_ws, b_bs, c_lambda, c_norm, rel_bias,
           w_br_a, w_br_b, w_br_c, w_out, norm_ffn, ffn_wg, ffn_wu, ffn_wd, router, moe_wg, moe_wu, moe_wd,
           final_norm):
    batch, seq, d_model = x.shape
    depth = w_in.shape[0]
    tq = _divisor(seq, (512, 256, 128))
    near_bias = attn_bias_tiles(rel_bias, tq)
    h = x.reshape(batch * seq, d_model).astype(F32)
    for li in range(depth):
        h = _mixer(h, li, batch, seq, near_bias, tq, norm_mix, w_in, a_gate_bias, a_conv, a_norm, b_ln_g, b_ln_b,
                   b_ws, b_bs, c_lambda, c_norm, w_br_a, w_br_b, w_br_c, w_out)
        j = li // 2
        fused_final = False
        if li % 2 == 0:
            hn = rmsnorm(h, norm_ffn[li])
            act = swiglu_up(hn, ffn_wg, ffn_wu, j)
            h = matmul_residual(act, ffn_wd, j, h)
        else:
            hn, logits = rmsnorm(h, norm_ffn[li], proj=router[j], out_dtype=F32)
            fused_final = li == depth - 1
            h = moe_ffn(h, hn, logits, moe_wg, moe_wu, moe_wd, j, final_norm if fused_final else None)
    out = h if fused_final else rmsnorm(h, final_norm, out_dtype=F32)
    return out.reshape(batch, seq, d_model).astype(x.dtype)
```

```python
import functools
import itertools
import math

import numpy as np
import jax
import jax.numpy as jnp
from jax import lax
from jax.experimental import pallas as pl
from jax.experimental.pallas import tpu as pltpu

F32 = jnp.float32
BF16 = jnp.bfloat16

EPS = 1e-6
HEAD_DIM = 128
CHUNK = 128
CONV_W = 4
N_BUCKETS = 32
MAX_DIST = 128
TOP_K = 2
LANES = 128
VMEM_BYTES_V7X = 64 * 1024 * 1024
VMEM_SLACK = 6 * 1024 * 1024
VMEM_BUDGET = VMEM_BYTES_V7X - 8 * 1024 * 1024
DMA_LOOP_UNROLL = 8


def _divisor(n, candidates):
    for c in candidates:
        if n % c == 0:
            return c
    raise ValueError(f"no tile size in {candidates} divides {n}")


def _vmem_need(block_bytes, scratch_bytes):
    return 2 * block_bytes + scratch_bytes + VMEM_SLACK


def _params(semantics, block_bytes, scratch_bytes=0):
    limit = int(min(max(_vmem_need(block_bytes, scratch_bytes), 16 * 1024 * 1024), VMEM_BYTES_V7X - 4 * 1024 * 1024))
    return pltpu.CompilerParams(dimension_semantics=semantics, vmem_limit_bytes=limit)


def _row_tile(T, block_bytes, scratch_bytes):
    for tm in (1024, 512, 256, 128):
        if T % tm == 0 and _vmem_need(block_bytes(tm), scratch_bytes(tm)) <= VMEM_BUDGET:
            return tm
    raise ValueError("no row tile fits VMEM")


def _nbytes(shape, dtype):
    return int(np.prod(shape)) * jnp.dtype(dtype).itemsize


def _sigmoid(x):
    return 1.0 / (1.0 + jnp.exp(-x))


def _split_bf16(x):
    hi = x.astype(BF16)
    lo = (x - hi.astype(F32)).astype(BF16)
    return hi, lo


def _dot_nt(a, b):
    return lax.dot_general(a, b, (((1,), (1,)), ((), ())), preferred_element_type=F32)


def _dot_tn(a, b):
    return lax.dot_general(a, b, (((0,), (0,)), ((), ())), preferred_element_type=F32)


def _norm_kernel(*refs, has_proj, out_dtype):
    if has_proj:
        x_ref, g_ref, w_ref, y_ref, p_ref = refs
    else:
        x_ref, g_ref, y_ref = refs
    x = x_ref[...]
    y = x * lax.rsqrt(jnp.mean(x * x, axis=-1, keepdims=True) + EPS) * g_ref[...]
    y_ref[...] = y.astype(out_dtype)
    if has_proj:
        y_hi, y_lo = _split_bf16(y)
        w = w_ref[...]
        both = jnp.dot(y_hi, w, preferred_element_type=F32)
        p_ref[...] = (both[:, :LANES] + both[:, LANES:]
                      + jnp.dot(y_lo, w[:, :LANES], preferred_element_type=F32))


def rmsnorm(x, g, proj=None, out_dtype=BF16):
    T, D = x.shape
    tm = _divisor(T, (512, 256, 128, 8))
    g2 = g.reshape(1, D).astype(F32)
    in_specs = [pl.BlockSpec((tm, D), lambda i: (i, 0)), pl.BlockSpec((1, D), lambda i: (0, 0))]
    out_shape = [jax.ShapeDtypeStruct((T, D), out_dtype)]
    out_specs = [pl.BlockSpec((tm, D), lambda i: (i, 0))]
    args = [x, g2]
    blk = _nbytes((tm, D), F32) + _nbytes((tm, D), out_dtype)
    if proj is not None:
        n = proj.shape[1]
        assert n <= LANES
        w = jnp.pad(proj.astype(F32), ((0, 0), (0, LANES - n)))
        in_specs.append(pl.BlockSpec((D, 2 * LANES), lambda i: (0, 0)))
        out_shape.append(jax.ShapeDtypeStruct((T, LANES), F32))
        out_specs.append(pl.BlockSpec((tm, LANES), lambda i: (i, 0)))
        args.append(jnp.concatenate(_split_bf16(w), axis=1))
        blk += _nbytes((D, 2 * LANES), BF16)
    outs = pl.pallas_call(
        functools.partial(_norm_kernel, has_proj=proj is not None, out_dtype=out_dtype),
        grid=(T // tm,),
        in_specs=in_specs,
        out_specs=out_specs,
        out_shape=out_shape,
        compiler_params=_params(("parallel",), blk, 2 * _nbytes((tm, D), F32)),
        name="rmsnorm",
    )(*args)
    return (outs[0], outs[1]) if proj is not None else outs[0]


def _stationary_step(j, nj, first, copies, lands, scrs, compute):
    @pl.when(first)
    def _():
        @pl.when(j == 0)
        def _():
            for c in copies(0):
                c.start()

        for c in copies(j):
            c.wait()
        ws = []
        for land, scr in zip(lands, scrs):
            w = land[0].astype(BF16)
            scr[...] = w
            ws.append(w)
        compute(*ws)

        @pl.when(j + 1 < nj)
        def _():
            for c in copies(j + 1):
                c.start()

    @pl.when(jnp.logical_not(first))
    def _():
        compute(*[scr[...] for scr in scrs])


def _matmul_kernel(*refs, layer, w_row, w_col, k, tn, transposed, has_res):
    a_ref, w_hbm = refs[:2]
    r_ref = refs[2] if has_res else None
    o_ref, w_land, w_scr, sem = refs[-4:]
    j = pl.program_id(0)
    nj = pl.num_programs(0)

    def window(jj):
        if transposed:
            src = w_hbm.at[layer, pl.ds(pl.multiple_of(w_col + jj * tn, 8), tn), pl.ds(w_row, k)]
        else:
            src = w_hbm.at[layer, pl.ds(w_row, k), pl.ds(pl.multiple_of(w_col + jj * tn, LANES), tn)]
        return pltpu.make_async_copy(src, w_land.at[0], sem)

    def compute(w):
        if transposed:
            d = _dot_nt(a_ref[...], w)
        else:
            d = jnp.dot(a_ref[...], w, preferred_element_type=F32)
        if has_res:
            d = d + r_ref[...]
        o_ref[...] = d.astype(o_ref.dtype)

    _stationary_step(j, nj, pl.program_id(1) == 0, lambda jj: [window(jj)], (w_land,), (w_scr,), compute)


def matmul(a, w, layer, out_dtype, *, a_col=0, k=None, w_row=0, w_col=0, n=None, transposed=False, res=None):
    T = a.shape[0]
    k = a.shape[1] if k is None else k
    n_total = w.shape[1] if transposed else w.shape[2]
    n = n_total - w_col if n is None else n
    tn = _divisor(n, (512, 256, 128))
    assert a_col % k == 0 and w_col % (8 if transposed else LANES) == 0
    wshape = (tn, k) if transposed else (k, tn)
    res_b = (lambda tm: _nbytes((tm, tn), F32)) if res is not None else (lambda tm: 0)

    def blk(tm):
        return _nbytes((tm, k), BF16) + _nbytes((tm, tn), out_dtype) + res_b(tm)

    def scr(tm):
        return _nbytes(wshape, F32) + _nbytes(wshape, BF16) + 2 * _nbytes((tm, tn), F32)

    tm = _row_tile(T, blk, scr)
    ab = a_col // k
    in_specs = [pl.BlockSpec((tm, k), lambda j, i: (i, ab)), pl.BlockSpec(memory_space=pl.ANY)]
    args = [a, w]
    if res is not None:
        in_specs.append(pl.BlockSpec((tm, tn), lambda j, i: (i, j)))
        args.append(res)
    return pl.pallas_call(
        functools.partial(_matmul_kernel, layer=layer, w_row=w_row, w_col=w_col, k=k, tn=tn, transposed=transposed,
                          has_res=res is not None),
        grid=(n // tn, T // tm),
        in_specs=in_specs,
        out_specs=pl.BlockSpec((tm, tn), lambda j, i: (i, j)),
        out_shape=jax.ShapeDtypeStruct((T, n), out_dtype),
        scratch_shapes=[pltpu.VMEM((1,) + wshape, F32), pltpu.VMEM(wshape, BF16), pltpu.SemaphoreType.DMA(())],
        compiler_params=_params(("arbitrary", "arbitrary"), blk(tm), scr(tm)),
        name="matmul",
    )(*args)


def matmul_residual(a, w, layer, res):
    K = a.shape[1]
    nk = next(n for n in range(1, K // LANES + 1) if K % (n * LANES) == 0 and K // n <= 6144)
    k = K // nk
    out = res
    for p in range(nk):
        out = matmul(a, w, layer, F32, a_col=p * k, k=k, w_row=p * k, res=out)
    return out


def _swiglu_up_kernel(x_ref, wg_hbm, wu_hbm, o_ref, wg_land, wu_land, wg_scr, wu_scr, sems, *, layer, tn):
    j = pl.program_id(0)

    def copies(jj):
        cols = pl.ds(pl.multiple_of(jj * tn, LANES), tn)
        return [pltpu.make_async_copy(w.at[layer, :, cols], land.at[0], sems.at[n])
                for n, (w, land) in enumerate(((wg_hbm, wg_land), (wu_hbm, wu_land)))]

    def compute(wg, wu):
        x = x_ref[...]
        g = jnp.dot(x, wg, preferred_element_type=F32)
        u = jnp.dot(x, wu, preferred_element_type=F32)
        o_ref[...] = (g * _sigmoid(g) * u).astype(o_ref.dtype)

    _stationary_step(j, pl.num_programs(0), pl.program_id(1) == 0, copies, (wg_land, wu_land), (wg_scr, wu_scr),
                     compute)


def swiglu_up(x, wg, wu, layer):
    T, K = x.shape
    F = wg.shape[2]
    tn = _divisor(F, (512, 256, 128))

    def blk(tm):
        return _nbytes((tm, K), BF16) + _nbytes((tm, tn), BF16)

    def scr(tm):
        return 2 * _nbytes((K, tn), F32) + 2 * _nbytes((K, tn), BF16) + 3 * _nbytes((tm, tn), F32)

    tm = _row_tile(T, blk, scr)
    hbm = pl.BlockSpec(memory_space=pl.ANY)
    return pl.pallas_call(
        functools.partial(_swiglu_up_kernel, layer=layer, tn=tn),
        grid=(F // tn, T // tm),
        in_specs=[pl.BlockSpec((tm, K), lambda j, i: (i, 0)), hbm, hbm],
        out_specs=pl.BlockSpec((tm, tn), lambda j, i: (i, j)),
        out_shape=jax.ShapeDtypeStruct((T, F), BF16),
        scratch_shapes=[pltpu.VMEM((1, K, tn), F32), pltpu.VMEM((1, K, tn), F32),
                        pltpu.VMEM((K, tn), BF16), pltpu.VMEM((K, tn), BF16), pltpu.SemaphoreType.DMA((2,))],
        compiler_params=_params(("arbitrary", "arbitrary"), blk(tm), scr(tm)),
        name="swiglu_up",
    )(x, wg, wu)


def _split_bf16_f32(x):
    hi = x.astype(BF16)
    return hi, x - hi.astype(F32)


def _mlstm_kernel(gb_ref, q_ref, k_ref, v_ref, og_ref, ig_ref, fg_ref, cwq_ref, cwk_ref, an_ref, y_ref,
                  c_scr, n_scr, m_scr, pq_scr, pk_scr, *, n_heads, n_chunks, group):
    hg = pl.program_id(1)
    L = CHUNK
    c_scr[...] = jnp.zeros_like(c_scr)
    n_scr[...] = jnp.zeros_like(n_scr)
    m_scr[...] = jnp.zeros_like(m_scr)
    pq_scr[...] = jnp.zeros_like(pq_scr)
    pk_scr[...] = jnp.zeros_like(pk_scr)
    ri = lax.broadcasted_iota(jnp.int32, (L, L), 0)
    ci = lax.broadcasted_iota(jnp.int32, (L, L), 1)
    lower = ci <= ri
    strict_lower01 = jnp.where(ri > ci, 1.0, 0.0).astype(BF16)

    def conv_silu(x, prev, w):
        y = x * w[CONV_W - 1:CONV_W, :]
        for s in range(1, CONV_W):
            shifted = jnp.where(ri < s, pltpu.roll(prev, s, 0), pltpu.roll(x, s, 0))
            y = y + shifted * w[CONV_W - 1 - s:CONV_W - s, :]
        return y * _sigmoid(y)

    def head_chunk(c, rows, g):
        cols = slice(g * HEAD_DIM, (g + 1) * HEAD_DIM)
        h = hg * group + g
        xq = q_ref[rows, cols].astype(F32)
        xk = k_ref[rows, cols].astype(F32)
        q = conv_silu(xq, pq_scr[g], cwq_ref[:, cols])
        k = conv_silu(xk, pk_scr[g], cwk_ref[:, cols]) * (HEAD_DIM ** -0.5)
        pq_scr[g] = xq
        pk_scr[g] = xk
        v = v_ref[rows, cols]
        q_bf = q.astype(BF16)
        k_bf = k.astype(BF16)
        yield

        i_row = ig_ref[0, g, pl.ds(c, 1), :] + gb_ref[h]
        f_row = fg_ref[0, g, pl.ds(c, 1), :] + gb_ref[n_heads + h]
        lf = jnp.minimum(f_row, 0.0) - jnp.log(1.0 + jnp.exp(-jnp.abs(f_row)))
        lf_low = jnp.where(lower, lf, 0.0)
        b_col = jnp.sum(lf_low, axis=1, keepdims=True)
        hi, rest = _split_bf16_f32(lf_low)
        mid, lo = _split_bf16_f32(rest)
        yield
        dmat = (jnp.dot(hi, strict_lower01, preferred_element_type=F32)
                + jnp.dot(mid, strict_lower01, preferred_element_type=F32)
                + jnp.dot(lo.astype(BF16), strict_lower01, preferred_element_type=F32))
        g_tot = b_col[L - 1:L, :]
        a_row = dmat[L - 1:L, :] + i_row
        a_col = jnp.sum(jnp.where(ci > ri, lf, 0.0) + jnp.where(ci == ri, i_row, 0.0),
                        axis=1, keepdims=True)
        yield

        m_prev = m_scr[g]
        c_prev = c_scr[g]
        n_prev = n_scr[g]

        log_d = jnp.where(lower, dmat + i_row, -jnp.inf)
        log_inter = b_col + m_prev
        m_t = jnp.maximum(log_inter, jnp.max(log_d, axis=1, keepdims=True))
        qk = _dot_nt(q_bf, k_bf)
        yield
        w = jnp.exp(log_d - m_t) * qk
        e_inter = jnp.exp(log_inter - m_t)
        yield
        num = (e_inter * jnp.dot(q_bf, c_prev.astype(BF16), preferred_element_type=F32)
               + jnp.dot(w.astype(BF16), v, preferred_element_type=F32))
        den = e_inter * jnp.sum(q * n_prev, axis=1, keepdims=True) + jnp.sum(w, axis=1, keepdims=True)
        yield
        hcell = num / jnp.maximum(jnp.abs(den), jnp.exp(-m_t))

        gated = _sigmoid(og_ref[rows, cols].astype(F32)) * hcell
        y = gated * lax.rsqrt(jnp.mean(gated * gated, axis=-1, keepdims=True) + EPS) * an_ref[:, cols]
        y_ref[rows, cols] = y.astype(y_ref.dtype)
        yield

        m_new =jnp.maximum(g_tot + m_prev, jnp.max(a_row, axis=1, keepdims=True))
        w_col = jnp.exp(a_col - m_new)
        decay = jnp.exp(g_tot + m_prev - m_new)
        c_scr[g] = decay * c_prev + _dot_tn(k_bf, (v.astype(F32) * w_col).astype(BF16))
        n_scr[g] = decay * n_prev + jnp.sum(k * w_col, axis=0, keepdims=True)
        m_scr[g] = m_new

    def chunk(c, carry):
        rows = pl.ds(pl.multiple_of(c * L, L), L)
        stages = [head_chunk(c, rows, g) for g in range(group)]
        for _ in itertools.zip_longest(*stages):
            pass
        return carry

    lax.fori_loop(0, n_chunks, chunk, 0)


def mlstm_branch(z, zg, a_gate_bias, a_conv, a_norm, batch, seq, n_heads):
    T = batch * seq
    nc = seq // CHUNK
    W = n_heads * HEAD_DIM
    group = _divisor(n_heads, (2, 1))
    ng = n_heads // group
    gw = group * HEAD_DIM

    def gate_rows(g):
        return jnp.transpose(g.reshape(batch, seq, n_heads), (0, 2, 1)).reshape(batch, n_heads, nc, CHUNK)

    ig = gate_rows(zg[:, :n_heads])
    fg = gate_rows(zg[:, n_heads:2 * n_heads])

    def zspec(part):
        return pl.BlockSpec((seq, gw), lambda b, h, gb: (b, part * ng + h))

    gspec = pl.BlockSpec((1, group, nc, CHUNK), lambda b, h, gb: (b, h, 0, 0))
    blk = 5 * _nbytes((seq, gw), BF16) + 2 * _nbytes((group, nc, CHUNK), F32)
    grid_spec = pltpu.PrefetchScalarGridSpec(
        num_scalar_prefetch=1,
        grid=(batch, ng),
        in_specs=[zspec(0), zspec(1), zspec(2), zspec(3), gspec, gspec,
                  pl.BlockSpec((CONV_W, gw), lambda b, h, gb: (0, h)),
                  pl.BlockSpec((CONV_W, gw), lambda b, h, gb: (0, ng + h)),
                  pl.BlockSpec((1, gw), lambda b, h, gb: (0, h))],
        out_specs=pl.BlockSpec((seq, gw), lambda b, h, gb: (b, h)),
        scratch_shapes=[pltpu.VMEM((group, HEAD_DIM, HEAD_DIM), F32), pltpu.VMEM((group, 1, HEAD_DIM), F32),
                        pltpu.VMEM((group, 1, 1), F32), pltpu.VMEM((group, CHUNK, HEAD_DIM), F32),
                        pltpu.VMEM((group, CHUNK, HEAD_DIM), F32)],
    )
    return pl.pallas_call(
        functools.partial(_mlstm_kernel, n_heads=n_heads, n_chunks=nc, group=group),
        grid_spec=grid_spec,
        out_shape=jax.ShapeDtypeStruct((T, W), BF16),
        compiler_params=_params(("parallel", "parallel"), blk, 64 * group * _nbytes((CHUNK, HEAD_DIM), F32)),
        name="mlstm",
    )(a_gate_bias.astype(F32), z, z, z, z, ig, fg, a_conv.astype(F32), a_conv.astype(F32),
      a_norm.reshape(1, W).astype(F32))


def _sgu_kernel(u_ref, v_ref, lng_ref, lnb_ref, ws_ref, bst_ref, y_ref, *, n_groups, chunks_per_block):
    L = CHUNK
    c0 = math.sqrt(2.0 / math.pi)

    def gelu(x):
        return x * (0.5 * (1.0 + jnp.tanh(c0 * (x + 0.044715 * (x * x * x)))))

    v = gelu(v_ref[...].astype(F32))
    mu = jnp.mean(v, axis=-1, keepdims=True)
    vc = v - mu
    vn = vc * lax.rsqrt(jnp.mean(vc * vc, axis=-1, keepdims=True) + EPS) * lng_ref[...] + lnb_ref[...]
    vn = vn.astype(BF16)
    ri = lax.broadcasted_iota(jnp.int32, (L, L), 0)
    ci = lax.broadcasted_iota(jnp.int32, (L, L), 1)
    bst = bst_ref[...]
    for g in range(n_groups):
        cs = slice(g * HEAD_DIM, (g + 1) * HEAD_DIM)
        w = jnp.where(ci <= ri, ws_ref[g], 0.0).astype(BF16)
        bias = bst[:, g:g + 1]
        for c in range(chunks_per_block):
            rs = slice(c * L, (c + 1) * L)
            s = jnp.dot(w, vn[rs, cs], preferred_element_type=F32) + bias
            y_ref[rs, cs] = (gelu(u_ref[rs, cs].astype(F32)) * s).astype(y_ref.dtype)


def sgu_branch(z, b_ln_g, b_ln_b, b_ws, b_bs, n_groups):
    T = z.shape[0]
    W = n_groups * HEAD_DIM
    cpb = _divisor(T // CHUNK, (4, 2, 1))
    R = cpb * CHUNK
    blk = 3 * _nbytes((R, W), BF16) + _nbytes((n_groups, CHUNK, CHUNK), F32)
    return pl.pallas_call(
        functools.partial(_sgu_kernel, n_groups=n_groups, chunks_per_block=cpb),
        grid=(T // R,),
        in_specs=[pl.BlockSpec((R, W), lambda i: (i, 0)),
                  pl.BlockSpec((R, W), lambda i: (i, 1)),
                  pl.BlockSpec((1, W), lambda i: (0, 0)),
                  pl.BlockSpec((1, W), lambda i: (0, 0)),
                  pl.BlockSpec((n_groups, CHUNK, CHUNK), lambda i: (0, 0, 0)),
                  pl.BlockSpec((CHUNK, n_groups), lambda i: (0, 0))],
        out_specs=pl.BlockSpec((R, W), lambda i: (i, 0)),
        out_shape=jax.ShapeDtypeStruct((T, W), BF16),
        compiler_params=_params(("parallel",), blk, 6 * _nbytes((R, W), F32)),
        name="spatial_gating",
    )(z, z, b_ln_g.reshape(1, W).astype(F32), b_ln_b.reshape(1, W).astype(F32), b_ws.astype(F32),
      jnp.transpose(b_bs).astype(F32))


def _t5_bucket_table(n):
    d = np.arange(n, dtype=np.int64)
    max_exact = N_BUCKETS // 2
    nf = np.maximum(d, 1).astype(np.float32)
    scaled = (np.log(nf / np.float32(max_exact)) / np.float32(math.log(MAX_DIST / max_exact))
              * np.float32(N_BUCKETS - max_exact))
    large = np.minimum(max_exact + scaled.astype(np.int32), N_BUCKETS - 1)
    return np.where(d < max_exact, d, large).astype(np.int32)


def _bias_kernel(rb_ref, bkt_ref, o_ref, *, n_heads):
    h = pl.program_id(0)
    bkt = bkt_ref[...]
    far = rb_ref[(N_BUCKETS - 1) * n_heads + h]
    out = jnp.full(bkt.shape, -jnp.inf, F32)
    for b in range(N_BUCKETS):
        out = jnp.where(bkt == b, rb_ref[b * n_heads + h] - far, out)
    o_ref[0] = out


def attn_bias_tiles(rel_bias, tq):
    n_heads = rel_bias.shape[1]
    assert tq >= MAX_DIST
    c = np.arange(2 * tq)[:, None]
    r = np.arange(tq)[None, :]
    dist = r - c + tq
    table = _t5_bucket_table(2 * tq)
    bkt = np.where(dist >= 0, table[np.maximum(dist, 0)], -1).astype(np.int32)
    grid_spec = pltpu.PrefetchScalarGridSpec(
        num_scalar_prefetch=1,
        grid=(n_heads,),
        in_specs=[pl.BlockSpec((2 * tq, tq), lambda h, rb: (0, 0))],
        out_specs=pl.BlockSpec((1, 2 * tq, tq), lambda h, rb: (h, 0, 0)),
    )
    return pl.pallas_call(
        functools.partial(_bias_kernel, n_heads=n_heads),
        grid_spec=grid_spec,
        out_shape=jax.ShapeDtypeStruct((n_heads, 2 * tq, tq), F32),
        compiler_params=_params(("arbitrary",), 2 * _nbytes((2 * tq, tq), F32)),
        name="attn_bias_tiles",
    )(rel_bias.astype(F32).reshape(-1), jnp.asarray(bkt))


ONES_ROWS = 16
ATTN_FAR_KEYS = 1024


def _attn_kernel(q_ref, k_ref, v_ref, nb_ref, lam_ref, cn_ref, y_ref,
                 vt_scr, m1_scr, a1_scr, m2_scr, a2_scr, *, tq, seq, lam_init):
    qi = pl.program_id(2)
    dk = HEAD_DIM // 2
    dv = HEAD_DIM
    scale = dk ** -0.5

    @pl.when(qi == 0)
    def _():
        for c in range(seq // tq):
            cs = slice(c * tq, (c + 1) * tq)
            vt_scr[:dv, cs] = v_ref[cs, :].astype(F32).T.astype(BF16)
        vt_scr[dv:, :] = jnp.ones((ONES_ROWS, seq), BF16)

    q = (q_ref[...].astype(F32) * scale).astype(BF16)
    lane = lax.broadcasted_iota(jnp.int32, q.shape, 1)
    zero = jnp.zeros_like(q)
    q1 = jnp.where(lane < dk, q, zero)
    q2 = jnp.where(lane >= dk, q, zero)
    for m_scr, a_scr in ((m1_scr, a1_scr), (m2_scr, a2_scr)):
        m_scr[...] = jnp.full(m_scr.shape, -jnp.inf, F32)
        a_scr[...] = jnp.zeros_like(a_scr)

    def update(s, vtb, m_scr, a_scr):
        m_prev = m_scr[...]
        sb = s.astype(BF16)
        m_new = jnp.maximum(m_prev, jnp.max(sb, axis=0, keepdims=True).astype(F32))
        alpha = jnp.exp(m_prev - m_new)
        p = jnp.exp(sb - m_new.astype(BF16))
        a_scr[...] = alpha * a_scr[...] + jnp.dot(vtb, p, preferred_element_type=F32)
        m_scr[...] = m_new

    def block(kstart, nk, bias):
        rows = pl.ds(pl.multiple_of(kstart, tq), nk)
        kb = k_ref[rows, :]
        vtb = vt_scr[:, rows]
        s1 = _dot_nt(kb, q1)
        s2 = _dot_nt(kb, q2)
        if bias is not None:
            s1 = s1 + bias
            s2 = s2 + bias
        update(s1, vtb, m1_scr, a1_scr)
        update(s2, vtb, m2_scr, a2_scr)

    big = max(ATTN_FAR_KEYS // tq, 1)
    n_far = jnp.maximum(qi - 1, 0)
    n_big = n_far // big

    def far_block(j, carry):
        block(j * (big * tq), big * tq, None)
        return carry

    lax.fori_loop(0, n_big, far_block, 0)
    rem = n_far - n_big * big
    unit = big // 2
    while unit >= 1:
        done = n_big * big + (rem // (2 * unit)) * (2 * unit)

        @pl.when((rem // unit) % 2 == 1)
        def _(done=done, unit=unit):
            block(done * tq, unit * tq, None)

        unit //= 2

    @pl.when(qi > 0)
    def _():
        block((qi - 1) * tq, 2 * tq, nb_ref[0])

    @pl.when(qi == 0)
    def _():
        block(0, tq, nb_ref[0, tq:, :])

    lf = lam_ref[...]
    lam = (jnp.exp(jnp.sum(lf[0:1] * lf[1:2], axis=-1, keepdims=True))
           - jnp.exp(jnp.sum(lf[2:3] * lf[3:4], axis=-1, keepdims=True)) + lam_init)
    a1 = a1_scr[...]
    a2 = a2_scr[...]
    o = a1[:dv] / a1[dv:dv + 1] - lam * (a2[:dv] / a2[dv:dv + 1])
    o = o * lax.rsqrt(jnp.mean(o * o, axis=0, keepdims=True) + EPS) * cn_ref[...] * (1.0 - lam_init)
    y_ref[...] = o.T.astype(y_ref.dtype)


def attn_branch(z, col0, near_bias, c_lambda, c_norm, batch, seq, n_heads, lam_init, tq):
    T = batch * seq
    nq = seq // tq
    W = n_heads * HEAD_DIM
    blk = (2 * _nbytes((tq, HEAD_DIM), BF16) + 2 * _nbytes((seq, HEAD_DIM), BF16)
           + _nbytes((tq, 2 * tq), F32))
    return pl.pallas_call(
        functools.partial(_attn_kernel, tq=tq, seq=seq, lam_init=lam_init),
        grid=(batch, n_heads, nq),
        in_specs=[pl.BlockSpec((tq, HEAD_DIM), lambda b, h, i: (b * nq + i, col0 + h)),
                  pl.BlockSpec((seq, HEAD_DIM), lambda b, h, i: (b, col0 + n_heads + h)),
                  pl.BlockSpec((seq, HEAD_DIM), lambda b, h, i: (b, col0 + 2 * n_heads + h)),
                  pl.BlockSpec((1, 2 * tq, tq), lambda b, h, i: (h, 0, 0)),
                  pl.BlockSpec(c_lambda.shape, lambda b, h, i: (0, 0)),
                  pl.BlockSpec((HEAD_DIM, 1), lambda b, h, i: (0, 0))],
        out_specs=pl.BlockSpec((tq, HEAD_DIM), lambda b, h, i: (b * nq + i, h)),
        out_shape=jax.ShapeDtypeStruct((T, W), BF16),
        scratch_shapes=[pltpu.VMEM((HEAD_DIM + ONES_ROWS, seq), BF16),
                        pltpu.VMEM((1, tq), F32), pltpu.VMEM((HEAD_DIM + ONES_ROWS, tq), F32),
                        pltpu.VMEM((1, tq), F32), pltpu.VMEM((HEAD_DIM + ONES_ROWS, tq), F32)],
        compiler_params=_params(("parallel", "parallel", "arbitrary"), blk,
                                _nbytes((HEAD_DIM + ONES_ROWS, seq), BF16)
                                + 8 * _nbytes((max(ATTN_FAR_KEYS, 2 * tq), tq), F32)),
        name="diff_attention",
    )(z, z, z, near_bias, c_lambda.astype(F32), c_norm.reshape(HEAD_DIM, 1).astype(F32))


def _merge_kernel(ya_ref, yb_ref, yc_ref, wa_hbm, wb_hbm, wc_hbm, g0_ref, g1_ref, g2_ref, o_ref,
                  wa_land, wb_land, wc_land, wa_scr, wb_scr, wc_scr, sems, *, layer, tn):
    j = pl.program_id(0)
    lands = (wa_land, wb_land, wc_land)

    def copies(jj):
        cols = pl.ds(pl.multiple_of(jj * tn, LANES), tn)
        return [pltpu.make_async_copy(w.at[layer, :, cols], land.at[0], sems.at[n])
                for n, (w, land) in enumerate(zip((wa_hbm, wb_hbm, wc_hbm), lands))]

    def compute(wa, wb, wc):
        da = jnp.dot(ya_ref[...], wa, preferred_element_type=F32)
        db = jnp.dot(yb_ref[...], wb, preferred_element_type=F32)
        dc = jnp.dot(yc_ref[...], wc, preferred_element_type=F32)
        merged = (_sigmoid(g0_ref[...].astype(F32)) * da + _sigmoid(g1_ref[...].astype(F32)) * db
                  + _sigmoid(g2_ref[...].astype(F32)) * dc)
        o_ref[...] = merged.astype(o_ref.dtype)

    _stationary_step(j, pl.num_programs(0), pl.program_id(1) == 0, copies, lands, (wa_scr, wb_scr, wc_scr), compute)


def gated_merge(y_a, y_b, y_c, w_a, w_b, w_c, layer, z, gate_col0, d_model):
    T = z.shape[0]
    goff = gate_col0 * HEAD_DIM
    tn = _divisor(math.gcd(goff, d_model), (512, 256, 128))
    ka, kb, kc = y_a.shape[1], y_b.shape[1], y_c.shape[1]
    ks = ka + kb + kc

    def blk(tm):
        return _nbytes((tm, ks), BF16) + 4 * _nbytes((tm, tn), BF16)

    def scr(tm):
        return _nbytes((ks, tn), F32) + _nbytes((ks, tn), BF16) + 4 * _nbytes((tm, tn), F32)

    tm = _row_tile(T, blk, scr)

    def gspec(j):
        base = (goff + j * d_model) // tn
        return pl.BlockSpec((tm, tn), lambda n, i: (i, base + n))

    hbm = pl.BlockSpec(memory_space=pl.ANY)
    lands = [pltpu.VMEM((1, k, tn), F32) for k in (ka, kb, kc)]
    scrs = [pltpu.VMEM((k, tn), BF16) for k in (ka, kb, kc)]
    return pl.pallas_call(
        functools.partial(_merge_kernel, layer=layer, tn=tn),
        grid=(d_model // tn, T // tm),
        in_specs=[pl.BlockSpec((tm, ka), lambda n, i: (i, 0)),
                  pl.BlockSpec((tm, kb), lambda n, i: (i, 0)),
                  pl.BlockSpec((tm, kc), lambda n, i: (i, 0)),
                  hbm, hbm, hbm, gspec(0), gspec(1), gspec(2)],
        out_specs=pl.BlockSpec((tm, tn), lambda n, i: (i, n)),
        out_shape=jax.ShapeDtypeStruct((T, d_model), BF16),
        scratch_shapes=lands + scrs + [pltpu.SemaphoreType.DMA((3,))],
        compiler_params=_params(("arbitrary", "arbitrary"), blk(tm), scr(tm)),
        name="gated_merge",
    )(y_a, y_b, y_c, w_a, w_b, w_c, z, z, z)


def _route_kernel(l_ref, idx_ref, w_ref, *, n_experts):
    logits = l_ref[...]
    lane = lax.broadcasted_iota(jnp.int32, logits.shape, 1)
    logits = jnp.where(lane < n_experts, logits, -jnp.inf)
    m1 = jnp.max(logits, axis=-1, keepdims=True)
    i1 = jnp.min(jnp.where(logits == m1, lane, LANES), axis=-1, keepdims=True)
    rest = jnp.where(lane == i1, -jnp.inf, logits)
    m2 = jnp.max(rest, axis=-1, keepdims=True)
    i2 = jnp.min(jnp.where(rest == m2, lane, LANES), axis=-1, keepdims=True)
    e = jnp.exp(m2 - m1)
    w1 = 1.0 / (1.0 + e)
    w2 = e / (1.0 + e)
    idx_ref[...] = jnp.where(lane == 0, i1, jnp.where(lane == 1, i2, 0))
    w_ref[...] = jnp.where(lane == 0, w1, jnp.where(lane == 1, w2, 0.0))


def route_top2(logits, n_experts):
    T = logits.shape[0]
    tm = _divisor(T, (1024, 512, 256, 128, 8))
    spec = pl.BlockSpec((tm, LANES), lambda i: (i, 0))
    idx, w = pl.pallas_call(
        functools.partial(_route_kernel, n_experts=n_experts),
        grid=(T // tm,),
        in_specs=[spec],
        out_specs=[spec, spec],
        out_shape=[jax.ShapeDtypeStruct((T, LANES), jnp.int32), jax.ShapeDtypeStruct((T, LANES), F32)],
        compiler_params=_params(("parallel",), 3 * _nbytes((tm, LANES), F32)),
        name="route_top2",
    )(logits)
    return idx[:, :TOP_K], w


def _moe_gather_kernel(src_ref, nv_ref, x_hbm, o_ref, buf, sem, *, tg):
    i = pl.program_id(0)
    n_valid = nv_ref[0]
    slot = i % 2

    def row_copy(tile, r, sl):
        tok = src_ref[tile * tg + r]
        return pltpu.make_async_copy(x_hbm.at[pl.ds(tok, 1), :], buf.at[sl, pl.ds(r, 1), :], sem.at[sl])

    def start_tile(tile, sl):
        def body(g, carry):
            for u in range(DMA_LOOP_UNROLL):
                row_copy(tile, g * DMA_LOOP_UNROLL + u, sl).start(priority=u % 2)
            return carry
        lax.fori_loop(0, tg // DMA_LOOP_UNROLL, body, 0)

    @pl.when(i == 0)
    def _():
        start_tile(0, 0)

    @pl.when(i + 1 < n_valid)
    def _():
        start_tile(i + 1, 1 - slot)

    @pl.when(i < n_valid)
    def _():
        def body(r, carry):
            row_copy(i, r, slot).wait()
            return carry
        lax.fori_loop(0, tg, body, 0, unroll=DMA_LOOP_UNROLL)
        o_ref[...] = buf[slot].astype(o_ref.dtype)

    @pl.when(i >= n_valid)
    def _():
        o_ref[...] = jnp.zeros_like(o_ref)


def _weight_window(w_hbm, e, j, tn):
    return w_hbm.at[e, :, pl.ds(pl.multiple_of(j * tn, tn), tn)]


def _grouped_step(s, sched, w_hbms, lands, scrs, sems, tn, compute, o_ref):
    ew_ref, jw_ref, fl_ref, ne_ref, nj_ref = sched

    def copies(e, j):
        return [pltpu.make_async_copy(_weight_window(w, e, j, tn), land.at[0], sems.at[k])
                for k, (w, land) in enumerate(zip(w_hbms, lands))]

    @pl.when(fl_ref[s] == 2)
    def _():
        @pl.when(s == 0)
        def _():
            for c in copies(ew_ref[0], jw_ref[0]):
                c.start()

        for c in copies(ew_ref[s], jw_ref[s]):
            c.wait()
        ws = []
        for land, scr in zip(lands, scrs):
            w = land[0].astype(BF16)
            scr[...] = w
            ws.append(w)
        compute(*ws)

        @pl.when(ne_ref[s] >= 0)
        def _():
            for c in copies(ne_ref[s], nj_ref[s]):
                c.start()

    @pl.when(fl_ref[s] == 1)
    def _():
        compute(*[scr[...] for scr in scrs])

    @pl.when(fl_ref[s] == 0)
    def _():
        o_ref[...] = jnp.zeros_like(o_ref)


def _moe_up_kernel(t_ref, jo_ref, ew_ref, jw_ref, fl_ref, ne_ref, nj_ref, x_ref, wg_hbm, wu_hbm, o_ref,
                   wg_land, wu_land, wg_scr, wu_scr, sems, *, tf):
    def compute(wg, wu):
        x = x_ref[...]
        g = jnp.dot(x, wg, preferred_element_type=F32)
        u = jnp.dot(x, wu, preferred_element_type=F32)
        o_ref[...] = (g * _sigmoid(g) * u).astype(o_ref.dtype)

    _grouped_step(pl.program_id(0), (ew_ref, jw_ref, fl_ref, ne_ref, nj_ref), (wg_hbm, wu_hbm),
                  (wg_land, wu_land), (wg_scr, wu_scr), sems, tf, compute, o_ref)


def _moe_down_kernel(t_ref, jo_ref, ew_ref, jw_ref, fl_ref, ne_ref, nj_ref, a_ref, wd_hbm, o_ref,
                     wd_land, wd_scr, sems, *, tn):
    def compute(wd):
        o_ref[...] = jnp.dot(a_ref[...], wd, preferred_element_type=F32)

    _grouped_step(pl.program_id(0), (ew_ref, jw_ref, fl_ref, ne_ref, nj_ref), (wd_hbm,), (wd_land,), (wd_scr,),
                  sems, tn, compute, o_ref)


def _moe_schedule(tiles_per_expert, n_tiles, n_cols):
    E = tiles_per_expert.shape[0]
    cnt = jnp.concatenate([tiles_per_expert, (n_tiles - jnp.sum(tiles_per_expert))[None]]).astype(jnp.int32)
    tile0 = jnp.cumsum(cnt) - cnt
    step_end = jnp.cumsum(cnt * n_cols)
    step0 = step_end - cnt * n_cols
    s = jnp.arange(n_tiles * n_cols, dtype=jnp.int32)
    g = jnp.sum((s[:, None] >= step_end[None, :]).astype(jnp.int32), axis=1)
    within = s - step0[g]
    c = jnp.maximum(cnt[g], 1)
    col = within // c
    tile = tile0[g] + within % c
    valid = g < E
    first = jnp.logical_and(valid, within % c == 0)
    n_valid = step_end[E - 1]
    last = jnp.maximum(n_valid - 1, 0)
    w_e = jnp.where(valid, g, g[last])
    w_j = jnp.where(valid, col, col[last])
    flag = jnp.where(first, 2, jnp.where(valid, 1, 0))
    nxt = s + c
    has_next = jnp.logical_and(first, nxt < n_valid)
    nxt = jnp.minimum(nxt, n_tiles * n_cols - 1)
    n_e = jnp.where(has_next, g[nxt], -1)
    n_j = jnp.where(has_next, col[nxt], 0)
    return [a.astype(jnp.int32) for a in (tile, col, w_e, w_j, flag, n_e, n_j)]


def _moe_combine_kernel(dest_ref, y_hbm, h_ref, w_ref, *rest, tm, final_norm):
    if final_norm:
        g_ref, o_ref, buf0, buf1, sem = rest
    else:
        o_ref, buf0, buf1, sem = rest
    i = pl.program_id(0)

    def row_copy(r, k, row):
        buf = buf0 if k == 0 else buf1
        return pltpu.make_async_copy(y_hbm.at[pl.ds(row, 1), :], buf.at[pl.ds(r, 1), :], sem.at[k])

    def start(r, carry):
        a = TOP_K * (i * tm + r)
        row_copy(r, 0, dest_ref[a]).start(priority=0)
        row_copy(r, 1, dest_ref[a + 1]).start(priority=1)
        return carry

    def wait(r, carry):
        row_copy(r, 0, 0).wait()
        row_copy(r, 1, 0).wait()
        return carry

    lax.fori_loop(0, tm, start, 0, unroll=DMA_LOOP_UNROLL)
    lax.fori_loop(0, tm, wait, 0, unroll=DMA_LOOP_UNROLL)
    w = w_ref[...]
    out = h_ref[...] + (w[:, 0:1] * buf0[...] + w[:, 1:2] * buf1[...])
    if final_norm:
        out = out * lax.rsqrt(jnp.mean(out * out, axis=-1, keepdims=True) + EPS) * g_ref[...]
    o_ref[...] = out


def moe_ffn(h, hn, logits, wg, wu, wd, layer, final_gain=None):
    T, D = hn.shape
    E, F = wg.shape[1], wg.shape[3]
    idx, wts = route_top2(logits, E)
    tm = _divisor(T, (256, 128))
    A = T * TOP_K
    n_tiles = A // tm + E

    e_flat = idx.reshape(-1)
    onehot = (e_flat[:, None] == jnp.arange(E, dtype=jnp.int32)[None, :]).astype(jnp.int32)
    csum = jnp.cumsum(onehot, axis=0)
    pos = jnp.take_along_axis(csum, e_flat[:, None], axis=1)[:, 0] - 1
    counts = csum[-1]
    padded = ((counts + tm - 1) // tm) * tm
    ends = jnp.cumsum(padded)
    starts = ends - padded
    dest = (starts[e_flat] + pos).astype(jnp.int32)
    src = jnp.zeros((n_tiles * tm,), jnp.int32).at[dest].set(jnp.arange(A, dtype=jnp.int32) // TOP_K)
    tiles_per_expert = padded // tm
    n_valid_tiles = jnp.sum(tiles_per_expert).astype(jnp.int32).reshape(1)

    x_sorted = pl.pallas_call(
        functools.partial(_moe_gather_kernel, tg=tm),
        grid_spec=pltpu.PrefetchScalarGridSpec(
            num_scalar_prefetch=2,
            grid=(n_tiles,),
            in_specs=[pl.BlockSpec(memory_space=pl.ANY)],
            out_specs=pl.BlockSpec((tm, D), lambda i, sr, nv: (i, 0)),
            scratch_shapes=[pltpu.VMEM((2, tm, D), F32), pltpu.SemaphoreType.DMA((2,))],
        ),
        out_shape=jax.ShapeDtypeStruct((n_tiles * tm, D), BF16),
        compiler_params=_params(("arbitrary",), _nbytes((tm, D), BF16), 3 * _nbytes((tm, D), F32)),
        name="moe_gather",
    )(src, n_valid_tiles, hn)

    def experts(w):
        return w.reshape((-1,) + w.shape[2:])

    tf = _divisor(F, (512, 256, 128))
    sched = _moe_schedule(tiles_per_expert, n_tiles, F // tf)
    sched[2] = sched[2] + layer * E
    sched[5] = jnp.where(sched[5] >= 0, sched[5] + layer * E, -1)
    hbm = pl.BlockSpec(memory_space=pl.ANY)
    act = pl.pallas_call(
        functools.partial(_moe_up_kernel, tf=tf),
        grid_spec=pltpu.PrefetchScalarGridSpec(
            num_scalar_prefetch=7,
            grid=(n_tiles * (F // tf),),
            in_specs=[pl.BlockSpec((tm, D), lambda s, t, jo, *_: (t[s], 0)), hbm, hbm],
            out_specs=pl.BlockSpec((tm, tf), lambda s, t, jo, *_: (t[s], jo[s])),
            scratch_shapes=[pltpu.VMEM((1, D, tf), F32), pltpu.VMEM((1, D, tf), F32),
                            pltpu.VMEM((D, tf), BF16), pltpu.VMEM((D, tf), BF16), pltpu.SemaphoreType.DMA((2,))],
        ),
        out_shape=jax.ShapeDtypeStruct((n_tiles * tm, F), BF16),
        compiler_params=_params(("arbitrary",), _nbytes((tm, D), BF16) + _nbytes((tm, tf), BF16),
                                2 * _nbytes((D, tf), F32) + 2 * _nbytes((D, tf), BF16) + 3 * _nbytes((tm, tf), F32)),
        name="moe_up",
    )(*sched, x_sorted, experts(wg), experts(wu))

    tn = _divisor(D, (1024, 512, 256, 128))
    sched = _moe_schedule(tiles_per_expert, n_tiles, D // tn)
    sched[2] = sched[2] + layer * E
    sched[5] = jnp.where(sched[5] >= 0, sched[5] + layer * E, -1)
    y_sorted = pl.pallas_call(
        functools.partial(_moe_down_kernel, tn=tn),
        grid_spec=pltpu.PrefetchScalarGridSpec(
            num_scalar_prefetch=7,
            grid=(n_tiles * (D // tn),),
            in_specs=[pl.BlockSpec((tm, F), lambda s, t, jo, *_: (t[s], 0)), hbm],
            out_specs=pl.BlockSpec((tm, tn), lambda s, t, jo, *_: (t[s], jo[s])),
            scratch_shapes=[pltpu.VMEM((1, F, tn), F32), pltpu.VMEM((F, tn), BF16), pltpu.SemaphoreType.DMA((1,))],
        ),
        out_shape=jax.ShapeDtypeStruct((n_tiles * tm, D), F32),
        compiler_params=_params(("arbitrary",), _nbytes((tm, F), BF16) + _nbytes((tm, tn), F32),
                                _nbytes((F, tn), F32) + _nbytes((F, tn), BF16) + _nbytes((tm, tn), F32)),
        name="moe_down",
    )(*sched, act, experts(wd))

    tc = _divisor(T, (256, 128))
    blk_c = 2 * _nbytes((tc, D), F32) + _nbytes((tc, LANES), F32)
    in_specs = [hbm, pl.BlockSpec((tc, D), lambda i, de: (i, 0)), pl.BlockSpec((tc, LANES), lambda i, de: (i, 0))]
    args = [dest, y_sorted, h, wts]
    if final_gain is not None:
        in_specs.append(pl.BlockSpec((1, D), lambda i, de: (0, 0)))
        args.append(final_gain.reshape(1, D).astype(F32))
    return pl.pallas_call(
        functools.partial(_moe_combine_kernel, tm=tc, final_norm=final_gain is not None),
        grid_spec=pltpu.PrefetchScalarGridSpec(
            num_scalar_prefetch=1,
            grid=(T // tc,),
            in_specs=in_specs,
            out_specs=pl.BlockSpec((tc, D), lambda i, de: (i, 0)),
            scratch_shapes=[pltpu.VMEM((tc, D), F32), pltpu.VMEM((tc, D), F32), pltpu.SemaphoreType.DMA((2,))],
        ),
        out_shape=jax.ShapeDtypeStruct((T, D), F32),
        compiler_params=_params(("arbitrary",), blk_c, 3 * _nbytes((tc, D), F32)),
        name="moe_combine",
    )(*args)


def _mixer(h, li, batch, seq, near_bias, tq, norm_mix, w_in, a_gate_bias, a_conv, a_norm, b_ln_g, b_ln_b, b_ws,
           b_bs, c_lambda, c_norm, w_br_a, w_br_b, w_br_c, w_out):
    d_model = h.shape[1]
    h_a = a_gate_bias.shape[1] // 2
    g_b = b_ws.shape[1]
    h_c = w_br_c.shape[1] // HEAD_DIM
    w_a = h_a * HEAD_DIM
    g0 = 4 * w_a
    g1 = g0 + 2 * h_a
    n_rest = w_in.shape[2] - g1
    w_in_t = jnp.swapaxes(w_in, 1, 2)
    n, zg = rmsnorm(h, norm_mix[li], proj=jnp.transpose(w_in_t[li, g0:g1, :]))
    z_a = matmul(n, w_in_t, li, BF16, n=g0, transposed=True)
    z_r = matmul(n, w_in_t, li, BF16, w_col=g1, n=n_rest, transposed=True)
    y_a = mlstm_branch(z_a, zg, a_gate_bias[li], a_conv[li], a_norm[li], batch, seq, h_a)
    y_b = sgu_branch(z_r, b_ln_g[li], b_ln_b[li], b_ws[li], b_bs[li], g_b)
    lam_init = 0.8 - 0.6 * math.exp(-0.3 * li)
    y_c = attn_branch(z_r, 2 * g_b, near_bias, c_lambda[li], c_norm[li], batch, seq, h_c, lam_init, tq)
    merged = gated_merge(y_a, y_b, y_c, w_br_a, w_br_b, w_br_c, li, z_r, 2 * g_b + 3 * h_c, d_model)
    return matmul_residual(merged, w_out, li, h)


def kernel(x, norm_mix, w_in, a_gate_bias, a_conv, a_norm, b_ln_g, b_ln_b, b_ws, b_bs, c_lambda, c_norm, rel_bias,
           w_br_a, w_br_b, w_br_c, w_out, norm_ffn, ffn_wg, ffn_wu, ffn_wd, router, moe_wg, moe_wu, moe_wd,
           final_norm):
    batch, seq, d_model = x.shape
    depth = w_in.shape[0]
    tq = _divisor(seq, (512, 256, 128))
    near_bias = attn_bias_tiles(rel_bias, tq)
    h = x.reshape(batch * seq, d_model).astype(F32)
    for li in range(depth):
        h = _mixer(h, li, batch, seq, near_bias, tq, norm_mix, w_in, a_gate_bias, a_conv, a_norm, b_ln_g, b_ln_b,
                   b_ws, b_bs, c_lambda, c_norm, w_br_a, w_br_b, w_br_c, w_out)
        j = li // 2
        fused_final = False
        if li % 2 == 0:
            hn = rmsnorm(h, norm_ffn[li])
            act = swiglu_up(hn, ffn_wg, ffn_wu, j)
            h = matmul_residual(act, ffn_wd, j, h)
        else:
            hn, logits = rmsnorm(h, norm_ffn[li], proj=router[j], out_dtype=F32)
            fused_final = li == depth - 1
            h = moe_ffn(h, hn, logits, moe_wg, moe_wu, moe_wd, j, final_norm if fused_final else None)
    out = h if fused_final else rmsnorm(h, final_norm, out_dtype=F32)
    return out.reshape(batch, seq, d_model).astype(x.dtype)
```

```python
import functools
import itertools
import math

import numpy as np
import jax
import jax.numpy as jnp
from jax import lax
from jax.experimental import pallas as pl
from jax.experimental.pallas import tpu as pltpu

F32 = jnp.float32
BF16 = jnp.bfloat16

EPS = 1e-6
HEAD_DIM = 128
CHUNK = 128
CONV_W = 4
N_BUCKETS = 32
MAX_DIST = 128
TOP_K = 2
LANES = 128
VMEM_BYTES_V7X = 64 * 1024 * 1024
VMEM_SLACK = 6 * 1024 * 1024
VMEM_BUDGET = VMEM_BYTES_V7X - 8 * 1024 * 1024
WEIGHT_STAGE_ROWS = 256
DMA_LOOP_UNROLL = 8


def _divisor(n, candidates):
    for c in candidates:
        if n % c == 0:
            return c
    raise ValueError(f"no tile size in {candidates} divides {n}")


def _vmem_need(block_bytes, scratch_bytes):
    return 2 * block_bytes + scratch_bytes + VMEM_SLACK


def _params(semantics, block_bytes, scratch_bytes=0):
    limit = int(min(max(_vmem_need(block_bytes, scratch_bytes), 16 * 1024 * 1024), VMEM_BYTES_V7X - 4 * 1024 * 1024))
    return pltpu.CompilerParams(dimension_semantics=semantics, vmem_limit_bytes=limit)


def _row_tile(T, block_bytes, scratch_bytes):
    for tm in (1024, 512, 256, 128):
        if T % tm == 0 and _vmem_need(block_bytes(tm), scratch_bytes(tm)) <= VMEM_BUDGET:
            return tm
    raise ValueError("no row tile fits VMEM")


def _nbytes(shape, dtype):
    return int(np.prod(shape)) * jnp.dtype(dtype).itemsize


def _sigmoid(x):
    return 1.0 / (1.0 + jnp.exp(-x))


def _split_bf16(x):
    hi = x.astype(BF16)
    lo = (x - hi.astype(F32)).astype(BF16)
    return hi, lo


def _dot_nt(a, b):
    return lax.dot_general(a, b, (((1,), (1,)), ((), ())), preferred_element_type=F32)


def _dot_tn(a, b):
    return lax.dot_general(a, b, (((0,), (0,)), ((), ())), preferred_element_type=F32)


def _norm_kernel(*refs, has_proj, out_dtype):
    if has_proj:
        x_ref, g_ref, w_ref, y_ref, p_ref = refs
    else:
        x_ref, g_ref, y_ref = refs
    x = x_ref[...]
    y = x * lax.rsqrt(jnp.mean(x * x, axis=-1, keepdims=True) + EPS) * g_ref[...]
    y_ref[...] = y.astype(out_dtype)
    if has_proj:
        y_hi, y_lo = _split_bf16(y)
        w = w_ref[...]
        both = jnp.dot(y_hi, w, preferred_element_type=F32)
        p_ref[...] = (both[:, :LANES] + both[:, LANES:]
                      + jnp.dot(y_lo, w[:, :LANES], preferred_element_type=F32))


def rmsnorm(x, g, proj=None, out_dtype=BF16):
    T, D = x.shape
    tm = _divisor(T, (512, 256, 128, 8))
    g2 = g.reshape(1, D).astype(F32)
    in_specs = [pl.BlockSpec((tm, D), lambda i: (i, 0)), pl.BlockSpec((1, D), lambda i: (0, 0))]
    out_shape = [jax.ShapeDtypeStruct((T, D), out_dtype)]
    out_specs = [pl.BlockSpec((tm, D), lambda i: (i, 0))]
    args = [x, g2]
    blk = _nbytes((tm, D), F32) + _nbytes((tm, D), out_dtype)
    if proj is not None:
        n = proj.shape[1]
        assert n <= LANES
        w = jnp.pad(proj.astype(F32), ((0, 0), (0, LANES - n)))
        in_specs.append(pl.BlockSpec((D, 2 * LANES), lambda i: (0, 0)))
        out_shape.append(jax.ShapeDtypeStruct((T, LANES), F32))
        out_specs.append(pl.BlockSpec((tm, LANES), lambda i: (i, 0)))
        args.append(jnp.concatenate(_split_bf16(w), axis=1))
        blk += _nbytes((D, 2 * LANES), BF16)
    outs = pl.pallas_call(
        functools.partial(_norm_kernel, has_proj=proj is not None, out_dtype=out_dtype),
        grid=(T // tm,),
        in_specs=in_specs,
        out_specs=out_specs,
        out_shape=out_shape,
        compiler_params=_params(("parallel",), blk, 2 * _nbytes((tm, D), F32)),
        name="rmsnorm",
    )(*args)
    return (outs[0], outs[1]) if proj is not None else outs[0]


def _stage_weight(w_ref, w_scr):
    R = w_scr.shape[0]
    rc = _divisor(R, (WEIGHT_STAGE_ROWS, LANES))

    def step(r, carry):
        rows = pl.ds(pl.multiple_of(r * rc, rc), rc)
        w_scr[rows, :] = w_ref[0, rows, :].astype(BF16)
        return carry

    lax.fori_loop(0, R // rc, step, 0)


def _stationary_weights(j, nj, copies, lands, scrs):
    @pl.when(j == 0)
    def _():
        for c in copies(0):
            c.start()

    for c in copies(j):
        c.wait()
    for land, scr in zip(lands, scrs):
        _stage_weight(land, scr)

    @pl.when(j + 1 < nj)
    def _():
        for c in copies(j + 1):
            c.start()


def _matmul_kernel(*refs, layer, w_row, w_col, k, tn, transposed, has_res):
    a_ref, w_hbm = refs[:2]
    r_ref = refs[2] if has_res else None
    o_ref, w_land, w_scr, sem = refs[-4:]
    j = pl.program_id(0)
    nj = pl.num_programs(0)

    def window(jj):
        if transposed:
            src = w_hbm.at[layer, pl.ds(pl.multiple_of(w_col + jj * tn, 8), tn), pl.ds(w_row, k)]
        else:
            src = w_hbm.at[layer, pl.ds(w_row, k), pl.ds(pl.multiple_of(w_col + jj * tn, LANES), tn)]
        return pltpu.make_async_copy(src, w_land.at[0], sem)

    @pl.when(pl.program_id(1) == 0)
    def _():
        _stationary_weights(j, nj, lambda jj: [window(jj)], (w_land,), (w_scr,))

    if transposed:
        d = _dot_nt(a_ref[...], w_scr[...])
    else:
        d = jnp.dot(a_ref[...], w_scr[...], preferred_element_type=F32)
    if has_res:
        d = d + r_ref[...]
    o_ref[...] = d.astype(o_ref.dtype)


def matmul(a, w, layer, out_dtype, *, a_col=0, k=None, w_row=0, w_col=0, n=None, transposed=False, res=None):
    T = a.shape[0]
    k = a.shape[1] if k is None else k
    n_total = w.shape[1] if transposed else w.shape[2]
    n = n_total - w_col if n is None else n
    tn = _divisor(n, (512, 256, 128))
    assert a_col % k == 0 and w_col % (8 if transposed else LANES) == 0
    wshape = (tn, k) if transposed else (k, tn)
    res_b = (lambda tm: _nbytes((tm, tn), F32)) if res is not None else (lambda tm: 0)

    def blk(tm):
        return _nbytes((tm, k), BF16) + _nbytes((tm, tn), out_dtype) + res_b(tm)

    def scr(tm):
        return _nbytes(wshape, F32) + _nbytes(wshape, BF16) + 2 * _nbytes((tm, tn), F32)

    tm = _row_tile(T, blk, scr)
    ab = a_col // k
    in_specs = [pl.BlockSpec((tm, k), lambda j, i: (i, ab)), pl.BlockSpec(memory_space=pl.ANY)]
    args = [a, w]
    if res is not None:
        in_specs.append(pl.BlockSpec((tm, tn), lambda j, i: (i, j)))
        args.append(res)
    return pl.pallas_call(
        functools.partial(_matmul_kernel, layer=layer, w_row=w_row, w_col=w_col, k=k, tn=tn, transposed=transposed,
                          has_res=res is not None),
        grid=(n // tn, T // tm),
        in_specs=in_specs,
        out_specs=pl.BlockSpec((tm, tn), lambda j, i: (i, j)),
        out_shape=jax.ShapeDtypeStruct((T, n), out_dtype),
        scratch_shapes=[pltpu.VMEM((1,) + wshape, F32), pltpu.VMEM(wshape, BF16), pltpu.SemaphoreType.DMA(())],
        compiler_params=_params(("arbitrary", "arbitrary"), blk(tm), scr(tm)),
        name="matmul",
    )(*args)


def matmul_residual(a, w, layer, res):
    K = a.shape[1]
    nk = next(n for n in range(1, K // LANES + 1) if K % (n * LANES) == 0 and K // n <= 6144)
    k = K // nk
    out = res
    for p in range(nk):
        out = matmul(a, w, layer, F32, a_col=p * k, k=k, w_row=p * k, res=out)
    return out


def _swiglu_up_kernel(x_ref, wg_hbm, wu_hbm, o_ref, wg_land, wu_land, wg_scr, wu_scr, sems, *, layer, tn):
    j = pl.program_id(0)

    def copies(jj):
        cols = pl.ds(pl.multiple_of(jj * tn, LANES), tn)
        return [pltpu.make_async_copy(w.at[layer, :, cols], land.at[0], sems.at[n])
                for n, (w, land) in enumerate(((wg_hbm, wg_land), (wu_hbm, wu_land)))]

    @pl.when(pl.program_id(1) == 0)
    def _():
        _stationary_weights(j, pl.num_programs(0), copies, (wg_land, wu_land), (wg_scr, wu_scr))

    x = x_ref[...]
    g = jnp.dot(x, wg_scr[...], preferred_element_type=F32)
    u = jnp.dot(x, wu_scr[...], preferred_element_type=F32)
    o_ref[...] = (g * _sigmoid(g) * u).astype(o_ref.dtype)


def swiglu_up(x, wg, wu, layer):
    T, K = x.shape
    F = wg.shape[2]
    tn = _divisor(F, (512, 256, 128))

    def blk(tm):
        return _nbytes((tm, K), BF16) + _nbytes((tm, tn), BF16)

    def scr(tm):
        return 2 * _nbytes((K, tn), F32) + 2 * _nbytes((K, tn), BF16) + 3 * _nbytes((tm, tn), F32)

    tm = _row_tile(T, blk, scr)
    hbm = pl.BlockSpec(memory_space=pl.ANY)
    return pl.pallas_call(
        functools.partial(_swiglu_up_kernel, layer=layer, tn=tn),
        grid=(F // tn, T // tm),
        in_specs=[pl.BlockSpec((tm, K), lambda j, i: (i, 0)), hbm, hbm],
        out_specs=pl.BlockSpec((tm, tn), lambda j, i: (i, j)),
        out_shape=jax.ShapeDtypeStruct((T, F), BF16),
        scratch_shapes=[pltpu.VMEM((1, K, tn), F32), pltpu.VMEM((1, K, tn), F32),
                        pltpu.VMEM((K, tn), BF16), pltpu.VMEM((K, tn), BF16), pltpu.SemaphoreType.DMA((2,))],
        compiler_params=_params(("arbitrary", "arbitrary"), blk(tm), scr(tm)),
        name="swiglu_up",
    )(x, wg, wu)


def _split_bf16_f32(x):
    hi = x.astype(BF16)
    return hi, x - hi.astype(F32)


def _mlstm_kernel(gb_ref, q_ref, k_ref, v_ref, og_ref, ig_ref, fg_ref, cwq_ref, cwk_ref, an_ref, y_ref,
                  c_scr, n_scr, m_scr, pq_scr, pk_scr, *, n_heads, n_chunks, group):
    hg = pl.program_id(1)
    L = CHUNK
    c_scr[...] = jnp.zeros_like(c_scr)
    n_scr[...] = jnp.zeros_like(n_scr)
    m_scr[...] = jnp.zeros_like(m_scr)
    pq_scr[...] = jnp.zeros_like(pq_scr)
    pk_scr[...] = jnp.zeros_like(pk_scr)
    ri = lax.broadcasted_iota(jnp.int32, (L, L), 0)
    ci = lax.broadcasted_iota(jnp.int32, (L, L), 1)
    lower = ci <= ri
    strict_lower01 = jnp.where(ri > ci, 1.0, 0.0).astype(BF16)

    def conv_silu(x, prev, w):
        y = x * w[CONV_W - 1:CONV_W, :]
        for s in range(1, CONV_W):
            shifted = jnp.where(ri < s, pltpu.roll(prev, s, 0), pltpu.roll(x, s, 0))
            y = y + shifted * w[CONV_W - 1 - s:CONV_W - s, :]
        return y * _sigmoid(y)

    def head_chunk(c, rows, g):
        cols = slice(g * HEAD_DIM, (g + 1) * HEAD_DIM)
        h = hg * group + g
        xq = q_ref[rows, cols].astype(F32)
        xk = k_ref[rows, cols].astype(F32)
        q = conv_silu(xq, pq_scr[g], cwq_ref[:, cols])
        k = conv_silu(xk, pk_scr[g], cwk_ref[:, cols]) * (HEAD_DIM ** -0.5)
        pq_scr[g] = xq
        pk_scr[g] = xk
        v = v_ref[rows, cols]
        q_bf = q.astype(BF16)
        k_bf = k.astype(BF16)
        yield

        i_row = ig_ref[0, g, pl.ds(c, 1), :] + gb_ref[h]
        f_row = fg_ref[0, g, pl.ds(c, 1), :] + gb_ref[n_heads + h]
        lf = jnp.minimum(f_row, 0.0) - jnp.log(1.0 + jnp.exp(-jnp.abs(f_row)))
        lf_low = jnp.where(lower, lf, 0.0)
        b_col = jnp.sum(lf_low, axis=1, keepdims=True)
        hi, rest = _split_bf16_f32(lf_low)
        mid, lo = _split_bf16_f32(rest)
        yield
        dmat = (jnp.dot(hi, strict_lower01, preferred_element_type=F32)
                + jnp.dot(mid, strict_lower01, preferred_element_type=F32)
                + jnp.dot(lo.astype(BF16), strict_lower01, preferred_element_type=F32))
        g_tot = b_col[L - 1:L, :]
        a_row = dmat[L - 1:L, :] + i_row
        a_col = jnp.sum(jnp.where(ci > ri, lf, 0.0) + jnp.where(ci == ri, i_row, 0.0),
                        axis=1, keepdims=True)
        yield

        m_prev = m_scr[g]
        c_prev = c_scr[g]
        n_prev = n_scr[g]

        log_d = jnp.where(lower, dmat + i_row, -jnp.inf)
        log_inter = b_col + m_prev
        m_t = jnp.maximum(log_inter, jnp.max(log_d, axis=1, keepdims=True))
        qk = _dot_nt(q_bf, k_bf)
        yield
        w = jnp.exp(log_d - m_t) * qk
        e_inter = jnp.exp(log_inter - m_t)
        yield
        num = (e_inter * jnp.dot(q_bf, c_prev.astype(BF16), preferred_element_type=F32)
               + jnp.dot(w.astype(BF16), v, preferred_element_type=F32))
        den = e_inter * jnp.sum(q * n_prev, axis=1, keepdims=True) + jnp.sum(w, axis=1, keepdims=True)
        yield
        hcell = num / jnp.maximum(jnp.abs(den), jnp.exp(-m_t))

        gated = _sigmoid(og_ref[rows, cols].astype(F32)) * hcell
        y = gated * lax.rsqrt(jnp.mean(gated * gated, axis=-1, keepdims=True) + EPS) * an_ref[:, cols]
        y_ref[rows, cols] = y.astype(y_ref.dtype)
        yield

        m_new =jnp.maximum(g_tot + m_prev, jnp.max(a_row, axis=1, keepdims=True))
        w_col = jnp.exp(a_col - m_new)
        decay = jnp.exp(g_tot + m_prev - m_new)
        c_scr[g] = decay * c_prev + _dot_tn(k_bf, (v.astype(F32) * w_col).astype(BF16))
        n_scr[g] = decay * n_prev + jnp.sum(k * w_col, axis=0, keepdims=True)
        m_scr[g] = m_new

    def chunk(c, carry):
        rows = pl.ds(pl.multiple_of(c * L, L), L)
        stages = [head_chunk(c, rows, g) for g in range(group)]
        for _ in itertools.zip_longest(*stages):
            pass
        return carry

    lax.fori_loop(0, n_chunks, chunk, 0)


def mlstm_branch(z, zg, a_gate_bias, a_conv, a_norm, batch, seq, n_heads):
    T = batch * seq
    nc = seq // CHUNK
    W = n_heads * HEAD_DIM
    group = _divisor(n_heads, (2, 1))
    ng = n_heads // group
    gw = group * HEAD_DIM

    def gate_rows(g):
        return jnp.transpose(g.reshape(batch, seq, n_heads), (0, 2, 1)).reshape(batch, n_heads, nc, CHUNK)

    ig = gate_rows(zg[:, :n_heads])
    fg = gate_rows(zg[:, n_heads:2 * n_heads])

    def zspec(part):
        return pl.BlockSpec((seq, gw), lambda b, h, gb: (b, part * ng + h))

    gspec = pl.BlockSpec((1, group, nc, CHUNK), lambda b, h, gb: (b, h, 0, 0))
    blk = 5 * _nbytes((seq, gw), BF16) + 2 * _nbytes((group, nc, CHUNK), F32)
    grid_spec = pltpu.PrefetchScalarGridSpec(
        num_scalar_prefetch=1,
        grid=(batch, ng),
        in_specs=[zspec(0), zspec(1), zspec(2), zspec(3), gspec, gspec,
                  pl.BlockSpec((CONV_W, gw), lambda b, h, gb: (0, h)),
                  pl.BlockSpec((CONV_W, gw), lambda b, h, gb: (0, ng + h)),
                  pl.BlockSpec((1, gw), lambda b, h, gb: (0, h))],
        out_specs=pl.BlockSpec((seq, gw), lambda b, h, gb: (b, h)),
        scratch_shapes=[pltpu.VMEM((group, HEAD_DIM, HEAD_DIM), F32), pltpu.VMEM((group, 1, HEAD_DIM), F32),
                        pltpu.VMEM((group, 1, 1), F32), pltpu.VMEM((group, CHUNK, HEAD_DIM), F32),
                        pltpu.VMEM((group, CHUNK, HEAD_DIM), F32)],
    )
    return pl.pallas_call(
        functools.partial(_mlstm_kernel, n_heads=n_heads, n_chunks=nc, group=group),
        grid_spec=grid_spec,
        out_shape=jax.ShapeDtypeStruct((T, W), BF16),
        compiler_params=_params(("parallel", "parallel"), blk, 64 * group * _nbytes((CHUNK, HEAD_DIM), F32)),
        name="mlstm",
    )(a_gate_bias.astype(F32), z, z, z, z, ig, fg, a_conv.astype(F32), a_conv.astype(F32),
      a_norm.reshape(1, W).astype(F32))


def _sgu_kernel(u_ref, v_ref, lng_ref, lnb_ref, ws_ref, bst_ref, y_ref, *, n_groups, chunks_per_block):
    L = CHUNK
    c0 = math.sqrt(2.0 / math.pi)

    def gelu(x):
        return x * (0.5 * (1.0 + jnp.tanh(c0 * (x + 0.044715 * (x * x * x)))))

    v = gelu(v_ref[...].astype(F32))
    mu = jnp.mean(v, axis=-1, keepdims=True)
    vc = v - mu
    vn = vc * lax.rsqrt(jnp.mean(vc * vc, axis=-1, keepdims=True) + EPS) * lng_ref[...] + lnb_ref[...]
    vn = vn.astype(BF16)
    ri = lax.broadcasted_iota(jnp.int32, (L, L), 0)
    ci = lax.broadcasted_iota(jnp.int32, (L, L), 1)
    bst = bst_ref[...]
    for g in range(n_groups):
        cs = slice(g * HEAD_DIM, (g + 1) * HEAD_DIM)
        w = jnp.where(ci <= ri, ws_ref[g], 0.0).astype(BF16)
        bias = bst[:, g:g + 1]
        for c in range(chunks_per_block):
            rs = slice(c * L, (c + 1) * L)
            s = jnp.dot(w, vn[rs, cs], preferred_element_type=F32) + bias
            y_ref[rs, cs] = (gelu(u_ref[rs, cs].astype(F32)) * s).astype(y_ref.dtype)


def sgu_branch(z, b_ln_g, b_ln_b, b_ws, b_bs, n_groups):
    T = z.shape[0]
    W = n_groups * HEAD_DIM
    cpb = _divisor(T // CHUNK, (4, 2, 1))
    R = cpb * CHUNK
    blk = 3 * _nbytes((R, W), BF16) + _nbytes((n_groups, CHUNK, CHUNK), F32)
    return pl.pallas_call(
        functools.partial(_sgu_kernel, n_groups=n_groups, chunks_per_block=cpb),
        grid=(T // R,),
        in_specs=[pl.BlockSpec((R, W), lambda i: (i, 0)),
                  pl.BlockSpec((R, W), lambda i: (i, 1)),
                  pl.BlockSpec((1, W), lambda i: (0, 0)),
                  pl.BlockSpec((1, W), lambda i: (0, 0)),
                  pl.BlockSpec((n_groups, CHUNK, CHUNK), lambda i: (0, 0, 0)),
                  pl.BlockSpec((CHUNK, n_groups), lambda i: (0, 0))],
        out_specs=pl.BlockSpec((R, W), lambda i: (i, 0)),
        out_shape=jax.ShapeDtypeStruct((T, W), BF16),
        compiler_params=_params(("parallel",), blk, 6 * _nbytes((R, W), F32)),
        name="spatial_gating",
    )(z, z, b_ln_g.reshape(1, W).astype(F32), b_ln_b.reshape(1, W).astype(F32), b_ws.astype(F32),
      jnp.transpose(b_bs).astype(F32))


def _t5_bucket_table(n):
    d = np.arange(n, dtype=np.int64)
    max_exact = N_BUCKETS // 2
    nf = np.maximum(d, 1).astype(np.float32)
    scaled = (np.log(nf / np.float32(max_exact)) / np.float32(math.log(MAX_DIST / max_exact))
              * np.float32(N_BUCKETS - max_exact))
    large = np.minimum(max_exact + scaled.astype(np.int32), N_BUCKETS - 1)
    return np.where(d < max_exact, d, large).astype(np.int32)


def _bias_kernel(rb_ref, bkt_ref, o_ref, *, n_heads):
    h = pl.program_id(0)
    bkt = bkt_ref[...]
    far = rb_ref[(N_BUCKETS - 1) * n_heads + h]
    out = jnp.full(bkt.shape, -jnp.inf, F32)
    for b in range(N_BUCKETS):
        out = jnp.where(bkt == b, rb_ref[b * n_heads + h] - far, out)
    o_ref[0] = out


def attn_bias_tiles(rel_bias, tq):
    n_heads = rel_bias.shape[1]
    assert tq >= MAX_DIST
    c = np.arange(2 * tq)[:, None]
    r = np.arange(tq)[None, :]
    dist = r - c + tq
    table = _t5_bucket_table(2 * tq)
    bkt = np.where(dist >= 0, table[np.maximum(dist, 0)], -1).astype(np.int32)
    grid_spec = pltpu.PrefetchScalarGridSpec(
        num_scalar_prefetch=1,
        grid=(n_heads,),
        in_specs=[pl.BlockSpec((2 * tq, tq), lambda h, rb: (0, 0))],
        out_specs=pl.BlockSpec((1, 2 * tq, tq), lambda h, rb: (h, 0, 0)),
    )
    return pl.pallas_call(
        functools.partial(_bias_kernel, n_heads=n_heads),
        grid_spec=grid_spec,
        out_shape=jax.ShapeDtypeStruct((n_heads, 2 * tq, tq), F32),
        compiler_params=_params(("arbitrary",), 2 * _nbytes((2 * tq, tq), F32)),
        name="attn_bias_tiles",
    )(rel_bias.astype(F32).reshape(-1), jnp.asarray(bkt))


ONES_ROWS = 16
ATTN_FAR_KEYS = 1024


def _attn_kernel(q_ref, k_ref, v_ref, nb_ref, lam_ref, cn_ref, y_ref,
                 vt_scr, m1_scr, a1_scr, m2_scr, a2_scr, *, tq, seq, lam_init):
    qi = pl.program_id(2)
    dk = HEAD_DIM // 2
    dv = HEAD_DIM
    scale = dk ** -0.5

    @pl.when(qi == 0)
    def _():
        for c in range(seq // tq):
            cs = slice(c * tq, (c + 1) * tq)
            vt_scr[:dv, cs] = v_ref[cs, :].astype(F32).T.astype(BF16)
        vt_scr[dv:, :] = jnp.ones((ONES_ROWS, seq), BF16)

    q = (q_ref[...].astype(F32) * scale).astype(BF16)
    lane = lax.broadcasted_iota(jnp.int32, q.shape, 1)
    zero = jnp.zeros_like(q)
    q1 = jnp.where(lane < dk, q, zero)
    q2 = jnp.where(lane >= dk, q, zero)
    for m_scr, a_scr in ((m1_scr, a1_scr), (m2_scr, a2_scr)):
        m_scr[...] = jnp.full(m_scr.shape, -jnp.inf, F32)
        a_scr[...] = jnp.zeros_like(a_scr)

    def update(s, vtb, m_scr, a_scr):
        m_prev = m_scr[...]
        sb = s.astype(BF16)
        m_new = jnp.maximum(m_prev, jnp.max(sb, axis=0, keepdims=True).astype(F32))
        alpha = jnp.exp(m_prev - m_new)
        p = jnp.exp(sb - m_new.astype(BF16))
        a_scr[...] = alpha * a_scr[...] + jnp.dot(vtb, p, preferred_element_type=F32)
        m_scr[...] = m_new

    def block(kstart, nk, bias):
        rows = pl.ds(pl.multiple_of(kstart, tq), nk)
        kb = k_ref[rows, :]
        vtb = vt_scr[:, rows]
        s1 = _dot_nt(kb, q1)
        s2 = _dot_nt(kb, q2)
        if bias is not None:
            s1 = s1 + bias
            s2 = s2 + bias
        update(s1, vtb, m1_scr, a1_scr)
        update(s2, vtb, m2_scr, a2_scr)

    big = max(ATTN_FAR_KEYS // tq, 1)
    n_far = jnp.maximum(qi - 1, 0)
    n_big = n_far // big

    def far_block(j, carry):
        block(j * (big * tq), big * tq, None)
        return carry

    lax.fori_loop(0, n_big, far_block, 0)
    rem = n_far - n_big * big
    unit = big // 2
    while unit >= 1:
        done = n_big * big + (rem // (2 * unit)) * (2 * unit)

        @pl.when((rem // unit) % 2 == 1)
        def _(done=done, unit=unit):
            block(done * tq, unit * tq, None)

        unit //= 2

    @pl.when(qi > 0)
    def _():
        block((qi - 1) * tq, 2 * tq, nb_ref[0])

    @pl.when(qi == 0)
    def _():
        block(0, tq, nb_ref[0, tq:, :])

    lf = lam_ref[...]
    lam = (jnp.exp(jnp.sum(lf[0:1] * lf[1:2], axis=-1, keepdims=True))
           - jnp.exp(jnp.sum(lf[2:3] * lf[3:4], axis=-1, keepdims=True)) + lam_init)
    a1 = a1_scr[...]
    a2 = a2_scr[...]
    o = a1[:dv] / a1[dv:dv + 1] - lam * (a2[:dv] / a2[dv:dv + 1])
    o = o * lax.rsqrt(jnp.mean(o * o, axis=0, keepdims=True) + EPS) * cn_ref[...] * (1.0 - lam_init)
    y_ref[...] = o.T.astype(y_ref.dtype)


def attn_branch(z, col0, near_bias, c_lambda, c_norm, batch, seq, n_heads, lam_init, tq):
    T = batch * seq
    nq = seq // tq
    W = n_heads * HEAD_DIM
    blk = (2 * _nbytes((tq, HEAD_DIM), BF16) + 2 * _nbytes((seq, HEAD_DIM), BF16)
           + _nbytes((tq, 2 * tq), F32))
    return pl.pallas_call(
        functools.partial(_attn_kernel, tq=tq, seq=seq, lam_init=lam_init),
        grid=(batch, n_heads, nq),
        in_specs=[pl.BlockSpec((tq, HEAD_DIM), lambda b, h, i: (b * nq + i, col0 + h)),
                  pl.BlockSpec((seq, HEAD_DIM), lambda b, h, i: (b, col0 + n_heads + h)),
                  pl.BlockSpec((seq, HEAD_DIM), lambda b, h, i: (b, col0 + 2 * n_heads + h)),
                  pl.BlockSpec((1, 2 * tq, tq), lambda b, h, i: (h, 0, 0)),
                  pl.BlockSpec(c_lambda.shape, lambda b, h, i: (0, 0)),
                  pl.BlockSpec((HEAD_DIM, 1), lambda b, h, i: (0, 0))],
        out_specs=pl.BlockSpec((tq, HEAD_DIM), lambda b, h, i: (b * nq + i, h)),
        out_shape=jax.ShapeDtypeStruct((T, W), BF16),
        scratch_shapes=[pltpu.VMEM((HEAD_DIM + ONES_ROWS, seq), BF16),
                        pltpu.VMEM((1, tq), F32), pltpu.VMEM((HEAD_DIM + ONES_ROWS, tq), F32),
                        pltpu.VMEM((1, tq), F32), pltpu.VMEM((HEAD_DIM + ONES_ROWS, tq), F32)],
        compiler_params=_params(("parallel", "parallel", "arbitrary"), blk,
                                _nbytes((HEAD_DIM + ONES_ROWS, seq), BF16)
                                + 8 * _nbytes((max(ATTN_FAR_KEYS, 2 * tq), tq), F32)),
        name="diff_attention",
    )(z, z, z, near_bias, c_lambda.astype(F32), c_norm.reshape(HEAD_DIM, 1).astype(F32))


def _merge_kernel(ya_ref, yb_ref, yc_ref, wa_hbm, wb_hbm, wc_hbm, g0_ref, g1_ref, g2_ref, o_ref,
                  wa_land, wb_land, wc_land, wa_scr, wb_scr, wc_scr, sems, *, layer, tn):
    j = pl.program_id(0)
    lands = (wa_land, wb_land, wc_land)

    def copies(jj):
        cols = pl.ds(pl.multiple_of(jj * tn, LANES), tn)
        return [pltpu.make_async_copy(w.at[layer, :, cols], land.at[0], sems.at[n])
                for n, (w, land) in enumerate(zip((wa_hbm, wb_hbm, wc_hbm), lands))]

    @pl.when(pl.program_id(1) == 0)
    def _():
        _stationary_weights(j, pl.num_programs(0), copies, lands, (wa_scr, wb_scr, wc_scr))

    da = jnp.dot(ya_ref[...], wa_scr[...], preferred_element_type=F32)
    db = jnp.dot(yb_ref[...], wb_scr[...], preferred_element_type=F32)
    dc = jnp.dot(yc_ref[...], wc_scr[...], preferred_element_type=F32)
    merged = (_sigmoid(g0_ref[...].astype(F32)) * da + _sigmoid(g1_ref[...].astype(F32)) * db
              + _sigmoid(g2_ref[...].astype(F32)) * dc)
    o_ref[...] = merged.astype(o_ref.dtype)


def gated_merge(y_a, y_b, y_c, w_a, w_b, w_c, layer, z, gate_col0, d_model):
    T = z.shape[0]
    goff = gate_col0 * HEAD_DIM
    tn = _divisor(math.gcd(goff, d_model), (512, 256, 128))
    ka, kb, kc = y_a.shape[1], y_b.shape[1], y_c.shape[1]
    ks = ka + kb + kc

    def blk(tm):
        return _nbytes((tm, ks), BF16) + 4 * _nbytes((tm, tn), BF16)

    def scr(tm):
        return _nbytes((ks, tn), F32) + _nbytes((ks, tn), BF16) + 4 * _nbytes((tm, tn), F32)

    tm = _row_tile(T, blk, scr)

    def gspec(j):
        base = (goff + j * d_model) // tn
        return pl.BlockSpec((tm, tn), lambda n, i: (i, base + n))

    hbm = pl.BlockSpec(memory_space=pl.ANY)
    lands = [pltpu.VMEM((1, k, tn), F32) for k in (ka, kb, kc)]
    scrs = [pltpu.VMEM((k, tn), BF16) for k in (ka, kb, kc)]
    return pl.pallas_call(
        functools.partial(_merge_kernel, layer=layer, tn=tn),
        grid=(d_model // tn, T // tm),
        in_specs=[pl.BlockSpec((tm, ka), lambda n, i: (i, 0)),
                  pl.BlockSpec((tm, kb), lambda n, i: (i, 0)),
                  pl.BlockSpec((tm, kc), lambda n, i: (i, 0)),
                  hbm, hbm, hbm, gspec(0), gspec(1), gspec(2)],
        out_specs=pl.BlockSpec((tm, tn), lambda n, i: (i, n)),
        out_shape=jax.ShapeDtypeStruct((T, d_model), BF16),
        scratch_shapes=lands + scrs + [pltpu.SemaphoreType.DMA((3,))],
        compiler_params=_params(("arbitrary", "arbitrary"), blk(tm), scr(tm)),
        name="gated_merge",
    )(y_a, y_b, y_c, w_a, w_b, w_c, z, z, z)


def _route_kernel(l_ref, idx_ref, w_ref, *, n_experts):
    logits = l_ref[...]
    lane = lax.broadcasted_iota(jnp.int32, logits.shape, 1)
    logits = jnp.where(lane < n_experts, logits, -jnp.inf)
    m1 = jnp.max(logits, axis=-1, keepdims=True)
    i1 = jnp.min(jnp.where(logits == m1, lane, LANES), axis=-1, keepdims=True)
    rest = jnp.where(lane == i1, -jnp.inf, logits)
    m2 = jnp.max(rest, axis=-1, keepdims=True)
    i2 = jnp.min(jnp.where(rest == m2, lane, LANES), axis=-1, keepdims=True)
    e = jnp.exp(m2 - m1)
    w1 = 1.0 / (1.0 + e)
    w2 = e / (1.0 + e)
    idx_ref[...] = jnp.where(lane == 0, i1, jnp.where(lane == 1, i2, 0))
    w_ref[...] = jnp.where(lane == 0, w1, jnp.where(lane == 1, w2, 0.0))


def route_top2(logits, n_experts):
    T = logits.shape[0]
    tm = _divisor(T, (1024, 512, 256, 128, 8))
    spec = pl.BlockSpec((tm, LANES), lambda i: (i, 0))
    idx, w = pl.pallas_call(
        functools.partial(_route_kernel, n_experts=n_experts),
        grid=(T // tm,),
        in_specs=[spec],
        out_specs=[spec, spec],
        out_shape=[jax.ShapeDtypeStruct((T, LANES), jnp.int32), jax.ShapeDtypeStruct((T, LANES), F32)],
        compiler_params=_params(("parallel",), 3 * _nbytes((tm, LANES), F32)),
        name="route_top2",
    )(logits)
    return idx[:, :TOP_K], w


def _moe_gather_kernel(src_ref, nv_ref, x_hbm, o_ref, buf, sem, *, tg):
    i = pl.program_id(0)
    n_valid = nv_ref[0]
    slot = i % 2

    def row_copy(tile, r, sl):
        tok = src_ref[tile * tg + r]
        return pltpu.make_async_copy(x_hbm.at[pl.ds(tok, 1), :], buf.at[sl, pl.ds(r, 1), :], sem.at[sl])

    def start_tile(tile, sl):
        def body(g, carry):
            for u in range(DMA_LOOP_UNROLL):
                row_copy(tile, g * DMA_LOOP_UNROLL + u, sl).start(priority=u % 2)
            return carry
        lax.fori_loop(0, tg // DMA_LOOP_UNROLL, body, 0)

    @pl.when(i == 0)
    def _():
        start_tile(0, 0)

    @pl.when(i + 1 < n_valid)
    def _():
        start_tile(i + 1, 1 - slot)

    @pl.when(i < n_valid)
    def _():
        def body(r, carry):
            row_copy(i, r, slot).wait()
            return carry
        lax.fori_loop(0, tg, body, 0, unroll=DMA_LOOP_UNROLL)
        o_ref[...] = buf[slot].astype(o_ref.dtype)

    @pl.when(i >= n_valid)
    def _():
        o_ref[...] = jnp.zeros_like(o_ref)


def _weight_window(w_hbm, e, j, tn):
    return w_hbm.at[e, :, pl.ds(pl.multiple_of(j * tn, tn), tn)]


def _grouped_weights(s, sched, w_hbms, lands, scrs, sems, tn):
    ew_ref, jw_ref, fl_ref, ne_ref, nj_ref = sched

    def copies(e, j):
        return [pltpu.make_async_copy(_weight_window(w, e, j, tn), land.at[0], sems.at[k])
                for k, (w, land) in enumerate(zip(w_hbms, lands))]

    @pl.when(fl_ref[s] == 2)
    def _():
        @pl.when(s == 0)
        def _():
            for c in copies(ew_ref[0], jw_ref[0]):
                c.start()

        for c in copies(ew_ref[s], jw_ref[s]):
            c.wait()
        for land, scr in zip(lands, scrs):
            _stage_weight(land, scr)

        @pl.when(ne_ref[s] >= 0)
        def _():
            for c in copies(ne_ref[s], nj_ref[s]):
                c.start()


def _moe_up_kernel(t_ref, jo_ref, ew_ref, jw_ref, fl_ref, ne_ref, nj_ref, x_ref, wg_hbm, wu_hbm, o_ref,
                   wg_land, wu_land, wg_scr, wu_scr, sems, *, tf):
    s = pl.program_id(0)
    _grouped_weights(s, (ew_ref, jw_ref, fl_ref, ne_ref, nj_ref), (wg_hbm, wu_hbm), (wg_land, wu_land),
                     (wg_scr, wu_scr), sems, tf)

    @pl.when(fl_ref[s] > 0)
    def _():
        x = x_ref[...]
        g = jnp.dot(x, wg_scr[...], preferred_element_type=F32)
        u = jnp.dot(x, wu_scr[...], preferred_element_type=F32)
        o_ref[...] = (g * _sigmoid(g) * u).astype(o_ref.dtype)

    @pl.when(fl_ref[s] == 0)
    def _():
        o_ref[...] = jnp.zeros_like(o_ref)


def _moe_down_kernel(t_ref, jo_ref, ew_ref, jw_ref, fl_ref, ne_ref, nj_ref, a_ref, wd_hbm, o_ref,
                     wd_land, wd_scr, sems, *, tn):
    s = pl.program_id(0)
    _grouped_weights(s, (ew_ref, jw_ref, fl_ref, ne_ref, nj_ref), (wd_hbm,), (wd_land,), (wd_scr,), sems, tn)

    @pl.when(fl_ref[s] > 0)
    def _():
        o_ref[...] = jnp.dot(a_ref[...], wd_scr[...], preferred_element_type=F32)

    @pl.when(fl_ref[s] == 0)
    def _():
        o_ref[...] = jnp.zeros_like(o_ref)


def _moe_schedule(tiles_per_expert, n_tiles, n_cols):
    E = tiles_per_expert.shape[0]
    cnt = jnp.concatenate([tiles_per_expert, (n_tiles - jnp.sum(tiles_per_expert))[None]]).astype(jnp.int32)
    tile0 = jnp.cumsum(cnt) - cnt
    step_end = jnp.cumsum(cnt * n_cols)
    step0 = step_end - cnt * n_cols
    s = jnp.arange(n_tiles * n_cols, dtype=jnp.int32)
    g = jnp.sum((s[:, None] >= step_end[None, :]).astype(jnp.int32), axis=1)
    within = s - step0[g]
    c = jnp.maximum(cnt[g], 1)
    col = within // c
    tile = tile0[g] + within % c
    valid = g < E
    first = jnp.logical_and(valid, within % c == 0)
    n_valid = step_end[E - 1]
    last = jnp.maximum(n_valid - 1, 0)
    w_e = jnp.where(valid, g, g[last])
    w_j = jnp.where(valid, col, col[last])
    flag = jnp.where(first, 2, jnp.where(valid, 1, 0))
    nxt = s + c
    has_next = jnp.logical_and(first, nxt < n_valid)
    nxt = jnp.minimum(nxt, n_tiles * n_cols - 1)
    n_e = jnp.where(has_next, g[nxt], -1)
    n_j = jnp.where(has_next, col[nxt], 0)
    return [a.astype(jnp.int32) for a in (tile, col, w_e, w_j, flag, n_e, n_j)]


def _moe_combine_kernel(dest_ref, y_hbm, h_ref, w_ref, *rest, tm, final_norm):
    if final_norm:
        g_ref, o_ref, buf0, buf1, sem = rest
    else:
        o_ref, buf0, buf1, sem = rest
    i = pl.program_id(0)

    def row_copy(r, k, row):
        buf = buf0 if k == 0 else buf1
        return pltpu.make_async_copy(y_hbm.at[pl.ds(row, 1), :], buf.at[pl.ds(r, 1), :], sem.at[k])

    def start(r, carry):
        a = TOP_K * (i * tm + r)
        row_copy(r, 0, dest_ref[a]).start(priority=0)
        row_copy(r, 1, dest_ref[a + 1]).start(priority=1)
        return carry

    def wait(r, carry):
        row_copy(r, 0, 0).wait()
        row_copy(r, 1, 0).wait()
        return carry

    lax.fori_loop(0, tm, start, 0, unroll=DMA_LOOP_UNROLL)
    lax.fori_loop(0, tm, wait, 0, unroll=DMA_LOOP_UNROLL)
    w = w_ref[...]
    out = h_ref[...] + (w[:, 0:1] * buf0[...] + w[:, 1:2] * buf1[...])
    if final_norm:
        out = out * lax.rsqrt(jnp.mean(out * out, axis=-1, keepdims=True) + EPS) * g_ref[...]
    o_ref[...] = out


def moe_ffn(h, hn, logits, wg, wu, wd, layer, final_gain=None):
    T, D = hn.shape
    E, F = wg.shape[1], wg.shape[3]
    idx, wts = route_top2(logits, E)
    tm = _divisor(T, (512, 256, 128))
    A = T * TOP_K
    n_tiles = A // tm + E

    e_flat = idx.reshape(-1)
    onehot = (e_flat[:, None] == jnp.arange(E, dtype=jnp.int32)[None, :]).astype(jnp.int32)
    csum = jnp.cumsum(onehot, axis=0)
    pos = jnp.take_along_axis(csum, e_flat[:, None], axis=1)[:, 0] - 1
    counts = csum[-1]
    padded = ((counts + tm - 1) // tm) * tm
    ends = jnp.cumsum(padded)
    starts = ends - padded
    dest = (starts[e_flat] + pos).astype(jnp.int32)
    src = jnp.zeros((n_tiles * tm,), jnp.int32).at[dest].set(jnp.arange(A, dtype=jnp.int32) // TOP_K)
    tiles_per_expert = padded // tm
    n_valid_tiles = jnp.sum(tiles_per_expert).astype(jnp.int32).reshape(1)

    x_sorted = pl.pallas_call(
        functools.partial(_moe_gather_kernel, tg=tm),
        grid_spec=pltpu.PrefetchScalarGridSpec(
            num_scalar_prefetch=2,
            grid=(n_tiles,),
            in_specs=[pl.BlockSpec(memory_space=pl.ANY)],
            out_specs=pl.BlockSpec((tm, D), lambda i, sr, nv: (i, 0)),
            scratch_shapes=[pltpu.VMEM((2, tm, D), F32), pltpu.SemaphoreType.DMA((2,))],
        ),
        out_shape=jax.ShapeDtypeStruct((n_tiles * tm, D), BF16),
        compiler_params=_params(("arbitrary",), _nbytes((tm, D), BF16), 3 * _nbytes((tm, D), F32)),
        name="moe_gather",
    )(src, n_valid_tiles, hn)

    def experts(w):
        return w.reshape((-1,) + w.shape[2:])

    tf = _divisor(F, (512, 256, 128))
    sched = _moe_schedule(tiles_per_expert, n_tiles, F // tf)
    sched[2] = sched[2] + layer * E
    sched[5] = jnp.where(sched[5] >= 0, sched[5] + layer * E, -1)
    hbm = pl.BlockSpec(memory_space=pl.ANY)
    act = pl.pallas_call(
        functools.partial(_moe_up_kernel, tf=tf),
        grid_spec=pltpu.PrefetchScalarGridSpec(
            num_scalar_prefetch=7,
            grid=(n_tiles * (F // tf),),
            in_specs=[pl.BlockSpec((tm, D), lambda s, t, jo, *_: (t[s], 0)), hbm, hbm],
            out_specs=pl.BlockSpec((tm, tf), lambda s, t, jo, *_: (t[s], jo[s])),
            scratch_shapes=[pltpu.VMEM((1, D, tf), F32), pltpu.VMEM((1, D, tf), F32),
                            pltpu.VMEM((D, tf), BF16), pltpu.VMEM((D, tf), BF16), pltpu.SemaphoreType.DMA((2,))],
        ),
        out_shape=jax.ShapeDtypeStruct((n_tiles * tm, F), BF16),
        compiler_params=_params(("arbitrary",), _nbytes((tm, D), BF16) + _nbytes((tm, tf), BF16),
                                2 * _nbytes((D, tf), F32) + 2 * _nbytes((D, tf), BF16) + 3 * _nbytes((tm, tf), F32)),
        name="moe_up",
    )(*sched, x_sorted, experts(wg), experts(wu))

    tn = _divisor(D, (1024, 512, 256, 128))
    sched = _moe_schedule(tiles_per_expert, n_tiles, D // tn)
    sched[2] = sched[2] + layer * E
    sched[5] = jnp.where(sched[5] >= 0, sched[5] + layer * E, -1)
    y_sorted = pl.pallas_call(
        functools.partial(_moe_down_kernel, tn=tn),
        grid_spec=pltpu.PrefetchScalarGridSpec(
            num_scalar_prefetch=7,
            grid=(n_tiles * (D // tn),),
            in_specs=[pl.BlockSpec((tm, F), lambda s, t, jo, *_: (t[s], 0)), hbm],
            out_specs=pl.BlockSpec((tm, tn), lambda s, t, jo, *_: (t[s], jo[s])),
            scratch_shapes=[pltpu.VMEM((1, F, tn), F32), pltpu.VMEM((F, tn), BF16), pltpu.SemaphoreType.DMA((1,))],
        ),
        out_shape=jax.ShapeDtypeStruct((n_tiles * tm, D), F32),
        compiler_params=_params(("arbitrary",), _nbytes((tm, F), BF16) + _nbytes((tm, tn), F32),
                                _nbytes((F, tn), F32) + _nbytes((F, tn), BF16) + _nbytes((tm, tn), F32)),
        name="moe_down",
    )(*sched, act, experts(wd))

    tc = _divisor(T, (256, 128))
    blk_c = 2 * _nbytes((tc, D), F32) + _nbytes((tc, LANES), F32)
    in_specs = [hbm, pl.BlockSpec((tc, D), lambda i, de: (i, 0)), pl.BlockSpec((tc, LANES), lambda i, de: (i, 0))]
    args = [dest, y_sorted, h, wts]
    if final_gain is not None:
        in_specs.append(pl.BlockSpec((1, D), lambda i, de: (0, 0)))
        args.append(final_gain.reshape(1, D).astype(F32))
    return pl.pallas_call(
        functools.partial(_moe_combine_kernel, tm=tc, final_norm=final_gain is not None),
        grid_spec=pltpu.PrefetchScalarGridSpec(
            num_scalar_prefetch=1,
            grid=(T // tc,),
            in_specs=in_specs,
            out_specs=pl.BlockSpec((tc, D), lambda i, de: (i, 0)),
            scratch_shapes=[pltpu.VMEM((tc, D), F32), pltpu.VMEM((tc, D), F32), pltpu.SemaphoreType.DMA((2,))],
        ),
        out_shape=jax.ShapeDtypeStruct((T, D), F32),
        compiler_params=_params(("arbitrary",), blk_c, 3 * _nbytes((tc, D), F32)),
        name="moe_combine",
    )(*args)


def _mixer(h, li, batch, seq, near_bias, tq, norm_mix, w_in, a_gate_bias, a_conv, a_norm, b_ln_g, b_ln_b, b_ws,
           b_bs, c_lambda, c_norm, w_br_a, w_br_b, w_br_c, w_out):
    d_model = h.shape[1]
    h_a = a_gate_bias.shape[1] // 2
    g_b = b_ws.shape[1]
    h_c = w_br_c.shape[1] // HEAD_DIM
    w_a = h_a * HEAD_DIM
    g0 = 4 * w_a
    g1 = g0 + 2 * h_a
    n_rest = w_in.shape[2] - g1
    w_in_t = jnp.swapaxes(w_in, 1, 2)
    n, zg = rmsnorm(h, norm_mix[li], proj=jnp.transpose(w_in_t[li, g0:g1, :]))
    z_a = matmul(n, w_in_t, li, BF16, n=g0, transposed=True)
    z_r = matmul(n, w_in_t, li, BF16, w_col=g1, n=n_rest, transposed=True)
    y_a = mlstm_branch(z_a, zg, a_gate_bias[li], a_conv[li], a_norm[li], batch, seq, h_a)
    y_b = sgu_branch(z_r, b_ln_g[li], b_ln_b[li], b_ws[li], b_bs[li], g_b)
    lam_init = 0.8 - 0.6 * math.exp(-0.3 * li)
    y_c = attn_branch(z_r, 2 * g_b, near_bias, c_lambda[li], c_norm[li], batch, seq, h_c, lam_init, tq)
    merged = gated_merge(y_a, y_b, y_c, w_br_a, w_br_b, w_br_c, li, z_r, 2 * g_b + 3 * h_c, d_model)
    return matmul_residual(merged, w_out, li, h)


def kernel(x, norm_mix, w_in, a_gate_bias, a_conv, a_norm, b_ln_g, b_ln_b, b_ws, b_bs, c_lambda, c_norm, rel_bias,
           w_br_a, w_br_b, w_br_c, w_out, norm_ffn, ffn_wg, ffn_wu, ffn_wd, router, moe_wg, moe_wu, moe_wd,
           final_norm):
    batch, seq, d_model = x.shape
    depth = w_in.shape[0]
    tq = _divisor(seq, (512, 256, 128))
    near_bias = attn_bias_tiles(rel_bias, tq)
    h = x.reshape(batch * seq, d_model).astype(F32)
    for li in range(depth):
        h = _mixer(h, li, batch, seq, near_bias, tq, norm_mix, w_in, a_gate_bias, a_conv, a_norm, b_ln_g, b_ln_b,
                   b_ws, b_bs, c_lambda, c_norm, w_br_a, w_br_b, w_br_c, w_out)
        j = li // 2
        fused_final = False
        if li % 2 == 0:
            hn = rmsnorm(h, norm_ffn[li])
            act = swiglu_up(hn, ffn_wg, ffn_wu, j)
            h = matmul_residual(act, ffn_wd, j, h)
        else:
            hn, logits = rmsnorm(h, norm_ffn[li], proj=router[j], out_dtype=F32)
            fused_final = li == depth - 1
            h = moe_ffn(h, hn, logits, moe_wg, moe_wu, moe_wd, j, final_norm if fused_final else None)
    out = h if fused_final else rmsnorm(h, final_norm, out_dtype=F32)
    return out.reshape(batch, seq, d_model).astype(x.dtype)
```
